```python
import math
import jax, jax.numpy as jnp
from jax import lax
import numpy as np

D_MODEL = 1024
BATCH = 4
SEQ = 4096
DEPTH = 2

N_EVEN = (DEPTH + 1) // 2
N_ODD = DEPTH // 2
RMS_EPS = 1e-6
ROPE_THETA = 10000.0
Q_BLOCK = 128
GRID_W = 64

D_FF = 2816

A_HEADS = 4
A_DH = 64
A_QK = A_HEADS * 2 * A_DH
A_V = A_HEADS * 2 * A_DH

B_HEADS = 8
B_DH = 64
B_W = B_HEADS * B_DH
NA_ROWS = 8
NA_COLS = 16

C_WIDTH = 512
C_BLOCKS = 8
C_BW = C_WIDTH // C_BLOCKS
CONV_W = 4
RG_C = 8.0

D_HEADS = 8
D_NOPE = 64
D_ROPE = 32
D_VDIM = 64
D_Q_RANK = 256
D_KV_RANK = 128

EVEN_IN = 2 * A_QK + A_V + 3 * B_W
EVEN_MIX = A_V + B_W
ODD_IN = 2 * C_WIDTH + D_Q_RANK + D_KV_RANK + D_ROPE
ODD_MIX = C_WIDTH + D_HEADS * D_VDIM

kernel_name = "hybrid_diff_na_rglru_mla_encoder"


def rmsnorm(x, g):
    xf = x.astype(jnp.float32)
    y = xf * lax.rsqrt(jnp.mean(xf * xf, axis=-1, keepdims=True) + RMS_EPS)
    return (y * g.astype(jnp.float32)).astype(x.dtype)


def swiglu(x, wg, wu, wd):
    return (jax.nn.silu(x @ wg) * (x @ wu)) @ wd


def split_cols(x, sizes):
    idx, acc = [], 0
    for s in sizes[:-1]:
        acc += s
        idx.append(acc)
    return jnp.split(x, idx, axis=-1)


def rope_tables(seq, dim):
    inv = 1.0 / (ROPE_THETA ** (jnp.arange(0, dim, 2, dtype=jnp.float32) / dim))
    ang = jnp.arange(seq, dtype=jnp.float32)[:, None] * inv[None, :]
    return jnp.cos(ang), jnp.sin(ang)


def apply_rope(x, cos, sin):
    half = cos.shape[1]
    shape = (1, cos.shape[0]) + (1,) * (x.ndim - 3) + (half,)
    c = cos.reshape(shape).astype(x.dtype)
    s = sin.reshape(shape).astype(x.dtype)
    x1, x2 = x[..., :half], x[..., half:]
    return jnp.concatenate([x1 * c - x2 * s, x1 * s + x2 * c], axis=-1)


def blocked_attention(q, k, v, scale):
    B_, S, H, Dk = q.shape
    Dv = v.shape[-1]
    nb = S // Q_BLOCK
    qb = q.reshape(B_, nb, Q_BLOCK, H, Dk).swapaxes(0, 1)

    def block(qi):
        s = jnp.einsum('bqhd,bkhd->bhqk', qi, k).astype(jnp.float32) * scale
        p = jax.nn.softmax(s, axis=-1)
        return jnp.einsum('bhqk,bkhe->bqhe', p.astype(v.dtype), v)

    o = lax.map(block, qb)
    return o.swapaxes(0, 1).reshape(B_, S, H * Dv)


def diff_attention(q, k, v, lam, subln_g, lam_init):
    B_, S, H = q.shape[:3]
    nb = S // Q_BLOCK
    scale = A_DH ** -0.5
    qb = q.reshape(B_, nb, Q_BLOCK, H, 2, A_DH).swapaxes(0, 1)

    def block(qi):
        s = jnp.einsum('bqhmd,bkhmd->bhmqk', qi, k).astype(jnp.float32) * scale
        p = jax.nn.softmax(s, axis=-1)
        w = p[:, :, 0] - lam * p[:, :, 1]
        return jnp.einsum('bhqk,bkhe->bqhe', w.astype(v.dtype), v)

    o = lax.map(block, qb).swapaxes(0, 1).reshape(B_, S, H, 2 * A_DH)
    o = rmsnorm(o, subln_g) * (1.0 - lam_init)
    return o.reshape(B_, S, H * 2 * A_DH)


def neighbourhood_attention(q, k, v, rpb):
    B_, S, H, dh = q.shape
    rows = S // GRID_W
    wr = min(NA_ROWS, rows)
    wc = NA_COLS
    r = jnp.arange(rows)
    r0 = jnp.clip(r - wr // 2, 0, rows - wr)
    key_rows = r0[:, None] + jnp.arange(wr)[None, :]
    c = jnp.arange(GRID_W)
    c0 = jnp.clip(c - wc // 2, 0, GRID_W - wc)
    col_in = (c[None, :] >= c0[:, None]) & (c[None, :] < c0[:, None] + wc)
    qg = q.reshape(B_, rows, GRID_W, H, dh)
    kg = k.reshape(B_, rows, GRID_W, H, dh)[:, key_rows]
    vg = v.reshape(B_, rows, GRID_W, H, dh)[:, key_rows]
    s = jnp.einsum('brqhd,brjkhd->bhrqjk', qg, kg).astype(jnp.float32) * (dh ** -0.5)
    dr = key_rows - r[:, None]
    dc = jnp.clip(c[None, :] - c[:, None], -(wc - 1), wc - 1)
    idx_r = (dr + NA_ROWS - 1)[:, None, :, None]
    idx_c = (dc + NA_COLS - 1)[None, :, None, :]
    bias = rpb.astype(jnp.float32)[:, idx_r, idx_c]
    s = jnp.where(col_in[:, None, :], s + bias[None], -jnp.inf)
    sh = s.shape
    p = jax.nn.softmax(s.reshape(sh[:4] + (wr * GRID_W,)), axis=-1).reshape(sh)
    o = jnp.einsum('bhrqjk,brjkhd->brqhd', p.astype(v.dtype), vg)
    return o.reshape(B_, S, H * dh)


def centred_conv(x, w, b):
    S = x.shape[1]
    lp = (CONV_W - 1) // 2
    rp = CONV_W - 1 - lp
    xp = jnp.pad(x, ((0, 0), (lp, rp), (0, 0)))
    return sum(xp[:, j:j + S] * w[j] for j in range(CONV_W)) + b


def rg_lru(x, w_a, b_a, w_x, b_x, lam, reverse):
    B_, S, C = x.shape
    xb = x.reshape(B_, S, C_BLOCKS, C_BW)
    r_t = jax.nn.sigmoid((jnp.einsum('bsnc,ncd->bsnd', xb, w_a).reshape(B_, S, C) + b_a).astype(jnp.float32))
    i_t = jax.nn.sigmoid((jnp.einsum('bsnc,ncd->bsnd', xb, w_x).reshape(B_, S, C) + b_x).astype(jnp.float32))
    log_a = -RG_C * r_t * jax.nn.softplus(-lam.astype(jnp.float32))
    a = jnp.exp(log_a)
    mult = jnp.sqrt(-jnp.expm1(2.0 * log_a))
    first = jnp.arange(S) == (S - 1 if reverse else 0)
    mult = jnp.where(first[None, :, None], 1.0, mult)
    bterm = mult * i_t * x.astype(jnp.float32)

    def comb(l, rr):
        a_l, b_l = l
        a_r, b_r = rr
        return a_l * a_r, a_r * b_l + b_r

    _, h = lax.associative_scan(comb, (a, bterm), reverse=reverse, axis=1)
    return h


def mla_attention(cq, ckv, kr, gq, gkv, wuq, wukv, cos, sin):
    B_, S, _ = cq.shape
    q = (rmsnorm(cq, gq) @ wuq).reshape(B_, S, D_HEADS, D_NOPE + D_ROPE)
    q = jnp.concatenate([q[..., :D_NOPE], apply_rope(q[..., D_NOPE:], cos, sin)], axis=-1)
    kv = (rmsnorm(ckv, gkv) @ wukv).reshape(B_, S, D_HEADS, D_NOPE + D_VDIM)
    k_pe = apply_rope(kr[:, :, None, :], cos, sin)
    k = jnp.concatenate([kv[..., :D_NOPE], jnp.broadcast_to(k_pe, (B_, S, D_HEADS, D_ROPE))], axis=-1)
    v = kv[..., D_NOPE:]
    return blocked_attention(q, k, v, (D_NOPE + D_ROPE) ** -0.5)


def even_mixer(h, w_in, w_out, lam_vec, subln_g, rpb, lam_init, cos, sin):
    B_, S, _ = h.shape
    qa, ka, va, qb, kb, vb = split_cols(h @ w_in, [A_QK, A_QK, A_V, B_W, B_W, B_W])
    qa = apply_rope(qa.reshape(B_, S, 2 * A_HEADS, A_DH), cos, sin).reshape(B_, S, A_HEADS, 2, A_DH)
    ka = apply_rope(ka.reshape(B_, S, 2 * A_HEADS, A_DH), cos, sin).reshape(B_, S, A_HEADS, 2, A_DH)
    lf = lam_vec.astype(jnp.float32)
    lam = jnp.exp(jnp.sum(lf[0] * lf[1])) - jnp.exp(jnp.sum(lf[2] * lf[3])) + lam_init
    oa = diff_attention(qa, ka, va.reshape(B_, S, A_HEADS, 2 * A_DH), lam, subln_g, lam_init)
    ob = neighbourhood_attention(qb.reshape(B_, S, B_HEADS, B_DH), kb.reshape(B_, S, B_HEADS, B_DH),
                                 vb.reshape(B_, S, B_HEADS, B_DH), rpb)
    return jnp.concatenate([oa, ob], axis=-1) @ w_out


def odd_mixer(h, w_in, w_out, conv_w, conv_b, rg_wa, rg_ba, rg_wx, rg_bx, rg_lam,
              gq, gkv, wuq, wukv, cos, sin):
    xc, gc, cq, ckv, kr = split_cols(h @ w_in, [C_WIDTH, C_WIDTH, D_Q_RANK, D_KV_RANK, D_ROPE])
    u = centred_conv(xc, conv_w, conv_b)
    hr = (rg_lru(u, rg_wa[0], rg_ba[0], rg_wx[0], rg_bx[0], rg_lam[0], False)
          + rg_lru(u, rg_wa[1], rg_ba[1], rg_wx[1], rg_bx[1], rg_lam[1], True))
    oc = jax.nn.gelu(gc) * hr.astype(h.dtype)
    od = mla_attention(cq, ckv, kr, gq, gkv, wuq, wukv, cos, sin)
    return jnp.concatenate([oc, od], axis=-1) @ w_out


def setup_inputs(seed: int = 0) -> dict:
    key = jax.random.key(seed)
    ks = iter(jax.random.split(key, 40))

    def nrm(shape, scale):
        return jax.random.normal(next(ks), shape, jnp.float32) * scale

    def gain(shape):
        return 1.0 + nrm(shape, 0.02)

    u = jax.random.uniform(next(ks), (N_ODD, 2, C_WIDTH), jnp.float32, 0.9, 0.999)
    a0 = u ** (1.0 / RG_C)
    return {
        "x": nrm((BATCH, SEQ, D_MODEL), 1.0),
        "ffn1_norm": gain((DEPTH, D_MODEL)),
        "ffn1_wg": nrm((DEPTH, D_MODEL, D_FF), D_MODEL ** -0.5),
        "ffn1_wu": nrm((DEPTH, D_MODEL, D_FF), D_MODEL ** -0.5),
        "ffn1_wd": nrm((DEPTH, D_FF, D_MODEL), D_FF ** -0.5),
        "mix_norm": gain((DEPTH, D_MODEL)),
        "ffn2_norm": gain((DEPTH, D_MODEL)),
        "ffn2_wg": nrm((DEPTH, D_MODEL, D_FF), D_MODEL ** -0.5),
        "ffn2_wu": nrm((DEPTH, D_MODEL, D_FF), D_MODEL ** -0.5),
        "ffn2_wd": nrm((DEPTH, D_FF, D_MODEL), D_FF ** -0.5),
        "final_norm": gain((D_MODEL,)),
        "ev_w_in": nrm((N_EVEN, D_MODEL, EVEN_IN), D_MODEL ** -0.5),
        "ev_w_out": nrm((N_EVEN, EVEN_MIX, D_MODEL), EVEN_MIX ** -0.5),
        "diff_lam": nrm((N_EVEN, 4, A_DH), 0.1),
        "diff_subln": gain((N_EVEN, 2 * A_DH)),
        "na_rpb": nrm((N_EVEN, B_HEADS, 2 * NA_ROWS - 1, 2 * NA_COLS - 1), 0.02),
        "od_w_in": nrm((N_ODD, D_MODEL, ODD_IN), D_MODEL ** -0.5),
        "od_w_out": nrm((N_ODD, ODD_MIX, D_MODEL), ODD_MIX ** -0.5),
        "conv_w": nrm((N_ODD, CONV_W, C_WIDTH), CONV_W ** -0.5),
        "conv_b": nrm((N_ODD, C_WIDTH), 0.01),
        "rg_wa": nrm((N_ODD, 2, C_BLOCKS, C_BW, C_BW), C_BW ** -0.5),
        "rg_ba": nrm((N_ODD, 2, C_WIDTH), 0.01),
        "rg_wx": nrm((N_ODD, 2, C_BLOCKS, C_BW, C_BW), C_BW ** -0.5),
        "rg_bx": nrm((N_ODD, 2, C_WIDTH), 0.01),
        "rg_lam": jnp.log(a0) - jnp.log1p(-a0),
        "mla_gq": gain((N_ODD, D_Q_RANK)),
        "mla_gkv": gain((N_ODD, D_KV_RANK)),
        "mla_wuq": nrm((N_ODD, D_Q_RANK, D_HEADS * (D_NOPE + D_ROPE)), D_Q_RANK ** -0.5),
        "mla_wukv": nrm((N_ODD, D_KV_RANK, D_HEADS * (D_NOPE + D_VDIM)), D_KV_RANK ** -0.5),
    }


def reference(x, ffn1_norm, ffn1_wg, ffn1_wu, ffn1_wd, mix_norm, ffn2_norm, ffn2_wg, ffn2_wu,
              ffn2_wd, final_norm, ev_w_in, ev_w_out, diff_lam, diff_subln, na_rpb, od_w_in,
              od_w_out, conv_w, conv_b, rg_wa, rg_ba, rg_wx, rg_bx, rg_lam, mla_gq, mla_gkv,
              mla_wuq, mla_wukv):
    S = x.shape[1]
    cos_a, sin_a = rope_tables(S, A_DH)
    cos_d, sin_d = rope_tables(S, D_ROPE)
    for l in range(DEPTH):
        x = x + 0.5 * swiglu(rmsnorm(x, ffn1_norm[l]), ffn1_wg[l], ffn1_wu[l], ffn1_wd[l])
        h = rmsnorm(x, mix_norm[l])
        if l % 2 == 0:
            i = l // 2
            lam_init = 0.8 - 0.6 * math.exp(-0.3 * l)
            x = x + even_mixer(h, ev_w_in[i], ev_w_out[i], diff_lam[i], diff_subln[i], na_rpb[i],
                               lam_init, cos_a, sin_a)
        else:
            i = l // 2
            x = x + odd_mixer(h, od_w_in[i], od_w_out[i], conv_w[i], conv_b[i], rg_wa[i], rg_ba[i],
                              rg_wx[i], rg_bx[i], rg_lam[i], mla_gq[i], mla_gkv[i], mla_wuq[i],
                              mla_wukv[i], cos_d, sin_d)
        x = x + 0.5 * swiglu(rmsnorm(x, ffn2_norm[l]), ffn2_wg[l], ffn2_wu[l], ffn2_wd[l])
    return rmsnorm(x, final_norm)
```

```python
import functools
import math

import jax
import jax.numpy as jnp
from jax import lax
from jax.experimental import pallas as pl
from jax.experimental.pallas import tpu as pltpu

F32 = jnp.float32
BF16 = jnp.bfloat16

D_MODEL = 1024
BATCH = 4
SEQ = 4096
DEPTH = 2
TOKENS = BATCH * SEQ
RMS_EPS = 1e-6
ROPE_THETA = 10000.0
GRID_W = 64
GRID_ROWS = SEQ // GRID_W
D_FF = 2816

A_HEADS = 4
A_DH = 64
A_QK = A_HEADS * 2 * A_DH
A_V = A_HEADS * 2 * A_DH
B_HEADS = 8
B_DH = 64
B_W = B_HEADS * B_DH
NA_ROWS = 8
NA_COLS = 16
C_WIDTH = 512
C_BLOCKS = 8
C_BW = C_WIDTH // C_BLOCKS
CONV_W = 4
RG_C = 8.0
D_HEADS = 8
D_NOPE = 64
D_ROPE = 32
D_VDIM = 64
D_Q_RANK = 256
D_KV_RANK = 128
EVEN_IN = 2 * A_QK + A_V + 3 * B_W
ODD_IN = 2 * C_WIDTH + D_Q_RANK + D_KV_RANK + D_ROPE

LANES = 128
SUBLANES = 8
VMEM_LIMIT_BYTES = 56 * 1024 * 1024

TOKEN_TILE = 512
FF_CHUNK = 256
Q_TILE = 256
PAD_DK = 128

_NT = (((1,), (1,)), ((), ()))


def _params(*sem):
    return pltpu.CompilerParams(dimension_semantics=sem, vmem_limit_bytes=VMEM_LIMIT_BYTES)


def _resident(shape):
    nd = len(shape)
    return pl.BlockSpec(shape, lambda *_: (0,) * nd, pipeline_mode=pl.Buffered(1))


def _rms(x, g):
    return x * lax.rsqrt(jnp.mean(x * x, axis=-1, keepdims=True) + RMS_EPS) * g


def _dot(a, b):
    return jnp.dot(a, b, preferred_element_type=F32)


def _rope_lanes(x, cos, sin_signed, half, lo_mask):
    n = x.shape[-1]
    partner = jnp.where(lo_mask, pltpu.roll(x, n - half, 1), pltpu.roll(x, half, 1))
    return x * cos + partner * sin_signed


def _ffn_kernel(*refs, has_mix, has_final):
    it = iter(refs)
    x_ref = next(it)
    if has_mix:
        o1_ref, o2_ref, wo_ref = next(it), next(it), next(it)
    g_ref, wg_ref, wu_ref, wd_ref = next(it), next(it), next(it), next(it)
    gf_ref = next(it) if has_final else None
    y_ref = next(it)
    acc_ref = next(it)

    x = x_ref[...]
    if has_mix:
        half = wo_ref.shape[0] // 2
        x = x + _dot(o1_ref[...], wo_ref[:half, :]) + _dot(o2_ref[...], wo_ref[half:, :])
    n = _rms(x, g_ref[...]).astype(BF16)
    for c in range(D_FF // FF_CHUNK):
        sl = slice(c * FF_CHUNK, (c + 1) * FF_CHUNK)
        gate = _dot(n, wg_ref[:, sl])
        up = _dot(n, wu_ref[:, sl])
        h = (gate * jax.nn.sigmoid(gate) * up).astype(BF16)
        d = _dot(h, wd_ref[sl, :])
        if c == 0:
            acc_ref[...] = d
        else:
            acc_ref[...] += d
    y = x + 0.5 * acc_ref[...]
    if has_final:
        y = _rms(y, gf_ref[...])
    y_ref[...] = y


def _ffn(x, g, wg, wu, wd, mix=None, final_g=None):
    tm = TOKEN_TILE
    tok = lambda w: pl.BlockSpec((tm, w), lambda i: (i, 0))
    args, specs = [x], [tok(D_MODEL)]
    if mix is not None:
        o1, o2, wo = mix
        args += [o1, o2, wo]
        specs += [tok(o1.shape[1]), tok(o2.shape[1]), _resident(wo.shape)]
    args += [g, wg, wu, wd]
    specs += [_resident(g.shape), _resident(wg.shape), _resident(wu.shape), _resident(wd.shape)]
    if final_g is not None:
        args.append(final_g)
        specs.append(_resident(final_g.shape))
    return pl.pallas_call(
        functools.partial(_ffn_kernel, has_mix=mix is not None, has_final=final_g is not None),
        grid=(TOKENS // tm,),
        in_specs=specs,
        out_specs=tok(D_MODEL),
        out_shape=jax.ShapeDtypeStruct((TOKENS, D_MODEL), F32),
        scratch_shapes=[pltpu.VMEM((tm, D_MODEL), F32)],
        compiler_params=_params("parallel"),
        name="ffn_mix" if mix is not None else "ffn",
    )(*args)


def _even_in_kernel(x_ref, g_ref, w_ref, cos_ref, sin_ref,
                    qa_ref, ka_ref, va_ref, qb_ref, kb_ref, vb_ref):
    h = _rms(x_ref[...], g_ref[...]).astype(BF16)
    cos, sin = cos_ref[...], sin_ref[...]
    lane = lax.broadcasted_iota(jnp.int32, cos.shape, 1)
    lo = (lane % A_DH) < (A_DH // 2)
    scale = A_DH ** -0.5

    def proj(i):
        return _dot(h, w_ref[:, i * A_QK:(i + 1) * A_QK])

    qa, ka = proj(0), proj(1)
    for j in range(A_QK // LANES):
        sl = slice(j * LANES, (j + 1) * LANES)
        qa_ref[:, sl] = (_rope_lanes(qa[:, sl], cos, sin, A_DH // 2, lo) * scale).astype(BF16)
        ka_ref[:, sl] = _rope_lanes(ka[:, sl], cos, sin, A_DH // 2, lo).astype(BF16)
    va_ref[...] = proj(2).astype(BF16)
    qb_ref[...] = (proj(3) * (B_DH ** -0.5)).astype(BF16)
    kb_ref[...] = proj(4).astype(BF16)
    vb_ref[...] = proj(5).astype(BF16)


def _even_in(x, g, w_in, cos, sin):
    tm = TOKEN_TILE
    per_seq = SEQ // tm
    tok = lambda w: pl.BlockSpec((tm, w), lambda i: (i, 0))
    pos = pl.BlockSpec((tm, LANES), lambda i: (i % per_seq, 0))
    out = jax.ShapeDtypeStruct((TOKENS, A_QK), BF16)
    return pl.pallas_call(
        _even_in_kernel,
        grid=(TOKENS // tm,),
        in_specs=[tok(D_MODEL), _resident(g.shape), _resident(w_in.shape), pos, pos],
        out_specs=[tok(A_QK)] * 6,
        out_shape=[out] * 6,
        compiler_params=_params("parallel"),
        name="even_in",
    )(x, g, w_in, cos, sin)


def _diff_attn_kernel(q_ref, k_ref, v_ref, lam_ref, g_ref, o_ref, *, lam_init):
    q = q_ref[...]
    k = k_ref[...]
    lane = lax.broadcasted_iota(jnp.int32, q.shape, 1)
    zero = jnp.zeros_like(q)
    lf = lam_ref[...]
    lam = (jnp.exp(jnp.sum(lf[0:1] * lf[1:2], axis=-1, keepdims=True))
           - jnp.exp(jnp.sum(lf[2:3] * lf[3:4], axis=-1, keepdims=True)) + lam_init)

    def softmax_parts(qm):
        s = lax.dot_general(qm, k, _NT, preferred_element_type=F32)
        e = jnp.exp(s - jnp.max(s, axis=-1, keepdims=True))
        return e, 1.0 / jnp.sum(e, axis=-1, keepdims=True)

    e1, r1 = softmax_parts(jnp.where(lane < A_DH, q, zero))
    e2, r2 = softmax_parts(jnp.where(lane >= A_DH, q, zero))
    w = e1 * r1 - e2 * (r2 * lam)
    o = _dot(w.astype(BF16), v_ref[...])
    o = _rms(o, g_ref[...]) * (1.0 - lam_init)
    o_ref[...] = o.astype(BF16)


def _diff_attn(qa, ka, va, lam_vec, subln_g, lam_init):
    nq = SEQ // Q_TILE
    hw = 2 * A_DH
    qspec = pl.BlockSpec((Q_TILE, hw), lambda b, h, i: (b * nq + i, h))
    kspec = pl.BlockSpec((SEQ, hw), lambda b, h, i: (b, h))
    return pl.pallas_call(
        functools.partial(_diff_attn_kernel, lam_init=lam_init),
        grid=(BATCH, A_HEADS, nq),
        in_specs=[qspec, kspec, kspec, _resident(lam_vec.shape), _resident(subln_g.shape)],
        out_specs=qspec,
        out_shape=jax.ShapeDtypeStruct((TOKENS, A_V), BF16),
        compiler_params=_params("parallel", "parallel", "arbitrary"),
        name="diff_attn",
    )(qa, ka, va, lam_vec, subln_g)


def _na_attn_kernel(q_ref, k_ref, v_ref, bias_ref, o_ref):
    nkeys = NA_ROWS * GRID_W
    lane = lax.broadcasted_iota(jnp.int32, (GRID_W, 2 * B_DH), 1)
    first = lane < B_DH

    def row(r, carry):
        r0 = jnp.clip(r - NA_ROWS // 2, 0, GRID_ROWS - NA_ROWS)
        var = r0 - r + (NA_ROWS - 1)
        q = q_ref[pl.ds(pl.multiple_of(r * GRID_W, GRID_W), GRID_W), :]
        kstart = pl.multiple_of(r0 * GRID_W, GRID_W)
        k = k_ref[pl.ds(kstart, nkeys), :]
        v = v_ref[pl.ds(kstart, nkeys), :]
        zero = jnp.zeros_like(q)
        outs = []
        for j in range(2):
            qj = jnp.where(first if j == 0 else jnp.logical_not(first), q, zero)
            s = lax.dot_general(qj, k, _NT, preferred_element_type=F32) + bias_ref[j, var]
            e = jnp.exp(s - jnp.max(s, axis=-1, keepdims=True))
            inv = 1.0 / jnp.sum(e, axis=-1, keepdims=True)
            outs.append(_dot(e.astype(BF16), v) * inv)
        o = jnp.where(first, outs[0], outs[1])
        o_ref[pl.ds(pl.multiple_of(r * GRID_W, GRID_W), GRID_W), :] = o.astype(BF16)
        return carry

    lax.fori_loop(0, GRID_ROWS, row, 0)


def _na_attn(qb, kb, vb, bias):
    hw = 2 * B_DH
    spec = pl.BlockSpec((SEQ, hw), lambda b, p: (b, p))
    bspec = pl.BlockSpec((2,) + bias.shape[1:], lambda b, p: (p, 0, 0, 0))
    return pl.pallas_call(
        _na_attn_kernel,
        grid=(BATCH, B_HEADS // 2),
        in_specs=[spec, spec, spec, bspec],
        out_specs=spec,
        out_shape=jax.ShapeDtypeStruct((TOKENS, B_W), BF16),
        compiler_params=_params("parallel", "parallel"),
        name="na_attn",
    )(qb, kb, vb, bias)


def _na_bias_table(rpb):
    c = jnp.arange(GRID_W)
    c0 = jnp.clip(c - NA_COLS // 2, 0, GRID_W - NA_COLS)
    col_in = (c[None, :] >= c0[:, None]) & (c[None, :] < c0[:, None] + NA_COLS)
    dc = jnp.clip(c[None, :] - c[:, None], -(NA_COLS - 1), NA_COLS - 1) + NA_COLS - 1
    idx_r = jnp.arange(NA_ROWS)[:, None] + jnp.arange(NA_ROWS)[None, :]
    bias = rpb.astype(F32)[:, idx_r[:, :, None, None], dc[None, None, :, :]]
    bias = jnp.where(col_in[None, None, None], bias, -jnp.inf)
    return bias.transpose(0, 1, 3, 2, 4).reshape(B_HEADS, NA_ROWS, GRID_W, NA_ROWS * GRID_W)


def _odd_in_kernel(x_ref, g_ref, w_ref, gq_ref, gkv_ref, wuq_ref, wk_ref, wv_ref,
                   cos_ref, sin_ref, xc_ref, gc_ref, q_ref, k_ref, v_ref):
    h = _rms(x_ref[...], g_ref[...]).astype(BF16)
    cos, sin = cos_ref[...], sin_ref[...]
    lane = lax.broadcasted_iota(jnp.int32, cos.shape, 1)
    lo = lane < D_NOPE + D_ROPE // 2
    scale = (D_NOPE + D_ROPE) ** -0.5
    o = 0
    xc_ref[...] = _dot(h, w_ref[:, o:o + C_WIDTH]); o += C_WIDTH
    gc_ref[...] = _dot(h, w_ref[:, o:o + C_WIDTH]); o += C_WIDTH
    cq = _dot(h, w_ref[:, o:o + D_Q_RANK]); o += D_Q_RANK
    ckv = _dot(h, w_ref[:, o:o + D_KV_RANK]); o += D_KV_RANK
    kr = _dot(h, w_ref[:, o:o + PAD_DK])
    q = _dot(_rms(cq, gq_ref[...]).astype(BF16), wuq_ref[...])
    ckvn = _rms(ckv, gkv_ref[...]).astype(BF16)
    kn = _dot(ckvn, wk_ref[...])
    kpe = _rope_lanes(kr, cos, sin, D_ROPE // 2, lo)
    for j in range(D_HEADS):
        sl = slice(j * PAD_DK, (j + 1) * PAD_DK)
        q_ref[:, sl] = (_rope_lanes(q[:, sl], cos, sin, D_ROPE // 2, lo) * scale).astype(BF16)
        k_ref[:, sl] = (kn[:, sl] + kpe).astype(BF16)
    v_ref[...] = _dot(ckvn, wv_ref[...]).astype(BF16)


def _odd_in(x, g, w_in, gq, gkv, wuq, wk, wv, cos, sin):
    tm = TOKEN_TILE
    per_seq = SEQ // tm
    tok = lambda w: pl.BlockSpec((tm, w), lambda i: (i, 0))
    pos = pl.BlockSpec((tm, LANES), lambda i: (i % per_seq, 0))
    sds = lambda w, dt: jax.ShapeDtypeStruct((TOKENS, w), dt)
    qw = D_HEADS * PAD_DK
    vw = D_HEADS * D_VDIM
    return pl.pallas_call(
        _odd_in_kernel,
        grid=(TOKENS // tm,),
        in_specs=[tok(D_MODEL)] + [_resident(a.shape) for a in (g, w_in, gq, gkv, wuq, wk, wv)] + [pos, pos],
        out_specs=[tok(C_WIDTH), tok(C_WIDTH), tok(qw), tok(qw), tok(vw)],
        out_shape=[sds(C_WIDTH, F32), sds(C_WIDTH, F32), sds(qw, BF16), sds(qw, BF16), sds(vw, BF16)],
        compiler_params=_params("parallel"),
        name="odd_in",
    )(x, g, w_in, gq, gkv, wuq, wk, wv, cos, sin)


def _rglru_kernel(xc_ref, gc_ref, cw_ref, cb_ref, wgate_ref, bgate_ref, lam_ref, o_ref,
                  a_scr, b_scr, h_scr):
    cw = cw_ref[...]
    x = xc_ref[...]
    row = lax.broadcasted_iota(jnp.int32, x.shape, 0)

    def shifted(d):
        rolled = pltpu.roll(x, (-d) % SEQ, 0)
        valid = (row + d >= 0) & (row + d < SEQ)
        return jnp.where(valid, rolled, 0.0)

    lp = (CONV_W - 1) // 2
    u = cb_ref[...] + sum((x if j == lp else shifted(j - lp)) * cw[j:j + 1] for j in range(CONV_W))
    gates = _dot(u.astype(BF16), wgate_ref[0]) + bgate_ref[0]
    lam = lam_ref[0]
    for d in range(2):
        o = 2 * d * LANES
        r_t = jax.nn.sigmoid(gates[:, o:o + LANES])
        i_t = jax.nn.sigmoid(gates[:, o + LANES:o + 2 * LANES])
        log_a = (-RG_C * r_t) * jax.nn.softplus(-lam[:, d * LANES:(d + 1) * LANES])
        a = jnp.exp(log_a)
        mult = jnp.sqrt(-jnp.tanh(log_a) * (a * a + 1.0))
        mult = jnp.where(row == (SEQ - 1 if d else 0), 1.0, mult)
        a_scr[d] = a
        b_scr[d] = mult * i_t * u

    n_tiles = SEQ // SUBLANES
    srow = lax.broadcasted_iota(jnp.int32, (SUBLANES, LANES), 0)

    def tile_scan(a, b, h_prev, reverse):
        for d in (1, 2, 4):
            if reverse:
                keep = srow < SUBLANES - d
                shift = SUBLANES - d
            else:
                keep = srow >= d
                shift = d
            a_s = jnp.where(keep, pltpu.roll(a, shift, 0), 1.0)
            b_s = jnp.where(keep, pltpu.roll(b, shift, 0), 0.0)
            b = a * b_s + b
            a = a * a_s
        return a * h_prev + b

    def step(i, carry):
        hf, hr = carry
        tf = pl.multiple_of(i * SUBLANES, SUBLANES)
        h = tile_scan(a_scr[0, pl.ds(tf, SUBLANES), :], b_scr[0, pl.ds(tf, SUBLANES), :], hf, False)
        h_scr[0, pl.ds(tf, SUBLANES), :] = h
        hf = jnp.broadcast_to(h[SUBLANES - 1:SUBLANES, :], h.shape)
        tr = pl.multiple_of((n_tiles - 1 - i) * SUBLANES, SUBLANES)
        h = tile_scan(a_scr[1, pl.ds(tr, SUBLANES), :], b_scr[1, pl.ds(tr, SUBLANES), :], hr, True)
        h_scr[1, pl.ds(tr, SUBLANES), :] = h
        hr = jnp.broadcast_to(h[0:1, :], h.shape)
        return hf, hr

    z = jnp.zeros((SUBLANES, LANES), F32)
    lax.fori_loop(0, n_tiles, step, (z, z))
    o_ref[...] = (jax.nn.gelu(gc_ref[...]) * (h_scr[0] + h_scr[1])).astype(BF16)


def _rglru(xc, gc, conv_w, conv_b, wgate, bgate, lam):
    ng = C_WIDTH // LANES
    seq = pl.BlockSpec((SEQ, LANES), lambda b, g: (b, g))
    grp = lambda a: pl.BlockSpec((1,) + a.shape[1:], lambda b, g: (g, 0, 0))
    return pl.pallas_call(
        _rglru_kernel,
        grid=(BATCH, ng),
        in_specs=[seq, seq,
                  pl.BlockSpec((CONV_W, LANES), lambda b, g: (0, g)),
                  pl.BlockSpec((1, LANES), lambda b, g: (0, g)),
                  grp(wgate), grp(bgate), grp(lam)],
        out_specs=seq,
        out_shape=jax.ShapeDtypeStruct((TOKENS, C_WIDTH), BF16),
        scratch_shapes=[pltpu.VMEM((2, SEQ, LANES), F32)] * 3,
        compiler_params=_params("parallel", "parallel"),
        name="rglru",
    )(xc, gc, conv_w, conv_b, wgate, bgate, lam)


def _rglru_gate_params(rg_wa, rg_ba, rg_wx, rg_bx, rg_lam):
    ng = C_WIDTH // LANES
    per = LANES // C_BW

    def dense(w):
        w = w.reshape(ng, per, C_BW, C_BW)
        eye = jnp.eye(per, dtype=w.dtype)
        return jnp.einsum('gpcd,pq->gpcqd', w, eye).reshape(ng, LANES, LANES)

    wgate = jnp.concatenate([dense(rg_wa[0]), dense(rg_wx[0]), dense(rg_wa[1]), dense(rg_wx[1])], axis=-1)
    grp = lambda v: v.reshape(ng, 1, LANES)
    bgate = jnp.concatenate([grp(rg_ba[0]), grp(rg_bx[0]), grp(rg_ba[1]), grp(rg_bx[1])], axis=-1)
    lam = jnp.concatenate([grp(rg_lam[0]), grp(rg_lam[1])], axis=-1)
    return wgate.astype(BF16), bgate.astype(F32), lam.astype(F32)


def _mla_attn_kernel(q_ref, k_ref, v_ref, o_ref):
    v = v_ref[...]
    lane = lax.broadcasted_iota(jnp.int32, (Q_TILE, 2 * D_VDIM), 1)
    outs = []
    for j in range(2):
        sl = slice(j * PAD_DK, (j + 1) * PAD_DK)
        s = lax.dot_general(q_ref[:, sl], k_ref[:, sl], _NT, preferred_element_type=F32)
        e = jnp.exp(s - jnp.max(s, axis=-1, keepdims=True))
        inv = 1.0 / jnp.sum(e, axis=-1, keepdims=True)
        outs.append(_dot(e.astype(BF16), v) * inv)
    o_ref[...] = jnp.where(lane < D_VDIM, outs[0], outs[1]).astype(BF16)


def _mla_attn(q, k, v):
    nq = SEQ // Q_TILE
    qspec = pl.BlockSpec((Q_TILE, 2 * PAD_DK), lambda b, p, i: (b * nq + i, p))
    kspec = pl.BlockSpec((SEQ, 2 * PAD_DK), lambda b, p, i: (b, p))
    vspec = pl.BlockSpec((SEQ, 2 * D_VDIM), lambda b, p, i: (b, p))
    ospec = pl.BlockSpec((Q_TILE, 2 * D_VDIM), lambda b, p, i: (b * nq + i, p))
    return pl.pallas_call(
        _mla_attn_kernel,
        grid=(BATCH, D_HEADS // 2, nq),
        in_specs=[qspec, kspec, vspec],
        out_specs=ospec,
        out_shape=jax.ShapeDtypeStruct((TOKENS, D_HEADS * D_VDIM), BF16),
        compiler_params=_params("parallel", "parallel", "arbitrary"),
        name="mla_attn",
    )(q, k, v)


def _rope_angles(dim):
    inv = 1.0 / (ROPE_THETA ** (jnp.arange(0, dim, 2, dtype=F32) / dim))
    ang = jnp.arange(SEQ, dtype=F32)[:, None] * inv[None, :]
    return jnp.cos(ang), jnp.sin(ang)


def _even_rope_tables():
    cos, sin = _rope_angles(A_DH)
    reps = LANES // A_DH
    return (jnp.tile(jnp.concatenate([cos, cos], -1), (1, reps)),
            jnp.tile(jnp.concatenate([-sin, sin], -1), (1, reps)))


def _odd_rope_tables():
    cos, sin = _rope_angles(D_ROPE)
    ones = jnp.ones((SEQ, D_NOPE), F32)
    zn = jnp.zeros((SEQ, D_NOPE), F32)
    zp = jnp.zeros((SEQ, PAD_DK - D_NOPE - D_ROPE), F32)
    return (jnp.concatenate([ones, cos, cos, zp], -1),
            jnp.concatenate([zn, -sin, sin, zp], -1))


def _odd_weights(w_in, wuq, wukv):
    base = 2 * C_WIDTH + D_Q_RANK + D_KV_RANK
    zl = jnp.zeros((D_MODEL, D_NOPE), w_in.dtype)
    zr = jnp.zeros((D_MODEL, PAD_DK - D_NOPE - D_ROPE), w_in.dtype)
    w_in_p = jnp.concatenate([w_in[:, :base], zl, w_in[:, base:], zr], axis=-1)
    dqk = D_NOPE + D_ROPE
    wuq_p = jnp.pad(wuq.reshape(D_Q_RANK, D_HEADS, dqk), ((0, 0), (0, 0), (0, PAD_DK - dqk)))
    wukv_h = wukv.reshape(D_KV_RANK, D_HEADS, D_NOPE + D_VDIM)
    wk_p = jnp.pad(wukv_h[:, :, :D_NOPE], ((0, 0), (0, 0), (0, PAD_DK - D_NOPE)))
    wv = wukv_h[:, :, D_NOPE:]
    return (w_in_p.astype(BF16), wuq_p.reshape(D_Q_RANK, -1).astype(BF16),
            wk_p.reshape(D_KV_RANK, -1).astype(BF16), wv.reshape(D_KV_RANK, -1).astype(BF16))


def kernel(x, ffn1_norm, ffn1_wg, ffn1_wu, ffn1_wd, mix_norm, ffn2_norm, ffn2_wg, ffn2_wu, ffn2_wd, final_norm, ev_w_in, ev_w_out, diff_lam, diff_subln, na_rpb, od_w_in, od_w_out, conv_w, conv_b, rg_wa, rg_ba, rg_wx, rg_bx, rg_lam, mla_gq, mla_gkv, mla_wuq, mla_wukv):
    row = lambda v: v.reshape(1, -1).astype(F32)
    bf = lambda w: w.astype(BF16)
    xt = x.reshape(TOKENS, D_MODEL)

    xt = _ffn(xt, row(ffn1_norm[0]), bf(ffn1_wg[0]), bf(ffn1_wu[0]), bf(ffn1_wd[0]))
    cos_a, sin_a = _even_rope_tables()
    qa, ka, va, qb, kb, vb = _even_in(xt, row(mix_norm[0]), bf(ev_w_in[0]), cos_a, sin_a)
    lam_init0 = 0.8 - 0.6 * math.exp(-0.3 * 0)
    oa = _diff_attn(qa, ka, va, diff_lam[0].astype(F32), row(diff_subln[0]), lam_init0)
    ob = _na_attn(qb, kb, vb, _na_bias_table(na_rpb[0]))
    xt = _ffn(xt, row(ffn2_norm[0]), bf(ffn2_wg[0]), bf(ffn2_wu[0]), bf(ffn2_wd[0]),
              mix=(oa, ob, bf(ev_w_out[0])))

    xt = _ffn(xt, row(ffn1_norm[1]), bf(ffn1_wg[1]), bf(ffn1_wu[1]), bf(ffn1_wd[1]))
    cos_d, sin_d = _odd_rope_tables()
    w_in_p, wuq_p, wk_p, wv_p = _odd_weights(od_w_in[0], mla_wuq[0], mla_wukv[0])
    xc, gc, q, k, v = _odd_in(xt, row(mix_norm[1]), w_in_p, row(mla_gq[0]), row(mla_gkv[0]),
                              wuq_p, wk_p, wv_p, cos_d, sin_d)
    wgate, bgate, lam = _rglru_gate_params(rg_wa[0], rg_ba[0], rg_wx[0], rg_bx[0], rg_lam[0])
    oc = _rglru(xc, gc, conv_w[0].astype(F32), row(conv_b[0]), wgate, bgate, lam)
    od = _mla_attn(q, k, v)
    xt = _ffn(xt, row(ffn2_norm[1]), bf(ffn2_wg[1]), bf(ffn2_wu[1]), bf(ffn2_wd[1]),
              mix=(oc, od, bf(od_w_out[0])), final_g=row(final_norm))
    return xt.reshape(BATCH, SEQ, D_MODEL)
```

```python
import functools
import math

import jax
import jax.numpy as jnp
import numpy as np
from jax import lax
from jax.experimental import pallas as pl
from jax.experimental.pallas import tpu as pltpu

F32 = jnp.float32
BF16 = jnp.bfloat16

D_MODEL = 1024
BATCH = 4
SEQ = 4096
DEPTH = 2
TOKENS = BATCH * SEQ
RMS_EPS = 1e-6
ROPE_THETA = 10000.0
GRID_W = 64
GRID_ROWS = SEQ // GRID_W
D_FF = 2816

A_HEADS = 4
A_DH = 64
A_QK = A_HEADS * 2 * A_DH
A_V = A_HEADS * 2 * A_DH
B_HEADS = 8
B_DH = 64
B_W = B_HEADS * B_DH
NA_ROWS = 8
NA_COLS = 16
C_WIDTH = 512
C_BLOCKS = 8
C_BW = C_WIDTH // C_BLOCKS
CONV_W = 4
RG_C = 8.0
D_HEADS = 8
D_NOPE = 64
D_ROPE = 32
D_VDIM = 64
D_Q_RANK = 256
D_KV_RANK = 128
EVEN_IN = 2 * A_QK + A_V + 3 * B_W
ODD_IN = 2 * C_WIDTH + D_Q_RANK + D_KV_RANK + D_ROPE

LANES = 128
SUBLANES = 8
VMEM_LIMIT_BYTES = 56 * 1024 * 1024

TOKEN_TILE = 512
FF_CHUNK = 256
Q_TILE = 256
PAD_DK = 128
SCAN_UNROLL = 8

_NT = (((1,), (1,)), ((), ()))

LOG2E = math.log2(math.e)


def _params(*sem):
    return pltpu.CompilerParams(dimension_semantics=sem, vmem_limit_bytes=VMEM_LIMIT_BYTES)


def _resident(shape):
    nd = len(shape)
    return pl.BlockSpec(shape, lambda *_: (0,) * nd, pipeline_mode=pl.Buffered(1))


def _rms(x, g):
    return x * lax.rsqrt(jnp.mean(x * x, axis=-1, keepdims=True) + RMS_EPS) * g


def _dot(a, b):
    return jnp.dot(a, b, preferred_element_type=F32)


def _sigmoid(x):
    return 0.5 * (jnp.tanh(0.5 * x) + 1.0)


def _rope_lanes(x, cos, sin_signed, half, lo_mask):
    n = x.shape[-1]
    partner = jnp.where(lo_mask, pltpu.roll(x, n - half, 1), pltpu.roll(x, half, 1))
    return x * cos + partner * sin_signed


def _ffn_kernel(*refs, has_mix, has_final):
    it = iter(refs)
    x_ref = next(it)
    if has_mix:
        o1_ref, o2_ref, wo_ref = next(it), next(it), next(it)
    g_ref, wg_ref, wu_ref, wd_ref = next(it), next(it), next(it), next(it)
    gf_ref = next(it) if has_final else None
    y_ref = next(it)
    acc_ref = next(it)

    x = x_ref[...]
    if has_mix:
        half = wo_ref.shape[0] // 2
        x = x + _dot(o1_ref[...], wo_ref[:half, :]) + _dot(o2_ref[...], wo_ref[half:, :])
    n = _rms(x, g_ref[...]).astype(BF16)
    for c in range(D_FF // FF_CHUNK):
        sl = slice(c * FF_CHUNK, (c + 1) * FF_CHUNK)
        gate = _dot(n, wg_ref[:, sl])
        up = _dot(n, wu_ref[:, sl])
        h = (gate * jax.nn.sigmoid(gate) * up).astype(BF16)
        d = _dot(h, wd_ref[sl, :])
        if c == 0:
            acc_ref[...] = d
        else:
            acc_ref[...] += d
    y = x + 0.5 * acc_ref[...]
    if has_final:
        y = _rms(y, gf_ref[...])
    y_ref[...] = y


def _ffn(x, g, wg, wu, wd, mix=None, final_g=None):
    tm = TOKEN_TILE
    tok = lambda w: pl.BlockSpec((tm, w), lambda i: (i, 0))
    args, specs = [x], [tok(D_MODEL)]
    if mix is not None:
        o1, o2, wo = mix
        args += [o1, o2, wo]
        specs += [tok(o1.shape[1]), tok(o2.shape[1]), _resident(wo.shape)]
    args += [g, wg, wu, wd]
    specs += [_resident(g.shape), _resident(wg.shape), _resident(wu.shape), _resident(wd.shape)]
    if final_g is not None:
        args.append(final_g)
        specs.append(_resident(final_g.shape))
    return pl.pallas_call(
        functools.partial(_ffn_kernel, has_mix=mix is not None, has_final=final_g is not None),
        grid=(TOKENS // tm,),
        in_specs=specs,
        out_specs=tok(D_MODEL),
        out_shape=jax.ShapeDtypeStruct((TOKENS, D_MODEL), F32),
        scratch_shapes=[pltpu.VMEM((tm, D_MODEL), F32)],
        compiler_params=_params("parallel"),
        name="ffn_mix" if mix is not None else "ffn",
    )(*args)


def _even_in_kernel(x_ref, g_ref, w_ref, cos_ref, sin_ref,
                    qa_ref, ka_ref, va_ref, qb_ref, kb_ref, vb_ref):
    h = _rms(x_ref[...], g_ref[...]).astype(BF16)
    cos, sin = cos_ref[...], sin_ref[...]
    lane = lax.broadcasted_iota(jnp.int32, cos.shape, 1)
    lo = (lane % A_DH) < (A_DH // 2)
    scale = A_DH ** -0.5 * LOG2E

    def proj(i):
        return _dot(h, w_ref[:, i * A_QK:(i + 1) * A_QK])

    qa, ka = proj(0), proj(1)
    for j in range(A_QK // LANES):
        sl = slice(j * LANES, (j + 1) * LANES)
        qa_ref[:, sl] = (_rope_lanes(qa[:, sl], cos, sin, A_DH // 2, lo) * scale).astype(BF16)
        ka_ref[:, sl] = _rope_lanes(ka[:, sl], cos, sin, A_DH // 2, lo).astype(BF16)
    va_ref[...] = proj(2).astype(BF16)
    qb_ref[...] = (proj(3) * (B_DH ** -0.5 * LOG2E)).astype(BF16)
    kb_ref[...] = proj(4).astype(BF16)
    vb_ref[...] = proj(5).astype(BF16)


def _even_in(x, g, w_in, cos, sin):
    tm = TOKEN_TILE
    per_seq = SEQ // tm
    tok = lambda w: pl.BlockSpec((tm, w), lambda i: (i, 0))
    pos = pl.BlockSpec((tm, LANES), lambda i: (i % per_seq, 0))
    out = jax.ShapeDtypeStruct((TOKENS, A_QK), BF16)
    return pl.pallas_call(
        _even_in_kernel,
        grid=(TOKENS // tm,),
        in_specs=[tok(D_MODEL), _resident(g.shape), _resident(w_in.shape), pos, pos],
        out_specs=[tok(A_QK)] * 6,
        out_shape=[out] * 6,
        compiler_params=_params("parallel"),
        name="even_in",
    )(x, g, w_in, cos, sin)


def _diff_attn_kernel(q_ref, k_ref, v_ref, lam_ref, g_ref, o_ref, *, lam_init):
    q = q_ref[...]
    k = k_ref[...]
    lane = lax.broadcasted_iota(jnp.int32, q.shape, 1)
    zero = jnp.zeros_like(q)
    lf = lam_ref[...]
    lam = (jnp.exp(jnp.sum(lf[0:1] * lf[1:2], axis=-1, keepdims=True))
           - jnp.exp(jnp.sum(lf[2:3] * lf[3:4], axis=-1, keepdims=True)) + lam_init)

    def softmax_parts(qm):
        s = lax.dot_general(qm, k, _NT, preferred_element_type=F32)
        e = jnp.exp2(s - jnp.max(s, axis=-1, keepdims=True))
        return e, jnp.sum(e, axis=-1, keepdims=True)

    e1, l1 = softmax_parts(jnp.where(lane < A_DH, q, zero))
    e2, l2 = softmax_parts(jnp.where(lane >= A_DH, q, zero))
    w = e1 - e2 * (lam * l1 / l2)
    o = _dot(w.astype(BF16), v_ref[...]) / l1
    o = _rms(o, g_ref[...]) * (1.0 - lam_init)
    o_ref[...] = o.astype(BF16)


def _diff_attn(qa, ka, va, lam_vec, subln_g, lam_init):
    nq = SEQ // Q_TILE
    hw = 2 * A_DH
    qspec = pl.BlockSpec((Q_TILE, hw), lambda b, h, i: (b * nq + i, h))
    kspec = pl.BlockSpec((SEQ, hw), lambda b, h, i: (b, h))
    return pl.pallas_call(
        functools.partial(_diff_attn_kernel, lam_init=lam_init),
        grid=(BATCH, A_HEADS, nq),
        in_specs=[qspec, kspec, kspec, _resident(lam_vec.shape), _resident(subln_g.shape)],
        out_specs=qspec,
        out_shape=jax.ShapeDtypeStruct((TOKENS, A_V), BF16),
        compiler_params=_params("parallel", "parallel", "arbitrary"),
        name="diff_attn",
    )(qa, ka, va, lam_vec, subln_g)


NA_QROWS = 4
NA_KROWS = NA_QROWS + NA_ROWS
NA_BLOCKS = GRID_ROWS // NA_QROWS
NA_PLACEMENTS = 3


def _na_window_start(g):
    lo, hi = 0, GRID_ROWS - NA_KROWS
    s = g * NA_QROWS - NA_ROWS // 2
    if isinstance(g, int):
        return min(max(s, lo), hi)
    return jnp.clip(s, lo, hi)


def _na_attn_kernel(q_ref, k_ref, v_ref, bias_ref, o_ref):
    nq = NA_QROWS * GRID_W
    nk = NA_KROWS * GRID_W
    lane = lax.broadcasted_iota(jnp.int32, (nq, 2 * B_DH), 1)
    first = lane < B_DH

    def block(g, carry):
        var = jnp.where(g == 0, 0, jnp.where(g == NA_BLOCKS - 1, 2, 1))
        qstart = pl.multiple_of(g * nq, nq)
        kstart = pl.multiple_of(_na_window_start(g) * GRID_W, GRID_W)
        q = q_ref[pl.ds(qstart, nq), :]
        k = k_ref[pl.ds(kstart, nk), :]
        v = v_ref[pl.ds(kstart, nk), :]
        zero = jnp.zeros_like(q)
        outs = []
        for j in range(2):
            qj = jnp.where(first if j == 0 else jnp.logical_not(first), q, zero)
            s = lax.dot_general(qj, k, _NT, preferred_element_type=F32) + bias_ref[j, var]
            e = jnp.exp2(s - jnp.max(s, axis=-1, keepdims=True))
            inv = 1.0 / jnp.sum(e, axis=-1, keepdims=True)
            outs.append(_dot(e.astype(BF16), v) * inv)
        o_ref[pl.ds(qstart, nq), :] = jnp.where(first, outs[0], outs[1]).astype(BF16)
        return carry

    lax.fori_loop(0, NA_BLOCKS, block, 0)


def _na_attn(qb, kb, vb, bias):
    hw = 2 * B_DH
    spec = pl.BlockSpec((SEQ, hw), lambda b, p: (b, p))
    bspec = pl.BlockSpec((2,) + bias.shape[1:], lambda b, p: (p, 0, 0, 0))
    return pl.pallas_call(
        _na_attn_kernel,
        grid=(BATCH, B_HEADS // 2),
        in_specs=[spec, spec, spec, bspec],
        out_specs=spec,
        out_shape=jax.ShapeDtypeStruct((TOKENS, B_W), BF16),
        compiler_params=_params("parallel", "parallel"),
        name="na_attn",
    )(qb, kb, vb, bias)


def _na_bias_table(rpb):
    c = np.arange(GRID_W)
    c0 = np.clip(c - NA_COLS // 2, 0, GRID_W - NA_COLS)
    col_in = (c[None, :] >= c0[:, None]) & (c[None, :] < c0[:, None] + NA_COLS)
    dc = np.clip(c[None, :] - c[:, None], -(NA_COLS - 1), NA_COLS - 1) + NA_COLS - 1
    rpb = rpb.astype(F32)
    toe = jnp.zeros(rpb.shape[:2] + (GRID_W, GRID_W), F32)
    for d in range(2 * NA_COLS - 1):
        toe = jnp.where(dc == d, rpb[:, :, d, None, None], toe)
    toe = jnp.where(col_in, toe, -jnp.inf)
    masked = jnp.full((B_HEADS, GRID_W, GRID_W), -jnp.inf, F32)
    tiles = []
    for g in (0, 1, NA_BLOCKS - 1):
        ws = _na_window_start(g)
        for ql in range(NA_QROWS):
            qr = g * NA_QROWS + ql
            r0 = min(max(qr - NA_ROWS // 2, 0), GRID_ROWS - NA_ROWS)
            for kl in range(NA_KROWS):
                kr = ws + kl
                tiles.append(toe[:, kr - qr + NA_ROWS - 1] if r0 <= kr < r0 + NA_ROWS else masked)
    t = jnp.stack(tiles).reshape(NA_PLACEMENTS, NA_QROWS, NA_KROWS, B_HEADS, GRID_W, GRID_W)
    return t.transpose(3, 0, 1, 4, 2, 5).reshape(
        B_HEADS, NA_PLACEMENTS, NA_QROWS * GRID_W, NA_KROWS * GRID_W)


def _odd_in_kernel(x_ref, g_ref, w_ref, gq_ref, gkv_ref, wuq_ref, wk_ref, wv_ref,
                   cos_ref, sin_ref, xc_ref, gc_ref, q_ref, k_ref, v_ref):
    h = _rms(x_ref[...], g_ref[...]).astype(BF16)
    cos, sin = cos_ref[...], sin_ref[...]
    lane = lax.broadcasted_iota(jnp.int32, cos.shape, 1)
    lo = lane < D_NOPE + D_ROPE // 2
    scale = (D_NOPE + D_ROPE) ** -0.5 * LOG2E
    o = 0
    xc_ref[...] = _dot(h, w_ref[:, o:o + C_WIDTH]); o += C_WIDTH
    gc_ref[...] = _dot(h, w_ref[:, o:o + C_WIDTH]); o += C_WIDTH
    cq = _dot(h, w_ref[:, o:o + D_Q_RANK]); o += D_Q_RANK
    ckv = _dot(h, w_ref[:, o:o + D_KV_RANK]); o += D_KV_RANK
    kr = _dot(h, w_ref[:, o:o + PAD_DK])
    q = _dot(_rms(cq, gq_ref[...]).astype(BF16), wuq_ref[...])
    ckvn = _rms(ckv, gkv_ref[...]).astype(BF16)
    kn = _dot(ckvn, wk_ref[...])
    kpe = _rope_lanes(kr, cos, sin, D_ROPE // 2, lo)
    for j in range(D_HEADS):
        sl = slice(j * PAD_DK, (j + 1) * PAD_DK)
        q_ref[:, sl] = (_rope_lanes(q[:, sl], cos, sin, D_ROPE // 2, lo) * scale).astype(BF16)
        k_ref[:, sl] = (kn[:, sl] + kpe).astype(BF16)
    v_ref[...] = _dot(ckvn, wv_ref[...]).astype(BF16)


def _odd_in(x, g, w_in, gq, gkv, wuq, wk, wv, cos, sin):
    tm = TOKEN_TILE
    per_seq = SEQ // tm
    tok = lambda w: pl.BlockSpec((tm, w), lambda i: (i, 0))
    pos = pl.BlockSpec((tm, LANES), lambda i: (i % per_seq, 0))
    sds = lambda w, dt: jax.ShapeDtypeStruct((TOKENS, w), dt)
    qw = D_HEADS * PAD_DK
    vw = D_HEADS * D_VDIM
    return pl.pallas_call(
        _odd_in_kernel,
        grid=(TOKENS // tm,),
        in_specs=[tok(D_MODEL)] + [_resident(a.shape) for a in (g, w_in, gq, gkv, wuq, wk, wv)] + [pos, pos],
        out_specs=[tok(C_WIDTH), tok(C_WIDTH), tok(qw), tok(qw), tok(vw)],
        out_shape=[sds(C_WIDTH, F32), sds(C_WIDTH, F32), sds(qw, BF16), sds(qw, BF16), sds(vw, BF16)],
        compiler_params=_params("parallel"),
        name="odd_in",
    )(x, g, w_in, gq, gkv, wuq, wk, wv, cos, sin)


def _rglru_kernel(xc_ref, gc_ref, cw_ref, cb_ref, wgate_ref, bgate_ref, lam_ref, o_ref,
                  a_scr, b_scr, h_scr):
    cw = cw_ref[...]
    x = xc_ref[...]
    row = lax.broadcasted_iota(jnp.int32, x.shape, 0)

    def shifted(d):
        rolled = pltpu.roll(x, (-d) % SEQ, 0)
        valid = (row + d >= 0) & (row + d < SEQ)
        return jnp.where(valid, rolled, 0.0)

    lp = (CONV_W - 1) // 2
    u = cb_ref[...] + sum((x if j == lp else shifted(j - lp)) * cw[j:j + 1] for j in range(CONV_W))
    gates = _dot(u.astype(BF16), wgate_ref[0]) + bgate_ref[0]
    lam = lam_ref[0]
    for d in range(2):
        o = 2 * d * LANES
        r_t = _sigmoid(gates[:, o:o + LANES])
        i_t = _sigmoid(gates[:, o + LANES:o + 2 * LANES])
        log_a = (-RG_C * r_t) * jax.nn.softplus(-lam[:, d * LANES:(d + 1) * LANES])
        a = jnp.exp(log_a)
        y = -jnp.tanh(log_a) * (a * a + 1.0)
        mult = jnp.where(y > 0.0, y * lax.rsqrt(y), 0.0)
        mult = jnp.where(row == (SEQ - 1 if d else 0), 1.0, mult)
        a_scr[d] = a
        b_scr[d] = mult * i_t * u

    n_tiles = SEQ // SUBLANES
    srow = lax.broadcasted_iota(jnp.int32, (SUBLANES, LANES), 0)

    def tile_scan(a, b, h_prev, reverse):
        for d in (1, 2, 4):
            if reverse:
                keep = srow < SUBLANES - d
                shift = SUBLANES - d
            else:
                keep = srow >= d
                shift = d
            a_s = jnp.where(keep, pltpu.roll(a, shift, 0), 1.0)
            b_s = jnp.where(keep, pltpu.roll(b, shift, 0), 0.0)
            b = a * b_s + b
            a = a * a_s
        return a * h_prev + b

    def step(i, carry):
        hf, hr = carry
        tf = pl.multiple_of(i * SUBLANES, SUBLANES)
        h = tile_scan(a_scr[0, pl.ds(tf, SUBLANES), :], b_scr[0, pl.ds(tf, SUBLANES), :], hf, False)
        h_scr[0, pl.ds(tf, SUBLANES), :] = h
        hf = jnp.broadcast_to(h[SUBLANES - 1:SUBLANES, :], h.shape)
        tr = pl.multiple_of((n_tiles - 1 - i) * SUBLANES, SUBLANES)
        h = tile_scan(a_scr[1, pl.ds(tr, SUBLANES), :], b_scr[1, pl.ds(tr, SUBLANES), :], hr, True)
        h_scr[1, pl.ds(tr, SUBLANES), :] = h
        hr = jnp.broadcast_to(h[0:1, :], h.shape)
        return hf, hr

    z = jnp.zeros((SUBLANES, LANES), F32)
    lax.fori_loop(0, n_tiles, step, (z, z), unroll=SCAN_UNROLL)
    o_ref[...] = (jax.nn.gelu(gc_ref[...]) * (h_scr[0] + h_scr[1])).astype(BF16)


def _rglru(xc, gc, conv_w, conv_b, wgate, bgate, lam):
    ng = C_WIDTH // LANES
    seq = pl.BlockSpec((SEQ, LANES), lambda b, g: (b, g))
    grp = lambda a: pl.BlockSpec((1,) + a.shape[1:], lambda b, g: (g, 0, 0))
    return pl.pallas_call(
        _rglru_kernel,
        grid=(BATCH, ng),
        in_specs=[seq, seq,
                  pl.BlockSpec((CONV_W, LANES), lambda b, g: (0, g)),
                  pl.BlockSpec((1, LANES), lambda b, g: (0, g)),
                  grp(wgate), grp(bgate), grp(lam)],
        out_specs=seq,
        out_shape=jax.ShapeDtypeStruct((TOKENS, C_WIDTH), BF16),
        scratch_shapes=[pltpu.VMEM((2, SEQ, LANES), F32)] * 3,
        compiler_params=_params("parallel", "parallel"),
        name="rglru",
    )(xc, gc, conv_w, conv_b, wgate, bgate, lam)


def _rglru_gate_params(rg_wa, rg_ba, rg_wx, rg_bx, rg_lam):
    ng = C_WIDTH // LANES
    per = LANES // C_BW

    def dense(w):
        w = w.reshape(ng, per, C_BW, C_BW)
        eye = jnp.eye(per, dtype=w.dtype)
        return jnp.einsum('gpcd,pq->gpcqd', w, eye).reshape(ng, LANES, LANES)

    wgate = jnp.concatenate([dense(rg_wa[0]), dense(rg_wx[0]), dense(rg_wa[1]), dense(rg_wx[1])], axis=-1)
    grp = lambda v: v.reshape(ng, 1, LANES)
    bgate = jnp.concatenate([grp(rg_ba[0]), grp(rg_bx[0]), grp(rg_ba[1]), grp(rg_bx[1])], axis=-1)
    lam = jnp.concatenate([grp(rg_lam[0]), grp(rg_lam[1])], axis=-1)
    return wgate.astype(BF16), bgate.astype(F32), lam.astype(F32)


def _mla_attn_kernel(q_ref, k_ref, v_ref, o_ref):
    v = v_ref[...]
    lane = lax.broadcasted_iota(jnp.int32, (Q_TILE, 2 * D_VDIM), 1)
    outs = []
    for j in range(2):
        sl = slice(j * PAD_DK, (j + 1) * PAD_DK)
        s = lax.dot_general(q_ref[:, sl], k_ref[:, sl], _NT, preferred_element_type=F32)
        e = jnp.exp2(s - jnp.max(s, axis=-1, keepdims=True))
        inv = 1.0 / jnp.sum(e, axis=-1, keepdims=True)
        outs.append(_dot(e.astype(BF16), v) * inv)
    o_ref[...] = jnp.where(lane < D_VDIM, outs[0], outs[1]).astype(BF16)


def _mla_attn(q, k, v):
    nq = SEQ // Q_TILE
    qspec = pl.BlockSpec((Q_TILE, 2 * PAD_DK), lambda b, p, i: (b * nq + i, p))
    kspec = pl.BlockSpec((SEQ, 2 * PAD_DK), lambda b, p, i: (b, p))
    vspec = pl.BlockSpec((SEQ, 2 * D_VDIM), lambda b, p, i: (b, p))
    ospec = pl.BlockSpec((Q_TILE, 2 * D_VDIM), lambda b, p, i: (b * nq + i, p))
    return pl.pallas_call(
        _mla_attn_kernel,
        grid=(BATCH, D_HEADS // 2, nq),
        in_specs=[qspec, kspec, vspec],
        out_specs=ospec,
        out_shape=jax.ShapeDtypeStruct((TOKENS, D_HEADS * D_VDIM), BF16),
        compiler_params=_params("parallel", "parallel", "arbitrary"),
        name="mla_attn",
    )(q, k, v)


def _rope_angles(dim):
    inv = 1.0 / (ROPE_THETA ** (jnp.arange(0, dim, 2, dtype=F32) / dim))
    ang = jnp.arange(SEQ, dtype=F32)[:, None] * inv[None, :]
    return jnp.cos(ang), jnp.sin(ang)


def _even_rope_tables():
    cos, sin = _rope_angles(A_DH)
    reps = LANES // A_DH
    return (jnp.tile(jnp.concatenate([cos, cos], -1), (1, reps)),
            jnp.tile(jnp.concatenate([-sin, sin], -1), (1, reps)))


def _odd_rope_tables():
    cos, sin = _rope_angles(D_ROPE)
    ones = jnp.ones((SEQ, D_NOPE), F32)
    zn = jnp.zeros((SEQ, D_NOPE), F32)
    zp = jnp.zeros((SEQ, PAD_DK - D_NOPE - D_ROPE), F32)
    return (jnp.concatenate([ones, cos, cos, zp], -1),
            jnp.concatenate([zn, -sin, sin, zp], -1))


def _odd_weights(w_in, wuq, wukv):
    base = 2 * C_WIDTH + D_Q_RANK + D_KV_RANK
    zl = jnp.zeros((D_MODEL, D_NOPE), w_in.dtype)
    zr = jnp.zeros((D_MODEL, PAD_DK - D_NOPE - D_ROPE), w_in.dtype)
    w_in_p = jnp.concatenate([w_in[:, :base], zl, w_in[:, base:], zr], axis=-1)
    dqk = D_NOPE + D_ROPE
    wuq_p = jnp.pad(wuq.reshape(D_Q_RANK, D_HEADS, dqk), ((0, 0), (0, 0), (0, PAD_DK - dqk)))
    wukv_h = wukv.reshape(D_KV_RANK, D_HEADS, D_NOPE + D_VDIM)
    wk_p = jnp.pad(wukv_h[:, :, :D_NOPE], ((0, 0), (0, 0), (0, PAD_DK - D_NOPE)))
    wv = wukv_h[:, :, D_NOPE:]
    return (w_in_p.astype(BF16), wuq_p.reshape(D_Q_RANK, -1).astype(BF16),
            wk_p.reshape(D_KV_RANK, -1).astype(BF16), wv.reshape(D_KV_RANK, -1).astype(BF16))


def kernel(x, ffn1_norm, ffn1_wg, ffn1_wu, ffn1_wd, mix_norm, ffn2_norm, ffn2_wg, ffn2_wu, ffn2_wd, final_norm, ev_w_in, ev_w_out, diff_lam, diff_subln, na_rpb, od_w_in, od_w_out, conv_w, conv_b, rg_wa, rg_ba, rg_wx, rg_bx, rg_lam, mla_gq, mla_gkv, mla_wuq, mla_wukv):
    row = lambda v: v.reshape(1, -1).astype(F32)
    bf = lambda w: w.astype(BF16)
    xt = x.reshape(TOKENS, D_MODEL)

    xt = _ffn(xt, row(ffn1_norm[0]), bf(ffn1_wg[0]), bf(ffn1_wu[0]), bf(ffn1_wd[0]))
    cos_a, sin_a = _even_rope_tables()
    qa, ka, va, qb, kb, vb = _even_in(xt, row(mix_norm[0]), bf(ev_w_in[0]), cos_a, sin_a)
    lam_init0 = 0.8 - 0.6 * math.exp(-0.3 * 0)
    oa = _diff_attn(qa, ka, va, diff_lam[0].astype(F32), row(diff_subln[0]), lam_init0)
    ob = _na_attn(qb, kb, vb, _na_bias_table(na_rpb[0]) * LOG2E)
    xt = _ffn(xt, row(ffn2_norm[0]), bf(ffn2_wg[0]), bf(ffn2_wu[0]), bf(ffn2_wd[0]),
              mix=(oa, ob, bf(ev_w_out[0])))

    xt = _ffn(xt, row(ffn1_norm[1]), bf(ffn1_wg[1]), bf(ffn1_wu[1]), bf(ffn1_wd[1]))
    cos_d, sin_d = _odd_rope_tables()
    w_in_p, wuq_p, wk_p, wv_p = _odd_weights(od_w_in[0], mla_wuq[0], mla_wukv[0])
    xc, gc, q, k, v = _odd_in(xt, row(mix_norm[1]), w_in_p, row(mla_gq[0]), row(mla_gkv[0]),
                              wuq_p, wk_p, wv_p, cos_d, sin_d)
    wgate, bgate, lam = _rglru_gate_params(rg_wa[0], rg_ba[0], rg_wx[0], rg_bx[0], rg_lam[0])
    oc = _rglru(xc, gc, conv_w[0].astype(F32), row(conv_b[0]), wgate, bgate, lam)
    od = _mla_attn(q, k, v)
    xt = _ffn(xt, row(ffn2_norm[1]), bf(ffn2_wg[1]), bf(ffn2_wu[1]), bf(ffn2_wd[1]),
              mix=(oc, od, bf(od_w_out[0])), final_g=row(final_norm))
    return xt.reshape(BATCH, SEQ, D_MODEL)
```

```python
import functools
import itertools
import math

import jax
import jax.numpy as jnp
import numpy as np
from jax import lax
from jax.experimental import pallas as pl
from jax.experimental.pallas import tpu as pltpu

F32 = jnp.float32
BF16 = jnp.bfloat16

D_MODEL = 1024
BATCH = 4
SEQ = 4096
DEPTH = 2
TOKENS = BATCH * SEQ
RMS_EPS = 1e-6
ROPE_THETA = 10000.0
GRID_W = 64
GRID_ROWS = SEQ // GRID_W
D_FF = 2816

A_HEADS = 4
A_DH = 64
A_QK = A_HEADS * 2 * A_DH
A_V = A_HEADS * 2 * A_DH
B_HEADS = 8
B_DH = 64
B_W = B_HEADS * B_DH
NA_ROWS = 8
NA_COLS = 16
C_WIDTH = 512
C_BLOCKS = 8
C_BW = C_WIDTH // C_BLOCKS
CONV_W = 4
RG_C = 8.0
D_HEADS = 8
D_NOPE = 64
D_ROPE = 32
D_VDIM = 64
D_Q_RANK = 256
D_KV_RANK = 128
EVEN_IN = 2 * A_QK + A_V + 3 * B_W
ODD_IN = 2 * C_WIDTH + D_Q_RANK + D_KV_RANK + D_ROPE

LANES = 128
SUBLANES = 8
VMEM_LIMIT_BYTES = 56 * 1024 * 1024

TOKEN_TILE = 512
FF_CHUNK = 256
Q_TILE = 256
KEY_CHUNK = 512
LOGIT_CHUNK = 2048
PAD_DK = 128
SCAN_UNROLL = 8

_NT = (((1,), (1,)), ((), ()))

LOG2E = math.log2(math.e)


def _params(*sem):
    return pltpu.CompilerParams(dimension_semantics=sem, vmem_limit_bytes=VMEM_LIMIT_BYTES)


def _resident(shape):
    nd = len(shape)
    return pl.BlockSpec(shape, lambda *_: (0,) * nd, pipeline_mode=pl.Buffered(1))


def _rms(x, g):
    return x * lax.rsqrt(jnp.mean(x * x, axis=-1, keepdims=True) + RMS_EPS) * g


def _dot(a, b):
    return jnp.dot(a, b, preferred_element_type=F32)


def _sigmoid(x):
    return 0.5 * (jnp.tanh(0.5 * x) + 1.0)


def _rope_lanes(x, cos, sin_signed, half, lo_mask):
    n = x.shape[-1]
    partner = jnp.where(lo_mask, pltpu.roll(x, n - half, 1), pltpu.roll(x, half, 1))
    return x * cos + partner * sin_signed


def _ffn_kernel(*refs, has_mix, has_final):
    it = iter(refs)
    x_ref = next(it)
    if has_mix:
        o1_ref, o2_ref, wo_ref = next(it), next(it), next(it)
    g_ref, wg_ref, wu_ref, wd_ref = next(it), next(it), next(it), next(it)
    gf_ref = next(it) if has_final else None
    y_ref = next(it)
    acc_ref = next(it)

    x = x_ref[...]
    if has_mix:
        half = wo_ref.shape[0] // 2
        x = x + _dot(o1_ref[...], wo_ref[:half, :]) + _dot(o2_ref[...], wo_ref[half:, :])
    n = _rms(x, g_ref[...]).astype(BF16)
    for c in range(D_FF // FF_CHUNK):
        sl = slice(c * FF_CHUNK, (c + 1) * FF_CHUNK)
        gate = _dot(n, wg_ref[:, sl])
        up = _dot(n, wu_ref[:, sl])
        h = (gate * jax.nn.sigmoid(gate) * up).astype(BF16)
        d = _dot(h, wd_ref[sl, :])
        if c == 0:
            acc_ref[...] = d
        else:
            acc_ref[...] += d
    y = x + 0.5 * acc_ref[...]
    if has_final:
        y = _rms(y, gf_ref[...])
    y_ref[...] = y


def _ffn(x, g, wg, wu, wd, mix=None, final_g=None):
    tm = TOKEN_TILE
    tok = lambda w: pl.BlockSpec((tm, w), lambda i: (i, 0))
    args, specs = [x], [tok(D_MODEL)]
    if mix is not None:
        o1, o2, wo = mix
        args += [o1, o2, wo]
        specs += [tok(o1.shape[1]), tok(o2.shape[1]), _resident(wo.shape)]
    args += [g, wg, wu, wd]
    specs += [_resident(g.shape), _resident(wg.shape), _resident(wu.shape), _resident(wd.shape)]
    if final_g is not None:
        args.append(final_g)
        specs.append(_resident(final_g.shape))
    return pl.pallas_call(
        functools.partial(_ffn_kernel, has_mix=mix is not None, has_final=final_g is not None),
        grid=(TOKENS // tm,),
        in_specs=specs,
        out_specs=tok(D_MODEL),
        out_shape=jax.ShapeDtypeStruct((TOKENS, D_MODEL), F32),
        scratch_shapes=[pltpu.VMEM((tm, D_MODEL), F32)],
        compiler_params=_params("parallel"),
        name="ffn_mix" if mix is not None else "ffn",
    )(*args)


def _even_in_kernel(x_ref, g_ref, w_ref, wvt_ref, cos_ref, sin_ref,
                    qa_ref, ka_ref, vat_ref, qb_ref, kb_ref, vb_ref):
    h = _rms(x_ref[...], g_ref[...]).astype(BF16)
    cos, sin = cos_ref[...], sin_ref[...]
    lane = lax.broadcasted_iota(jnp.int32, cos.shape, 1)
    lo = (lane % A_DH) < (A_DH // 2)
    scale = A_DH ** -0.5 * LOG2E

    def proj(i):
        return _dot(h, w_ref[:, i * A_QK:(i + 1) * A_QK])

    qa, ka = proj(0), proj(1)
    for j in range(A_QK // LANES):
        sl = slice(j * LANES, (j + 1) * LANES)
        qa_ref[:, sl] = (_rope_lanes(qa[:, sl], cos, sin, A_DH // 2, lo) * scale).astype(BF16)
        ka_ref[:, sl] = _rope_lanes(ka[:, sl], cos, sin, A_DH // 2, lo).astype(BF16)
    vat_ref[0] = lax.dot_general(wvt_ref[...], h, _NT, preferred_element_type=F32).astype(BF16)
    qb_ref[...] = (proj(3) * (B_DH ** -0.5 * LOG2E)).astype(BF16)
    kb_ref[...] = proj(4).astype(BF16)
    vb_ref[...] = proj(5).astype(BF16)


def _even_in(x, g, w_in, wvt, cos, sin):
    tm = TOKEN_TILE
    per_seq = SEQ // tm
    tok = lambda w: pl.BlockSpec((tm, w), lambda i: (i, 0))
    pos = pl.BlockSpec((tm, LANES), lambda i: (i % per_seq, 0))
    tr = pl.BlockSpec((1, A_V, tm), lambda i: (i // per_seq, 0, i % per_seq))
    out = jax.ShapeDtypeStruct((TOKENS, A_QK), BF16)
    out_t = jax.ShapeDtypeStruct((BATCH, A_V, SEQ), BF16)
    return pl.pallas_call(
        _even_in_kernel,
        grid=(TOKENS // tm,),
        in_specs=[tok(D_MODEL), _resident(g.shape), _resident(w_in.shape), _resident(wvt.shape), pos, pos],
        out_specs=[tok(A_QK), tok(A_QK), tr, tok(B_W), tok(B_W), tok(B_W)],
        out_shape=[out, out, out_t, out, out, out],
        compiler_params=_params("parallel"),
        name="even_in",
    )(x, g, w_in, wvt, cos, sin)


def _logit_chunks(q_ref, k_ref, s_buf, m_buf, masked):
    q = q_ref[...]
    tq, qw = q.shape
    if masked:
        lane = lax.broadcasted_iota(jnp.int32, q.shape, 1)
        zero = jnp.zeros_like(q)
        qs = [jnp.where(lane < qw // 2, q, zero), jnp.where(lane >= qw // 2, q, zero)]
        kcols = [slice(None), slice(None)]
    else:
        qs = [q[:, :qw // 2], q[:, qw // 2:]]
        kcols = [slice(0, qw // 2), slice(qw // 2, qw)]
    groups8 = LOGIT_CHUNK // SUBLANES
    mrun = [None, None]
    for c in range(SEQ // LOGIT_CHUNK):
        keys = slice(c * LOGIT_CHUNK, (c + 1) * LOGIT_CHUNK)
        for j in range(2):
            s = lax.dot_general(k_ref[keys, kcols[j]], qs[j], _NT, preferred_element_type=F32)
            s_buf[j, keys, :] = s
            part = jnp.max(s.reshape(groups8, SUBLANES, tq), axis=0)
            mrun[j] = part if mrun[j] is None else jnp.maximum(mrun[j], part)
            yield
    for j in range(2):
        m_buf[j] = jnp.broadcast_to(jnp.max(mrun[j], axis=0, keepdims=True), (SUBLANES, tq))


def _softmax_pv_chunks(s_buf, m_buf, vt_ref, result, vrows):
    tq = s_buf.shape[-1]
    groups8 = KEY_CHUNK // SUBLANES
    m = [m_buf[j][None] for j in range(2)]
    acc = [None, None]
    lrun = [None, None]
    for c in range(SEQ // KEY_CHUNK):
        keys = slice(c * KEY_CHUNK, (c + 1) * KEY_CHUNK)
        for j in range(2):
            e = jnp.exp2(s_buf[j, keys, :].reshape(groups8, SUBLANES, tq) - m[j])
            part = jnp.sum(e, axis=0)
            lrun[j] = part if lrun[j] is None else lrun[j] + part
            pv = _dot(vt_ref[0, vrows[j], keys], e.reshape(KEY_CHUNK, tq).astype(BF16))
            acc[j] = pv if acc[j] is None else acc[j] + pv
            yield
    for j in range(2):
        result.append((acc[j], jnp.sum(lrun[j], axis=0, keepdims=True)))


def _stream_attn_kernel(*refs, masked, finish):
    qf_ref, qn_ref, kn_ref, vt_ref = refs[:4]
    extra = refs[4:-5]
    o_ref, s_a, m_a, s_b, m_b = refs[-5:]
    t = pl.program_id(0)
    dv = vt_ref.shape[1]
    vrows = [slice(None)] * 2 if masked else [slice(0, dv // 2), slice(dv // 2, dv)]

    @pl.when(t == 0)
    def _():
        for _ in _logit_chunks(qf_ref, kn_ref, s_a, m_a, masked):
            pass

    def step(nxt, cur):
        result = []
        logits = _logit_chunks(qn_ref, kn_ref, nxt[0], nxt[1], masked)
        softmax = _softmax_pv_chunks(cur[0], cur[1], vt_ref, result, vrows)
        ratio = LOGIT_CHUNK // KEY_CHUNK
        for i, _ in enumerate(softmax):
            if i % ratio == 0:
                next(logits, None)
        for _ in logits:
            pass
        o_ref[...] = finish(result, *extra).astype(o_ref.dtype)

    pl.when((t & 1) == 0)(lambda: step((s_b, m_b), (s_a, m_a)))
    pl.when((t & 1) == 1)(lambda: step((s_a, m_a), (s_b, m_b)))


def _stream_attn(q, k, vt, extra, finish, *, groups, masked, name):
    tq = Q_TILE
    nq = SEQ // tq
    n_items = BATCH * groups * nq
    qw = q.shape[1] // groups

    def rows(t):
        return (t // (groups * nq)) * nq + t % nq

    def group(t):
        return (t // nq) % groups

    def batch(t):
        return t // (groups * nq)

    nxt = lambda t: jnp.minimum(t + 1, n_items - 1)
    in_specs = [
        pl.BlockSpec((tq, qw), lambda t: (0, 0)),
        pl.BlockSpec((tq, qw), lambda t: (rows(nxt(t)), group(nxt(t)))),
        pl.BlockSpec((SEQ, qw), lambda t: (batch(nxt(t)), group(nxt(t)))),
        pl.BlockSpec((1, LANES, SEQ), lambda t: (batch(t), group(t), 0)),
    ] + [_resident(a.shape) for a in extra]
    sbuf = pltpu.VMEM((2, SEQ, tq), F32)
    mbuf = pltpu.VMEM((2, SUBLANES, tq), F32)
    return pl.pallas_call(
        functools.partial(_stream_attn_kernel, masked=masked, finish=finish),
        grid=(n_items,),
        in_specs=in_specs,
        out_specs=pl.BlockSpec((tq, LANES), lambda t: (rows(t), group(t))),
        out_shape=jax.ShapeDtypeStruct((TOKENS, groups * LANES), BF16),
        scratch_shapes=[sbuf, mbuf, sbuf, mbuf],
        compiler_params=_params("arbitrary"),
        name=name,
    )(q, q, k, vt, *extra)


def _diff_finish(result, lam_ref, g_ref, *, lam_init):
    (acc1, l1), (acc2, l2) = result
    lf = lam_ref[...]
    lam = (jnp.exp(jnp.sum(lf[0:1] * lf[1:2], axis=-1, keepdims=True))
           - jnp.exp(jnp.sum(lf[2:3] * lf[3:4], axis=-1, keepdims=True)) + lam_init)
    o = (acc1 / l1 - lam * (acc2 / l2)).T
    return _rms(o, g_ref[...]) * (1.0 - lam_init)


def _diff_attn(qa, ka, va, lam_vec, subln_g, lam_init):
    return _stream_attn(qa, ka, va, (lam_vec, subln_g),
                        functools.partial(_diff_finish, lam_init=lam_init),
                        groups=A_HEADS, masked=True, name="diff_attn")


NA_QROWS = 4
NA_KROWS = NA_QROWS + NA_ROWS
NA_BLOCKS = GRID_ROWS // NA_QROWS
NA_PLACEMENTS = 3


def _na_window_start(g):
    lo, hi = 0, GRID_ROWS - NA_KROWS
    s = g * NA_QROWS - NA_ROWS // 2
    if isinstance(g, int):
        return min(max(s, lo), hi)
    return jnp.clip(s, lo, hi)


def _na_attn_kernel(q_ref, k_ref, v_ref, bias_ref, o_ref):
    nq = NA_QROWS * GRID_W
    nk = NA_KROWS * GRID_W
    lane = lax.broadcasted_iota(jnp.int32, (nq, 2 * B_DH), 1)
    first = lane < B_DH

    def block(g, carry):
        var = jnp.where(g == 0, 0, jnp.where(g == NA_BLOCKS - 1, 2, 1))
        qstart = pl.multiple_of(g * nq, nq)
        kstart = pl.multiple_of(_na_window_start(g) * GRID_W, GRID_W)
        q = q_ref[pl.ds(qstart, nq), :]
        k = k_ref[pl.ds(kstart, nk), :]
        v = v_ref[pl.ds(kstart, nk), :]
        zero = jnp.zeros_like(q)
        outs = []
        for j in range(2):
            qj = jnp.where(first if j == 0 else jnp.logical_not(first), q, zero)
            s = lax.dot_general(qj, k, _NT, preferred_element_type=F32) + bias_ref[j, var]
            e = jnp.exp2(s - jnp.max(s, axis=-1, keepdims=True))
            inv = 1.0 / jnp.sum(e, axis=-1, keepdims=True)
            outs.append(_dot(e.astype(BF16), v) * inv)
        o_ref[pl.ds(qstart, nq), :] = jnp.where(first, outs[0], outs[1]).astype(BF16)
        return carry

    lax.fori_loop(0, NA_BLOCKS, block, 0)


def _na_attn(qb, kb, vb, bias):
    hw = 2 * B_DH
    spec = pl.BlockSpec((SEQ, hw), lambda b, p: (b, p))
    bspec = pl.BlockSpec((2,) + bias.shape[1:], lambda b, p: (p, 0, 0, 0))
    return pl.pallas_call(
        _na_attn_kernel,
        grid=(BATCH, B_HEADS // 2),
        in_specs=[spec, spec, spec, bspec],
        out_specs=spec,
        out_shape=jax.ShapeDtypeStruct((TOKENS, B_W), BF16),
        compiler_params=_params("parallel", "parallel"),
        name="na_attn",
    )(qb, kb, vb, bias)


def _na_bias_table(rpb):
    c = np.arange(GRID_W)
    c0 = np.clip(c - NA_COLS // 2, 0, GRID_W - NA_COLS)
    col_in = (c[None, :] >= c0[:, None]) & (c[None, :] < c0[:, None] + NA_COLS)
    dc = np.clip(c[None, :] - c[:, None], -(NA_COLS - 1), NA_COLS - 1) + NA_COLS - 1
    rpb = rpb.astype(F32)
    toe = jnp.zeros(rpb.shape[:2] + (GRID_W, GRID_W), F32)
    for d in range(2 * NA_COLS - 1):
        toe = jnp.where(dc == d, rpb[:, :, d, None, None], toe)
    toe = jnp.where(col_in, toe, -jnp.inf)
    masked = jnp.full((B_HEADS, GRID_W, GRID_W), -jnp.inf, F32)
    tiles = []
    for g in (0, 1, NA_BLOCKS - 1):
        ws = _na_window_start(g)
        for ql in range(NA_QROWS):
            qr = g * NA_QROWS + ql
            r0 = min(max(qr - NA_ROWS // 2, 0), GRID_ROWS - NA_ROWS)
            for kl in range(NA_KROWS):
                kr = ws + kl
                tiles.append(toe[:, kr - qr + NA_ROWS - 1] if r0 <= kr < r0 + NA_ROWS else masked)
    t = jnp.stack(tiles).reshape(NA_PLACEMENTS, NA_QROWS, NA_KROWS, B_HEADS, GRID_W, GRID_W)
    return t.transpose(3, 0, 1, 4, 2, 5).reshape(
        B_HEADS, NA_PLACEMENTS, NA_QROWS * GRID_W, NA_KROWS * GRID_W)


def _odd_in_kernel(x_ref, g_ref, w_ref, gq_ref, gkv_ref, wuq_ref, wk_ref, wvt_ref,
                   cos_ref, sin_ref, xc_ref, gc_ref, q_ref, k_ref, vt_ref):
    h = _rms(x_ref[...], g_ref[...]).astype(BF16)
    cos, sin = cos_ref[...], sin_ref[...]
    lane = lax.broadcasted_iota(jnp.int32, cos.shape, 1)
    lo = lane < D_NOPE + D_ROPE // 2
    scale = (D_NOPE + D_ROPE) ** -0.5 * LOG2E
    o = 0
    xc_ref[...] = _dot(h, w_ref[:, o:o + C_WIDTH]); o += C_WIDTH
    gc_ref[...] = _dot(h, w_ref[:, o:o + C_WIDTH]); o += C_WIDTH
    cq = _dot(h, w_ref[:, o:o + D_Q_RANK]); o += D_Q_RANK
    ckv = _dot(h, w_ref[:, o:o + D_KV_RANK]); o += D_KV_RANK
    kr = _dot(h, w_ref[:, o:o + PAD_DK])
    q = _dot(_rms(cq, gq_ref[...]).astype(BF16), wuq_ref[...])
    ckvn = _rms(ckv, gkv_ref[...]).astype(BF16)
    kn = _dot(ckvn, wk_ref[...])
    kpe = _rope_lanes(kr, cos, sin, D_ROPE // 2, lo)
    for j in range(D_HEADS):
        sl = slice(j * PAD_DK, (j + 1) * PAD_DK)
        q_ref[:, sl] = (_rope_lanes(q[:, sl], cos, sin, D_ROPE // 2, lo) * scale).astype(BF16)
        k_ref[:, sl] = (kn[:, sl] + kpe).astype(BF16)
    vt_ref[0] = lax.dot_general(wvt_ref[...], ckvn, _NT, preferred_element_type=F32).astype(BF16)


def _odd_in(x, g, w_in, gq, gkv, wuq, wk, wvt, cos, sin):
    tm = TOKEN_TILE
    per_seq = SEQ // tm
    tok = lambda w: pl.BlockSpec((tm, w), lambda i: (i, 0))
    pos = pl.BlockSpec((tm, LANES), lambda i: (i % per_seq, 0))
    sds = lambda w, dt: jax.ShapeDtypeStruct((TOKENS, w), dt)
    qw = D_HEADS * PAD_DK
    vw = D_HEADS * D_VDIM
    return pl.pallas_call(
        _odd_in_kernel,
        grid=(TOKENS // tm,),
        in_specs=[tok(D_MODEL)] + [_resident(a.shape) for a in (g, w_in, gq, gkv, wuq, wk, wvt)] + [pos, pos],
        out_specs=[tok(C_WIDTH), tok(C_WIDTH), tok(qw), tok(qw),
                   pl.BlockSpec((1, vw, tm), lambda i: (i // per_seq, 0, i % per_seq))],
        out_shape=[sds(C_WIDTH, F32), sds(C_WIDTH, F32), sds(qw, BF16), sds(qw, BF16),
                   jax.ShapeDtypeStruct((BATCH, vw, SEQ), BF16)],
        compiler_params=_params("parallel"),
        name="odd_in",
    )(x, g, w_in, gq, gkv, wuq, wk, wvt, cos, sin)


def _rglru_kernel(xc_ref, gc_ref, cw_ref, cb_ref, wgate_ref, bgate_ref, lam_ref, o_ref,
                  a_scr, b_scr, h_scr):
    cw = cw_ref[...]
    x = xc_ref[...]
    row = lax.broadcasted_iota(jnp.int32, x.shape, 0)

    def shifted(d):
        rolled = pltpu.roll(x, (-d) % SEQ, 0)
        valid = (row + d >= 0) & (row + d < SEQ)
        return jnp.where(valid, rolled, 0.0)

    lp = (CONV_W - 1) // 2
    u = cb_ref[...] + sum((x if j == lp else shifted(j - lp)) * cw[j:j + 1] for j in range(CONV_W))
    gates = _dot(u.astype(BF16), wgate_ref[0]) + bgate_ref[0]
    lam = lam_ref[0]
    for d in range(2):
        o = 2 * d * LANES
        r_t = _sigmoid(gates[:, o:o + LANES])
        i_t = _sigmoid(gates[:, o + LANES:o + 2 * LANES])
        log_a = (-RG_C * r_t) * jax.nn.softplus(-lam[:, d * LANES:(d + 1) * LANES])
        a = jnp.exp(log_a)
        y = -jnp.tanh(log_a) * (a * a + 1.0)
        mult = jnp.where(y > 0.0, y * lax.rsqrt(y), 0.0)
        mult = jnp.where(row == (SEQ - 1 if d else 0), 1.0, mult)
        a_scr[d] = a
        b_scr[d] = mult * i_t * u

    n_tiles = SEQ // SUBLANES
    srow = lax.broadcasted_iota(jnp.int32, (SUBLANES, LANES), 0)

    def tile_scan(a, b, h_prev, reverse):
        for d in (1, 2, 4):
            if reverse:
                keep = srow < SUBLANES - d
                shift = SUBLANES - d
            else:
                keep = srow >= d
                shift = d
            a_s = jnp.where(keep, pltpu.roll(a, shift, 0), 1.0)
            b_s = jnp.where(keep, pltpu.roll(b, shift, 0), 0.0)
            b = a * b_s + b
            a = a * a_s
        return a * h_prev + b

    def step(i, carry):
        hf, hr = carry
        tf = pl.multiple_of(i * SUBLANES, SUBLANES)
        h = tile_scan(a_scr[0, pl.ds(tf, SUBLANES), :], b_scr[0, pl.ds(tf, SUBLANES), :], hf, False)
        h_scr[0, pl.ds(tf, SUBLANES), :] = h
        hf = jnp.broadcast_to(h[SUBLANES - 1:SUBLANES, :], h.shape)
        tr = pl.multiple_of((n_tiles - 1 - i) * SUBLANES, SUBLANES)
        h = tile_scan(a_scr[1, pl.ds(tr, SUBLANES), :], b_scr[1, pl.ds(tr, SUBLANES), :], hr, True)
        h_scr[1, pl.ds(tr, SUBLANES), :] = h
        hr = jnp.broadcast_to(h[0:1, :], h.shape)
        return hf, hr

    z = jnp.zeros((SUBLANES, LANES), F32)
    lax.fori_loop(0, n_tiles, step, (z, z), unroll=SCAN_UNROLL)
    o_ref[...] = (jax.nn.gelu(gc_ref[...]) * (h_scr[0] + h_scr[1])).astype(BF16)


def _rglru(xc, gc, conv_w, conv_b, wgate, bgate, lam):
    ng = C_WIDTH // LANES
    seq = pl.BlockSpec((SEQ, LANES), lambda b, g: (b, g))
    grp = lambda a: pl.BlockSpec((1,) + a.shape[1:], lambda b, g: (g, 0, 0))
    return pl.pallas_call(
        _rglru_kernel,
        grid=(BATCH, ng),
        in_specs=[seq, seq,
                  pl.BlockSpec((CONV_W, LANES), lambda b, g: (0, g)),
                  pl.BlockSpec((1, LANES), lambda b, g: (0, g)),
                  grp(wgate), grp(bgate), grp(lam)],
        out_specs=seq,
        out_shape=jax.ShapeDtypeStruct((TOKENS, C_WIDTH), BF16),
        scratch_shapes=[pltpu.VMEM((2, SEQ, LANES), F32)] * 3,
        compiler_params=_params("parallel", "parallel"),
        name="rglru",
    )(xc, gc, conv_w, conv_b, wgate, bgate, lam)


def _rglru_gate_params(rg_wa, rg_ba, rg_wx, rg_bx, rg_lam):
    ng = C_WIDTH // LANES
    per = LANES // C_BW

    def dense(w):
        w = w.reshape(ng, per, C_BW, C_BW)
        eye = jnp.eye(per, dtype=w.dtype)
        return jnp.einsum('gpcd,pq->gpcqd', w, eye).reshape(ng, LANES, LANES)

    wgate = jnp.concatenate([dense(rg_wa[0]), dense(rg_wx[0]), dense(rg_wa[1]), dense(rg_wx[1])], axis=-1)
    grp = lambda v: v.reshape(ng, 1, LANES)
    bgate = jnp.concatenate([grp(rg_ba[0]), grp(rg_bx[0]), grp(rg_ba[1]), grp(rg_bx[1])], axis=-1)
    lam = jnp.concatenate([grp(rg_lam[0]), grp(rg_lam[1])], axis=-1)
    return wgate.astype(BF16), bgate.astype(F32), lam.astype(F32)


def _mla_finish(result):
    (acc1, l1), (acc2, l2) = result
    return jnp.concatenate([acc1 / l1, acc2 / l2], axis=0).T


def _mla_attn(q, k, v):
    return _stream_attn(q, k, v, (), _mla_finish, groups=D_HEADS // 2, masked=False, name="mla_attn")


def _rope_angles(dim):
    inv = 1.0 / (ROPE_THETA ** (jnp.arange(0, dim, 2, dtype=F32) / dim))
    ang = jnp.arange(SEQ, dtype=F32)[:, None] * inv[None, :]
    return jnp.cos(ang), jnp.sin(ang)


def _even_rope_tables():
    cos, sin = _rope_angles(A_DH)
    reps = LANES // A_DH
    return (jnp.tile(jnp.concatenate([cos, cos], -1), (1, reps)),
            jnp.tile(jnp.concatenate([-sin, sin], -1), (1, reps)))


def _odd_rope_tables():
    cos, sin = _rope_angles(D_ROPE)
    ones = jnp.ones((SEQ, D_NOPE), F32)
    zn = jnp.zeros((SEQ, D_NOPE), F32)
    zp = jnp.zeros((SEQ, PAD_DK - D_NOPE - D_ROPE), F32)
    return (jnp.concatenate([ones, cos, cos, zp], -1),
            jnp.concatenate([zn, -sin, sin, zp], -1))


def _odd_weights(w_in, wuq, wukv):
    base = 2 * C_WIDTH + D_Q_RANK + D_KV_RANK
    zl = jnp.zeros((D_MODEL, D_NOPE), w_in.dtype)
    zr = jnp.zeros((D_MODEL, PAD_DK - D_NOPE - D_ROPE), w_in.dtype)
    w_in_p = jnp.concatenate([w_in[:, :base], zl, w_in[:, base:], zr], axis=-1)
    dqk = D_NOPE + D_ROPE
    wuq_p = jnp.pad(wuq.reshape(D_Q_RANK, D_HEADS, dqk), ((0, 0), (0, 0), (0, PAD_DK - dqk)))
    wukv_h = wukv.reshape(D_KV_RANK, D_HEADS, D_NOPE + D_VDIM)
    wk_p = jnp.pad(wukv_h[:, :, :D_NOPE], ((0, 0), (0, 0), (0, PAD_DK - D_NOPE)))
    wv = wukv_h[:, :, D_NOPE:]
    return (w_in_p.astype(BF16), wuq_p.reshape(D_Q_RANK, -1).astype(BF16),
            wk_p.reshape(D_KV_RANK, -1).astype(BF16), wv.reshape(D_KV_RANK, -1).T.astype(BF16))


def kernel(x, ffn1_norm, ffn1_wg, ffn1_wu, ffn1_wd, mix_norm, ffn2_norm, ffn2_wg, ffn2_wu, ffn2_wd, final_norm, ev_w_in, ev_w_out, diff_lam, diff_subln, na_rpb, od_w_in, od_w_out, conv_w, conv_b, rg_wa, rg_ba, rg_wx, rg_bx, rg_lam, mla_gq, mla_gkv, mla_wuq, mla_wukv):
    row = lambda v: v.reshape(1, -1).astype(F32)
    bf = lambda w: w.astype(BF16)
    xt = x.reshape(TOKENS, D_MODEL)

    xt = _ffn(xt, row(ffn1_norm[0]), bf(ffn1_wg[0]), bf(ffn1_wu[0]), bf(ffn1_wd[0]))
    cos_a, sin_a = _even_rope_tables()
    wvt_a = bf(ev_w_in[0][:, 2 * A_QK:2 * A_QK + A_V].T)
    qa, ka, va, qb, kb, vb = _even_in(xt, row(mix_norm[0]), bf(ev_w_in[0]), wvt_a, cos_a, sin_a)
    lam_init0 = 0.8 - 0.6 * math.exp(-0.3 * 0)
    oa = _diff_attn(qa, ka, va, diff_lam[0].astype(F32), row(diff_subln[0]), lam_init0)
    ob = _na_attn(qb, kb, vb, _na_bias_table(na_rpb[0]) * LOG2E)
    xt = _ffn(xt, row(ffn2_norm[0]), bf(ffn2_wg[0]), bf(ffn2_wu[0]), bf(ffn2_wd[0]),
              mix=(oa, ob, bf(ev_w_out[0])))

    xt = _ffn(xt, row(ffn1_norm[1]), bf(ffn1_wg[1]), bf(ffn1_wu[1]), bf(ffn1_wd[1]))
    cos_d, sin_d = _odd_rope_tables()
    w_in_p, wuq_p, wk_p, wv_p = _odd_weights(od_w_in[0], mla_wuq[0], mla_wukv[0])
    xc, gc, q, k, v = _odd_in(xt, row(mix_norm[1]), w_in_p, row(mla_gq[0]), row(mla_gkv[0]),
                              wuq_p, wk_p, wv_p, cos_d, sin_d)
    wgate, bgate, lam = _rglru_gate_params(rg_wa[0], rg_ba[0], rg_wx[0], rg_bx[0], rg_lam[0])
    oc = _rglru(xc, gc, conv_w[0].astype(F32), row(conv_b[0]), wgate, bgate, lam)
    od = _mla_attn(q, k, v)
    xt = _ffn(xt, row(ffn2_norm[1]), bf(ffn2_wg[1]), bf(ffn2_wu[1]), bf(ffn2_wd[1]),
              mix=(oc, od, bf(od_w_out[0])), final_g=row(final_norm))
    return xt.reshape(BATCH, SEQ, D_MODEL)
```

```python
import functools
import itertools
import math

import jax
import jax.numpy as jnp
import numpy as np
from jax import lax
from jax.experimental import pallas as pl
from jax.experimental.pallas import tpu as pltpu

F32 = jnp.float32
BF16 = jnp.bfloat16

D_MODEL = 1024
BATCH = 4
SEQ = 4096
DEPTH = 2
TOKENS = BATCH * SEQ
RMS_EPS = 1e-6
ROPE_THETA = 10000.0
GRID_W = 64
GRID_ROWS = SEQ // GRID_W
D_FF = 2816

A_HEADS = 4
A_DH = 64
A_QK = A_HEADS * 2 * A_DH
A_V = A_HEADS * 2 * A_DH
B_HEADS = 8
B_DH = 64
B_W = B_HEADS * B_DH
NA_ROWS = 8
NA_COLS = 16
C_WIDTH = 512
C_BLOCKS = 8
C_BW = C_WIDTH // C_BLOCKS
CONV_W = 4
RG_C = 8.0
D_HEADS = 8
D_NOPE = 64
D_ROPE = 32
D_VDIM = 64
D_Q_RANK = 256
D_KV_RANK = 128
EVEN_IN = 2 * A_QK + A_V + 3 * B_W
ODD_IN = 2 * C_WIDTH + D_Q_RANK + D_KV_RANK + D_ROPE

LANES = 128
SUBLANES = 8
VMEM_LIMIT_BYTES = 56 * 1024 * 1024

TOKEN_TILE = 512
FF_CHUNK = 256
Q_TILE = 512
PV_TILE = 256
KEY_CHUNK = 512
LOGIT_CHUNK = 2048
PAD_DK = 128
SCAN_UNROLL = 8

_NT = (((1,), (1,)), ((), ()))

LOG2E = math.log2(math.e)


def _params(*sem):
    return pltpu.CompilerParams(dimension_semantics=sem, vmem_limit_bytes=VMEM_LIMIT_BYTES)


def _resident(shape):
    nd = len(shape)
    return pl.BlockSpec(shape, lambda *_: (0,) * nd, pipeline_mode=pl.Buffered(1))


def _rms(x, g):
    return x * lax.rsqrt(jnp.mean(x * x, axis=-1, keepdims=True) + RMS_EPS) * g


def _dot(a, b):
    return jnp.dot(a, b, preferred_element_type=F32)


def _sigmoid(x):
    return 0.5 * (jnp.tanh(0.5 * x) + 1.0)


def _rope_lanes(x, cos, sin_signed, half, lo_mask):
    n = x.shape[-1]
    partner = jnp.where(lo_mask, pltpu.roll(x, n - half, 1), pltpu.roll(x, half, 1))
    return x * cos + partner * sin_signed


def _ffn_kernel(*refs, has_mix, has_final):
    it = iter(refs)
    x_ref = next(it)
    if has_mix:
        o1_ref, o2_ref, wo_ref = next(it), next(it), next(it)
    g_ref, wg_ref, wu_ref, wd_ref = next(it), next(it), next(it), next(it)
    gf_ref = next(it) if has_final else None
    y_ref = next(it)
    acc_ref = next(it)

    x = x_ref[...]
    if has_mix:
        half = wo_ref.shape[0] // 2
        x = x + _dot(o1_ref[...], wo_ref[:half, :]) + _dot(o2_ref[...], wo_ref[half:, :])
    n = _rms(x, g_ref[...]).astype(BF16)
    for c in range(D_FF // FF_CHUNK):
        sl = slice(c * FF_CHUNK, (c + 1) * FF_CHUNK)
        gate = _dot(n, wg_ref[:, sl])
        up = _dot(n, wu_ref[:, sl])
        h = (gate * jax.nn.sigmoid(gate) * up).astype(BF16)
        d = _dot(h, wd_ref[sl, :])
        if c == 0:
            acc_ref[...] = d
        else:
            acc_ref[...] += d
    y = x + 0.5 * acc_ref[...]
    if has_final:
        y = _rms(y, gf_ref[...])
    y_ref[...] = y


def _ffn(x, g, wg, wu, wd, layer, mix=None, final_g=None):
    tm = TOKEN_TILE
    tok = lambda w: pl.BlockSpec((tm, w), lambda i: (i, 0))
    slab = lambda a: pl.BlockSpec((None,) + a.shape[1:], lambda i: (layer, 0, 0),
                                  pipeline_mode=pl.Buffered(1))
    args, specs = [x], [tok(D_MODEL)]
    if mix is not None:
        o1, o2, wo = mix
        args += [o1, o2, wo]
        specs += [tok(o1.shape[1]), tok(o2.shape[1]), _resident(wo.shape)]
    args += [g, wg, wu, wd]
    specs += [_resident(g.shape), slab(wg), slab(wu), slab(wd)]
    if final_g is not None:
        args.append(final_g)
        specs.append(_resident(final_g.shape))
    return pl.pallas_call(
        functools.partial(_ffn_kernel, has_mix=mix is not None, has_final=final_g is not None),
        grid=(TOKENS // tm,),
        in_specs=specs,
        out_specs=tok(D_MODEL),
        out_shape=jax.ShapeDtypeStruct((TOKENS, D_MODEL), F32),
        scratch_shapes=[pltpu.VMEM((tm, D_MODEL), F32)],
        compiler_params=_params("parallel"),
        name="ffn_mix" if mix is not None else "ffn",
    )(*args)


def _even_in_kernel(x_ref, g_ref, w_ref, wvt_ref, cos_ref, sin_ref,
                    qa_ref, ka_ref, vat_ref, qb_ref, kb_ref, vb_ref):
    h = _rms(x_ref[...], g_ref[...]).astype(BF16)
    cos, sin = cos_ref[...], sin_ref[...]
    lane = lax.broadcasted_iota(jnp.int32, cos.shape, 1)
    lo = (lane % A_DH) < (A_DH // 2)
    scale = A_DH ** -0.5 * LOG2E

    def proj(i):
        return _dot(h, w_ref[:, i * A_QK:(i + 1) * A_QK])

    qa, ka = proj(0), proj(1)
    for j in range(A_QK // LANES):
        sl = slice(j * LANES, (j + 1) * LANES)
        qa_ref[:, sl] = (_rope_lanes(qa[:, sl], cos, sin, A_DH // 2, lo) * scale).astype(BF16)
        ka_ref[:, sl] = _rope_lanes(ka[:, sl], cos, sin, A_DH // 2, lo).astype(BF16)
    vat_ref[0] = lax.dot_general(wvt_ref[...], h, _NT, preferred_element_type=F32).astype(BF16)
    qb_ref[...] = (proj(3) * (B_DH ** -0.5 * LOG2E)).astype(BF16)
    kb_ref[...] = proj(4).astype(BF16)
    vb_ref[...] = proj(5).astype(BF16)


def _even_in(x, g, w_in, wvt, cos, sin):
    tm = TOKEN_TILE
    per_seq = SEQ // tm
    tok = lambda w: pl.BlockSpec((tm, w), lambda i: (i, 0))
    pos = pl.BlockSpec((tm, LANES), lambda i: (i % per_seq, 0))
    tr = pl.BlockSpec((1, A_V, tm), lambda i: (i // per_seq, 0, i % per_seq))
    out = jax.ShapeDtypeStruct((TOKENS, A_QK), BF16)
    out_t = jax.ShapeDtypeStruct((BATCH, A_V, SEQ), BF16)
    return pl.pallas_call(
        _even_in_kernel,
        grid=(TOKENS // tm,),
        in_specs=[tok(D_MODEL), _resident(g.shape), _resident(w_in.shape), _resident(wvt.shape), pos, pos],
        out_specs=[tok(A_QK), tok(A_QK), tr, tok(B_W), tok(B_W), tok(B_W)],
        out_shape=[out, out, out_t, out, out, out],
        compiler_params=_params("parallel"),
        name="even_in",
    )(x, g, w_in, wvt, cos, sin)


def _logit_chunks(q_ref, k_ref, s_buf, m_buf, masked):
    q = q_ref[...]
    tq, qw = q.shape
    if masked:
        lane = lax.broadcasted_iota(jnp.int32, q.shape, 1)
        zero = jnp.zeros_like(q)
        qs = [jnp.where(lane < qw // 2, q, zero), jnp.where(lane >= qw // 2, q, zero)]
        kcols = [slice(None), slice(None)]
    else:
        qs = [q[:, :qw // 2], q[:, qw // 2:]]
        kcols = [slice(0, qw // 2), slice(qw // 2, qw)]
    groups8 = LOGIT_CHUNK // SUBLANES
    mrun = [None, None]
    for c in range(SEQ // LOGIT_CHUNK):
        keys = slice(c * LOGIT_CHUNK, (c + 1) * LOGIT_CHUNK)
        for j in range(2):
            s = lax.dot_general(k_ref[keys, kcols[j]], qs[j], _NT, preferred_element_type=F32)
            s_buf[j, keys, :] = s
            part = jnp.max(s.reshape(groups8, SUBLANES, tq), axis=0)
            mrun[j] = part if mrun[j] is None else jnp.maximum(mrun[j], part)
            yield
    for j in range(2):
        m_buf[j] = jnp.broadcast_to(jnp.max(mrun[j], axis=0, keepdims=True), (SUBLANES, tq))


def _softmax_pv_chunks(s_buf, m_buf, vt_ref, vrows, emit):
    groups8 = KEY_CHUNK // SUBLANES
    for sub in range(s_buf.shape[-1] // PV_TILE):
        cols = slice(sub * PV_TILE, (sub + 1) * PV_TILE)
        m = [m_buf[j, :, cols][None] for j in range(2)]
        acc = [None, None]
        lrun = [None, None]
        for c in range(SEQ // KEY_CHUNK):
            keys = slice(c * KEY_CHUNK, (c + 1) * KEY_CHUNK)
            for j in range(2):
                e = jnp.exp2(s_buf[j, keys, cols].reshape(groups8, SUBLANES, PV_TILE) - m[j])
                part = jnp.sum(e, axis=0)
                lrun[j] = part if lrun[j] is None else lrun[j] + part
                pv = _dot(vt_ref[0, vrows[j], keys], e.reshape(KEY_CHUNK, PV_TILE).astype(BF16))
                acc[j] = pv if acc[j] is None else acc[j] + pv
                yield
        emit(sub * PV_TILE, [(acc[j], jnp.sum(lrun[j], axis=0, keepdims=True)) for j in range(2)])


def _stream_attn_kernel(*refs, masked, finish):
    qf_ref, qn_ref, kn_ref, vt_ref = refs[:4]
    extra = refs[4:-5]
    o_ref, s_a, m_a, s_b, m_b = refs[-5:]
    t = pl.program_id(0)
    dv = vt_ref.shape[1]
    vrows = [slice(None)] * 2 if masked else [slice(0, dv // 2), slice(dv // 2, dv)]

    @pl.when(t == 0)
    def _():
        for _ in _logit_chunks(qf_ref, kn_ref, s_a, m_a, masked):
            pass

    def emit(first, result):
        o_ref[first:first + PV_TILE, :] = finish(result, *extra).astype(o_ref.dtype)

    def step(nxt, cur):
        logits = _logit_chunks(qn_ref, kn_ref, nxt[0], nxt[1], masked)
        softmax = _softmax_pv_chunks(cur[0], cur[1], vt_ref, vrows, emit)
        n_softmax = (o_ref.shape[0] // PV_TILE) * (SEQ // KEY_CHUNK) * 2
        ratio = n_softmax // ((SEQ // LOGIT_CHUNK) * 2)
        for i, _ in enumerate(softmax):
            if i % ratio == 0:
                next(logits, None)
        for _ in logits:
            pass

    pl.when((t & 1) == 0)(lambda: step((s_b, m_b), (s_a, m_a)))
    pl.when((t & 1) == 1)(lambda: step((s_a, m_a), (s_b, m_b)))


def _stream_attn(q, k, vt, extra, finish, *, groups, masked, name):
    tq = Q_TILE
    nq = SEQ // tq
    n_items = BATCH * groups * nq
    qw = q.shape[1] // groups

    def rows(t):
        return (t // (groups * nq)) * nq + t % nq

    def group(t):
        return (t // nq) % groups

    def batch(t):
        return t // (groups * nq)

    nxt = lambda t: jnp.minimum(t + 1, n_items - 1)
    in_specs = [
        pl.BlockSpec((tq, qw), lambda t: (0, 0)),
        pl.BlockSpec((tq, qw), lambda t: (rows(nxt(t)), group(nxt(t)))),
        pl.BlockSpec((SEQ, qw), lambda t: (batch(nxt(t)), group(nxt(t)))),
        pl.BlockSpec((1, LANES, SEQ), lambda t: (batch(t), group(t), 0)),
    ] + [_resident(a.shape) for a in extra]
    sbuf = pltpu.VMEM((2, SEQ, tq), F32)
    mbuf = pltpu.VMEM((2, SUBLANES, tq), F32)
    return pl.pallas_call(
        functools.partial(_stream_attn_kernel, masked=masked, finish=finish),
        grid=(n_items,),
        in_specs=in_specs,
        out_specs=pl.BlockSpec((tq, LANES), lambda t: (rows(t), group(t))),
        out_shape=jax.ShapeDtypeStruct((TOKENS, groups * LANES), BF16),
        scratch_shapes=[sbuf, mbuf, sbuf, mbuf],
        compiler_params=_params("arbitrary"),
        name=name,
    )(q, q, k, vt, *extra)


def _diff_finish(result, lam_ref, g_ref, *, lam_init):
    (acc1, l1), (acc2, l2) = result
    lf = lam_ref[...]
    lam = (jnp.exp(jnp.sum(lf[0:1] * lf[1:2], axis=-1, keepdims=True))
           - jnp.exp(jnp.sum(lf[2:3] * lf[3:4], axis=-1, keepdims=True)) + lam_init)
    o = (acc1 / l1 - lam * (acc2 / l2)).T
    return _rms(o, g_ref[...]) * (1.0 - lam_init)


def _diff_attn(qa, ka, va, lam_vec, subln_g, lam_init):
    return _stream_attn(qa, ka, va, (lam_vec, subln_g),
                        functools.partial(_diff_finish, lam_init=lam_init),
                        groups=A_HEADS, masked=True, name="diff_attn")


NA_QROWS = 4
NA_KROWS = NA_QROWS + NA_ROWS
NA_BLOCKS = GRID_ROWS // NA_QROWS
NA_PLACEMENTS = 3


def _na_window_start(g):
    lo, hi = 0, GRID_ROWS - NA_KROWS
    s = g * NA_QROWS - NA_ROWS // 2
    if isinstance(g, int):
        return min(max(s, lo), hi)
    return jnp.clip(s, lo, hi)


def _na_attn_kernel(q_ref, k_ref, v_ref, bias_ref, o_ref):
    nq = NA_QROWS * GRID_W
    nk = NA_KROWS * GRID_W
    lane = lax.broadcasted_iota(jnp.int32, (nq, 2 * B_DH), 1)
    first = lane < B_DH

    def block(g, carry):
        var = jnp.where(g == 0, 0, jnp.where(g == NA_BLOCKS - 1, 2, 1))
        qstart = pl.multiple_of(g * nq, nq)
        kstart = pl.multiple_of(_na_window_start(g) * GRID_W, GRID_W)
        q = q_ref[pl.ds(qstart, nq), :]
        k = k_ref[pl.ds(kstart, nk), :]
        v = v_ref[pl.ds(kstart, nk), :]
        zero = jnp.zeros_like(q)
        outs = []
        for j in range(2):
            qj = jnp.where(first if j == 0 else jnp.logical_not(first), q, zero)
            s = lax.dot_general(qj, k, _NT, preferred_element_type=F32) + bias_ref[j, var]
            e = jnp.exp2(s - jnp.max(s, axis=-1, keepdims=True))
            inv = 1.0 / jnp.sum(e, axis=-1, keepdims=True)
            outs.append(_dot(e.astype(BF16), v) * inv)
        o_ref[pl.ds(qstart, nq), :] = jnp.where(first, outs[0], outs[1]).astype(BF16)
        return carry

    lax.fori_loop(0, NA_BLOCKS, block, 0, unroll=2)


def _na_attn(qb, kb, vb, bias):
    hw = 2 * B_DH
    spec = pl.BlockSpec((SEQ, hw), lambda b, p: (b, p))
    bspec = pl.BlockSpec((2,) + bias.shape[1:], lambda b, p: (p, 0, 0, 0))
    return pl.pallas_call(
        _na_attn_kernel,
        grid=(BATCH, B_HEADS // 2),
        in_specs=[spec, spec, spec, bspec],
        out_specs=spec,
        out_shape=jax.ShapeDtypeStruct((TOKENS, B_W), BF16),
        compiler_params=_params("parallel", "parallel"),
        name="na_attn",
    )(qb, kb, vb, bias)


def _na_bias_table(rpb):
    c = np.arange(GRID_W)
    c0 = np.clip(c - NA_COLS // 2, 0, GRID_W - NA_COLS)
    col_in = (c[None, :] >= c0[:, None]) & (c[None, :] < c0[:, None] + NA_COLS)
    dc = np.clip(c[None, :] - c[:, None], -(NA_COLS - 1), NA_COLS - 1) + NA_COLS - 1
    rpb = rpb.astype(F32) * LOG2E
    n_dr = 2 * NA_ROWS - 1
    toe = jnp.zeros((B_HEADS, GRID_W, n_dr, GRID_W), F32)
    for d in range(2 * NA_COLS - 1):
        toe = jnp.where((dc == d)[:, None, :], rpb[:, None, :, d, None], toe)
    toe = jnp.where(col_in[:, None, :], toe, -jnp.inf).reshape(B_HEADS, GRID_W, n_dr * GRID_W)
    masked = lambda n: jnp.full((B_HEADS, GRID_W, n * GRID_W), -jnp.inf, F32)
    rows = []
    for g in (0, 1, NA_BLOCKS - 1):
        ws = _na_window_start(g)
        for ql in range(NA_QROWS):
            qr = g * NA_QROWS + ql
            r0 = min(max(qr - NA_ROWS // 2, 0), GRID_ROWS - NA_ROWS)
            lead = r0 - ws
            trail = NA_KROWS - NA_ROWS - lead
            first = r0 - qr + NA_ROWS - 1
            own = toe[:, :, first * GRID_W:(first + NA_ROWS) * GRID_W]
            rows.append(jnp.concatenate(
                ([masked(lead)] if lead else []) + [own] + ([masked(trail)] if trail else []), axis=-1))
    return jnp.stack(rows, axis=1).reshape(
        B_HEADS, NA_PLACEMENTS, NA_QROWS * GRID_W, NA_KROWS * GRID_W)


def _odd_in_kernel(x_ref, g_ref, w_ref, gq_ref, gkv_ref, wuq_ref, wk_ref, wvt_ref,
                   cos_ref, sin_ref, xc_ref, gc_ref, q_ref, k_ref, vt_ref):
    h = _rms(x_ref[...], g_ref[...]).astype(BF16)
    cos, sin = cos_ref[...], sin_ref[...]
    lane = lax.broadcasted_iota(jnp.int32, cos.shape, 1)
    lo = lane < D_NOPE + D_ROPE // 2
    scale = (D_NOPE + D_ROPE) ** -0.5 * LOG2E
    o = 0
    xc_ref[...] = _dot(h, w_ref[:, o:o + C_WIDTH]); o += C_WIDTH
    gc_ref[...] = _dot(h, w_ref[:, o:o + C_WIDTH]); o += C_WIDTH
    cq = _dot(h, w_ref[:, o:o + D_Q_RANK]); o += D_Q_RANK
    ckv = _dot(h, w_ref[:, o:o + D_KV_RANK]); o += D_KV_RANK
    kr = _dot(h, w_ref[:, o:o + PAD_DK])
    q = _dot(_rms(cq, gq_ref[...]).astype(BF16), wuq_ref[...])
    ckvn = _rms(ckv, gkv_ref[...]).astype(BF16)
    kn = _dot(ckvn, wk_ref[...])
    kpe = _rope_lanes(kr, cos, sin, D_ROPE // 2, lo)
    for j in range(D_HEADS):
        sl = slice(j * PAD_DK, (j + 1) * PAD_DK)
        q_ref[:, sl] = (_rope_lanes(q[:, sl], cos, sin, D_ROPE // 2, lo) * scale).astype(BF16)
        k_ref[:, sl] = (kn[:, sl] + kpe).astype(BF16)
    vt_ref[0] = lax.dot_general(wvt_ref[...], ckvn, _NT, preferred_element_type=F32).astype(BF16)


def _odd_in(x, g, w_in, gq, gkv, wuq, wk, wvt, cos, sin):
    tm = TOKEN_TILE
    per_seq = SEQ // tm
    tok = lambda w: pl.BlockSpec((tm, w), lambda i: (i, 0))
    pos = pl.BlockSpec((tm, LANES), lambda i: (i % per_seq, 0))
    sds = lambda w, dt: jax.ShapeDtypeStruct((TOKENS, w), dt)
    qw = D_HEADS * PAD_DK
    vw = D_HEADS * D_VDIM
    return pl.pallas_call(
        _odd_in_kernel,
        grid=(TOKENS // tm,),
        in_specs=[tok(D_MODEL)] + [_resident(a.shape) for a in (g, w_in, gq, gkv, wuq, wk, wvt)] + [pos, pos],
        out_specs=[tok(C_WIDTH), tok(C_WIDTH), tok(qw), tok(qw),
                   pl.BlockSpec((1, vw, tm), lambda i: (i // per_seq, 0, i % per_seq))],
        out_shape=[sds(C_WIDTH, F32), sds(C_WIDTH, F32), sds(qw, BF16), sds(qw, BF16),
                   jax.ShapeDtypeStruct((BATCH, vw, SEQ), BF16)],
        compiler_params=_params("parallel"),
        name="odd_in",
    )(x, g, w_in, gq, gkv, wuq, wk, wvt, cos, sin)


def _rglru_kernel(xc_ref, gc_ref, cw_ref, cb_ref, wgate_ref, bgate_ref, lam_ref, o_ref,
                  a_scr, b_scr, h_scr):
    cw = cw_ref[...]
    x = xc_ref[...]
    row = lax.broadcasted_iota(jnp.int32, x.shape, 0)

    def shifted(d):
        rolled = pltpu.roll(x, (-d) % SEQ, 0)
        valid = (row + d >= 0) & (row + d < SEQ)
        return jnp.where(valid, rolled, 0.0)

    lp = (CONV_W - 1) // 2
    u = cb_ref[...] + sum((x if j == lp else shifted(j - lp)) * cw[j:j + 1] for j in range(CONV_W))
    gates = _dot(u.astype(BF16), wgate_ref[0]) + bgate_ref[0]
    lam = lam_ref[0]
    for d in range(2):
        o = 2 * d * LANES
        r_t = _sigmoid(gates[:, o:o + LANES])
        i_t = _sigmoid(gates[:, o + LANES:o + 2 * LANES])
        log_a = (-RG_C * r_t) * jax.nn.softplus(-lam[:, d * LANES:(d + 1) * LANES])
        a = jnp.exp(log_a)
        y = -jnp.tanh(log_a) * (a * a + 1.0)
        mult = jnp.where(y > 0.0, y * lax.rsqrt(y), 0.0)
        mult = jnp.where(row == (SEQ - 1 if d else 0), 1.0, mult)
        a_scr[d] = a
        b_scr[d] = mult * i_t * u

    n_tiles = SEQ // SUBLANES
    srow = lax.broadcasted_iota(jnp.int32, (SUBLANES, LANES), 0)

    def tile_scan(a, b, h_prev, reverse):
        for d in (1, 2, 4):
            if reverse:
                keep = srow < SUBLANES - d
                shift = SUBLANES - d
            else:
                keep = srow >= d
                shift = d
            a_s = jnp.where(keep, pltpu.roll(a, shift, 0), 1.0)
            b_s = jnp.where(keep, pltpu.roll(b, shift, 0), 0.0)
            b = a * b_s + b
            a = a * a_s
        return a * h_prev + b

    def step(i, carry):
        hf, hr = carry
        tf = pl.multiple_of(i * SUBLANES, SUBLANES)
        h = tile_scan(a_scr[0, pl.ds(tf, SUBLANES), :], b_scr[0, pl.ds(tf, SUBLANES), :], hf, False)
        h_scr[0, pl.ds(tf, SUBLANES), :] = h
        hf = jnp.broadcast_to(h[SUBLANES - 1:SUBLANES, :], h.shape)
        tr = pl.multiple_of((n_tiles - 1 - i) * SUBLANES, SUBLANES)
        h = tile_scan(a_scr[1, pl.ds(tr, SUBLANES), :], b_scr[1, pl.ds(tr, SUBLANES), :], hr, True)
        h_scr[1, pl.ds(tr, SUBLANES), :] = h
        hr = jnp.broadcast_to(h[0:1, :], h.shape)
        return hf, hr

    z = jnp.zeros((SUBLANES, LANES), F32)
    lax.fori_loop(0, n_tiles, step, (z, z), unroll=SCAN_UNROLL)
    o_ref[...] = (jax.nn.gelu(gc_ref[...]) * (h_scr[0] + h_scr[1])).astype(BF16)


def _rglru(xc, gc, conv_w, conv_b, wgate, bgate, lam):
    ng = C_WIDTH // LANES
    seq = pl.BlockSpec((SEQ, LANES), lambda b, g: (b, g))
    grp = lambda a: pl.BlockSpec((1,) + a.shape[1:], lambda b, g: (g, 0, 0))
    return pl.pallas_call(
        _rglru_kernel,
        grid=(BATCH, ng),
        in_specs=[seq, seq,
                  pl.BlockSpec((CONV_W, LANES), lambda b, g: (0, g)),
                  pl.BlockSpec((1, LANES), lambda b, g: (0, g)),
                  grp(wgate), grp(bgate), grp(lam)],
        out_specs=seq,
        out_shape=jax.ShapeDtypeStruct((TOKENS, C_WIDTH), BF16),
        scratch_shapes=[pltpu.VMEM((2, SEQ, LANES), F32)] * 3,
        compiler_params=_params("parallel", "parallel"),
        name="rglru",
    )(xc, gc, conv_w, conv_b, wgate, bgate, lam)


def _rglru_gate_params(rg_wa, rg_ba, rg_wx, rg_bx, rg_lam):
    ng = C_WIDTH // LANES
    per = LANES // C_BW

    def dense(w):
        w = w.reshape(ng, per, C_BW, C_BW)
        eye = jnp.eye(per, dtype=w.dtype)
        return jnp.einsum('gpcd,pq->gpcqd', w, eye).reshape(ng, LANES, LANES)

    wgate = jnp.concatenate([dense(rg_wa[0]), dense(rg_wx[0]), dense(rg_wa[1]), dense(rg_wx[1])], axis=-1)
    grp = lambda v: v.reshape(ng, 1, LANES)
    bgate = jnp.concatenate([grp(rg_ba[0]), grp(rg_bx[0]), grp(rg_ba[1]), grp(rg_bx[1])], axis=-1)
    lam = jnp.concatenate([grp(rg_lam[0]), grp(rg_lam[1])], axis=-1)
    return wgate.astype(BF16), bgate.astype(F32), lam.astype(F32)


def _mla_finish(result):
    (acc1, l1), (acc2, l2) = result
    return jnp.concatenate([acc1 / l1, acc2 / l2], axis=0).T


def _mla_attn(q, k, v):
    return _stream_attn(q, k, v, (), _mla_finish, groups=D_HEADS // 2, masked=False, name="mla_attn")


def _rope_angles(dim):
    inv = 1.0 / (ROPE_THETA ** (jnp.arange(0, dim, 2, dtype=F32) / dim))
    ang = jnp.arange(SEQ, dtype=F32)[:, None] * inv[None, :]
    return jnp.cos(ang), jnp.sin(ang)


def _even_rope_tables():
    cos, sin = _rope_angles(A_DH)
    reps = LANES // A_DH
    return (jnp.tile(jnp.concatenate([cos, cos], -1), (1, reps)),
            jnp.tile(jnp.concatenate([-sin, sin], -1), (1, reps)))


def _odd_rope_tables():
    cos, sin = _rope_angles(D_ROPE)
    ones = jnp.ones((SEQ, D_NOPE), F32)
    zn = jnp.zeros((SEQ, D_NOPE), F32)
    zp = jnp.zeros((SEQ, PAD_DK - D_NOPE - D_ROPE), F32)
    return (jnp.concatenate([ones, cos, cos, zp], -1),
            jnp.concatenate([zn, -sin, sin, zp], -1))


def _odd_weights(w_in, wuq, wukv):
    base = 2 * C_WIDTH + D_Q_RANK + D_KV_RANK
    zl = jnp.zeros((D_MODEL, D_NOPE), w_in.dtype)
    zr = jnp.zeros((D_MODEL, PAD_DK - D_NOPE - D_ROPE), w_in.dtype)
    w_in_p = jnp.concatenate([w_in[:, :base], zl, w_in[:, base:], zr], axis=-1)
    dqk = D_NOPE + D_ROPE
    wuq_p = jnp.pad(wuq.reshape(D_Q_RANK, D_HEADS, dqk), ((0, 0), (0, 0), (0, PAD_DK - dqk)))
    wukv_h = wukv.reshape(D_KV_RANK, D_HEADS, D_NOPE + D_VDIM)
    wk_p = jnp.pad(wukv_h[:, :, :D_NOPE], ((0, 0), (0, 0), (0, PAD_DK - D_NOPE)))
    wv = wukv_h[:, :, D_NOPE:]
    return (w_in_p.astype(BF16), wuq_p.reshape(D_Q_RANK, -1).astype(BF16),
            wk_p.reshape(D_KV_RANK, -1).astype(BF16), wv.reshape(D_KV_RANK, -1).T.astype(BF16))


def kernel(x, ffn1_norm, ffn1_wg, ffn1_wu, ffn1_wd, mix_norm, ffn2_norm, ffn2_wg, ffn2_wu, ffn2_wd, final_norm, ev_w_in, ev_w_out, diff_lam, diff_subln, na_rpb, od_w_in, od_w_out, conv_w, conv_b, rg_wa, rg_ba, rg_wx, rg_bx, rg_lam, mla_gq, mla_gkv, mla_wuq, mla_wukv):
    row = lambda v: v.reshape(1, -1).astype(F32)
    bf = lambda w: w.astype(BF16)
    xt = x.reshape(TOKENS, D_MODEL)

    ffn1 = (bf(ffn1_wg), bf(ffn1_wu), bf(ffn1_wd))
    ffn2 = (bf(ffn2_wg), bf(ffn2_wu), bf(ffn2_wd))

    xt = _ffn(xt, row(ffn1_norm[0]), *ffn1, 0)
    cos_a, sin_a = _even_rope_tables()
    wvt_a = bf(ev_w_in[0][:, 2 * A_QK:2 * A_QK + A_V].T)
    qa, ka, va, qb, kb, vb = _even_in(xt, row(mix_norm[0]), bf(ev_w_in[0]), wvt_a, cos_a, sin_a)
    lam_init0 = 0.8 - 0.6 * math.exp(-0.3 * 0)
    oa = _diff_attn(qa, ka, va, diff_lam[0].astype(F32), row(diff_subln[0]), lam_init0)
    ob = _na_attn(qb, kb, vb, _na_bias_table(na_rpb[0]))
    xt = _ffn(xt, row(ffn2_norm[0]), *ffn2, 0, mix=(oa, ob, bf(ev_w_out[0])))

    xt = _ffn(xt, row(ffn1_norm[1]), *ffn1, 1)
    cos_d, sin_d = _odd_rope_tables()
    w_in_p, wuq_p, wk_p, wv_p = _odd_weights(od_w_in[0], mla_wuq[0], mla_wukv[0])
    xc, gc, q, k, v = _odd_in(xt, row(mix_norm[1]), w_in_p, row(mla_gq[0]), row(mla_gkv[0]),
                              wuq_p, wk_p, wv_p, cos_d, sin_d)
    wgate, bgate, lam = _rglru_gate_params(rg_wa[0], rg_ba[0], rg_wx[0], rg_bx[0], rg_lam[0])
    oc = _rglru(xc, gc, conv_w[0].astype(F32), row(conv_b[0]), wgate, bgate, lam)
    od = _mla_attn(q, k, v)
    xt = _ffn(xt, row(ffn2_norm[1]), *ffn2, 1, mix=(oc, od, bf(od_w_out[0])), final_g=row(final_norm))
    return xt.reshape(BATCH, SEQ, D_MODEL)
```

```python
import functools
import itertools
import math

import jax
import jax.numpy as jnp
import numpy as np
from jax import lax
from jax.experimental import pallas as pl
from jax.experimental.pallas import tpu as pltpu

F32 = jnp.float32
BF16 = jnp.bfloat16

D_MODEL = 1024
BATCH = 4
SEQ = 4096
DEPTH = 2
TOKENS = BATCH * SEQ
RMS_EPS = 1e-6
ROPE_THETA = 10000.0
GRID_W = 64
GRID_ROWS = SEQ // GRID_W
D_FF = 2816

A_HEADS = 4
A_DH = 64
A_QK = A_HEADS * 2 * A_DH
A_V = A_HEADS * 2 * A_DH
B_HEADS = 8
B_DH = 64
B_W = B_HEADS * B_DH
NA_ROWS = 8
NA_COLS = 16
C_WIDTH = 512
C_BLOCKS = 8
C_BW = C_WIDTH // C_BLOCKS
CONV_W = 4
RG_C = 8.0
D_HEADS = 8
D_NOPE = 64
D_ROPE = 32
D_VDIM = 64
D_Q_RANK = 256
D_KV_RANK = 128
EVEN_IN = 2 * A_QK + A_V + 3 * B_W
ODD_IN = 2 * C_WIDTH + D_Q_RANK + D_KV_RANK + D_ROPE

LANES = 128
SUBLANES = 8
VMEM_LIMIT_BYTES = 56 * 1024 * 1024

TOKEN_TILE = 512
FF_CHUNK = 256
Q_TILE = 256
PV_TILE = 256
KEY_CHUNK = 1024
LOGIT_CHUNK = 1024
PAD_DK = 128
SCAN_UNROLL = 8

_NT = (((1,), (1,)), ((), ()))

LOG2E = math.log2(math.e)


def _params(*sem):
    return pltpu.CompilerParams(dimension_semantics=sem, vmem_limit_bytes=VMEM_LIMIT_BYTES)


def _resident(shape):
    nd = len(shape)
    return pl.BlockSpec(shape, lambda *_: (0,) * nd, pipeline_mode=pl.Buffered(1))


def _rms(x, g):
    return x * lax.rsqrt(jnp.mean(x * x, axis=-1, keepdims=True) + RMS_EPS) * g


def _dot(a, b):
    return jnp.dot(a, b, preferred_element_type=F32)


def _sigmoid(x):
    return 0.5 * (jnp.tanh(0.5 * x) + 1.0)


def _rope_lanes(x, cos, sin_signed, half, lo_mask):
    n = x.shape[-1]
    partner = jnp.where(lo_mask, pltpu.roll(x, n - half, 1), pltpu.roll(x, half, 1))
    return x * cos + partner * sin_signed


def _ffn_kernel(*refs, has_mix, has_final):
    it = iter(refs)
    x_ref = next(it)
    if has_mix:
        o1_ref, o2_ref, wo_ref = next(it), next(it), next(it)
    g_ref, wg_ref, wu_ref, wd_ref = next(it), next(it), next(it), next(it)
    gf_ref = next(it) if has_final else None
    y_ref = next(it)
    acc_ref = next(it)

    x = x_ref[...]
    if has_mix:
        half = wo_ref.shape[0] // 2
        x = x + _dot(o1_ref[...], wo_ref[:half, :]) + _dot(o2_ref[...], wo_ref[half:, :])
    n = _rms(x, g_ref[...]).astype(BF16)
    for c in range(D_FF // FF_CHUNK):
        sl = slice(c * FF_CHUNK, (c + 1) * FF_CHUNK)
        gate = _dot(n, wg_ref[:, sl])
        up = _dot(n, wu_ref[:, sl])
        h = (gate * jax.nn.sigmoid(gate) * up).astype(BF16)
        d = _dot(h, wd_ref[sl, :])
        if c == 0:
            acc_ref[...] = d
        else:
            acc_ref[...] += d
    y = x + 0.5 * acc_ref[...]
    if has_final:
        y = _rms(y, gf_ref[...])
    y_ref[...] = y


def _ffn(x, g, wg, wu, wd, layer, mix=None, final_g=None):
    tm = TOKEN_TILE
    tok = lambda w: pl.BlockSpec((tm, w), lambda i: (i, 0))
    slab = lambda a: pl.BlockSpec((None,) + a.shape[1:], lambda i: (layer, 0, 0),
                                  pipeline_mode=pl.Buffered(1))
    args, specs = [x], [tok(D_MODEL)]
    if mix is not None:
        o1, o2, wo = mix
        args += [o1, o2, wo]
        specs += [tok(o1.shape[1]), tok(o2.shape[1]), _resident(wo.shape)]
    args += [g, wg, wu, wd]
    specs += [_resident(g.shape), slab(wg), slab(wu), slab(wd)]
    if final_g is not None:
        args.append(final_g)
        specs.append(_resident(final_g.shape))
    return pl.pallas_call(
        functools.partial(_ffn_kernel, has_mix=mix is not None, has_final=final_g is not None),
        grid=(TOKENS // tm,),
        in_specs=specs,
        out_specs=tok(D_MODEL),
        out_shape=jax.ShapeDtypeStruct((TOKENS, D_MODEL), F32),
        scratch_shapes=[pltpu.VMEM((tm, D_MODEL), F32)],
        compiler_params=_params("parallel"),
        name="ffn_mix" if mix is not None else "ffn",
    )(*args)


def _even_in_kernel(x_ref, g_ref, w_ref, wvt_ref, cos_ref, sin_ref,
                    qa_ref, ka_ref, vat_ref, qb_ref, kb_ref, vb_ref):
    h = _rms(x_ref[...], g_ref[...]).astype(BF16)
    cos, sin = cos_ref[...], sin_ref[...]
    lane = lax.broadcasted_iota(jnp.int32, cos.shape, 1)
    lo = (lane % A_DH) < (A_DH // 2)
    scale = A_DH ** -0.5 * LOG2E

    def proj(i):
        return _dot(h, w_ref[:, i * A_QK:(i + 1) * A_QK])

    qa, ka = proj(0), proj(1)
    for j in range(A_QK // LANES):
        sl = slice(j * LANES, (j + 1) * LANES)
        qa_ref[:, sl] = (_rope_lanes(qa[:, sl], cos, sin, A_DH // 2, lo) * scale).astype(BF16)
        ka_ref[:, sl] = _rope_lanes(ka[:, sl], cos, sin, A_DH // 2, lo).astype(BF16)
    vat_ref[0] = lax.dot_general(wvt_ref[...], h, _NT, preferred_element_type=F32).astype(BF16)
    qb_ref[...] = (proj(3) * (B_DH ** -0.5 * LOG2E)).astype(BF16)
    kb_ref[...] = proj(4).astype(BF16)
    vb_ref[...] = proj(5).astype(BF16)


def _even_in(x, g, w_in, wvt, cos, sin):
    tm = TOKEN_TILE
    per_seq = SEQ // tm
    tok = lambda w: pl.BlockSpec((tm, w), lambda i: (i, 0))
    pos = pl.BlockSpec((tm, LANES), lambda i: (i % per_seq, 0))
    tr = pl.BlockSpec((1, A_V, tm), lambda i: (i // per_seq, 0, i % per_seq))
    out = jax.ShapeDtypeStruct((TOKENS, A_QK), BF16)
    out_t = jax.ShapeDtypeStruct((BATCH, A_V, SEQ), BF16)
    return pl.pallas_call(
        _even_in_kernel,
        grid=(TOKENS // tm,),
        in_specs=[tok(D_MODEL), _resident(g.shape), _resident(w_in.shape), _resident(wvt.shape), pos, pos],
        out_specs=[tok(A_QK), tok(A_QK), tr, tok(B_W), tok(B_W), tok(B_W)],
        out_shape=[out, out, out_t, out, out, out],
        compiler_params=_params("parallel"),
        name="even_in",
    )(x, g, w_in, wvt, cos, sin)


def _logit_chunks(q_ref, k_ref, s_buf, m_buf):
    q = q_ref[...]
    tq, qw = q.shape
    lane = lax.broadcasted_iota(jnp.int32, q.shape, 1)
    zero = jnp.zeros_like(q)
    q2 = jnp.concatenate([jnp.where(lane < qw // 2, q, zero), jnp.where(lane >= qw // 2, q, zero)], axis=0)
    groups8 = LOGIT_CHUNK // SUBLANES
    mrun = [None, None]
    for c in range(SEQ // LOGIT_CHUNK):
        keys = slice(c * LOGIT_CHUNK, (c + 1) * LOGIT_CHUNK)
        s2 = lax.dot_general(k_ref[keys, :], q2, _NT, preferred_element_type=F32)
        for j in range(2):
            s = s2[:, j * tq:(j + 1) * tq]
            s_buf[j, keys, :] = s
            part = jnp.max(s.reshape(groups8, SUBLANES, tq), axis=0)
            mrun[j] = part if mrun[j] is None else jnp.maximum(mrun[j], part)
        yield
    for j in range(2):
        m_buf[j] = jnp.broadcast_to(jnp.max(mrun[j], axis=0, keepdims=True), (SUBLANES, tq))


def _softmax_pv_chunks(s_buf, m_buf, vt_ref, vrows, acc_scr, l_scr):
    tq = s_buf.shape[-1]
    groups8 = KEY_CHUNK // SUBLANES
    m = [m_buf[j][None] for j in range(2)]
    acc = [None, None]
    lrun = [None, None]
    for c in range(SEQ // KEY_CHUNK):
        keys = slice(c * KEY_CHUNK, (c + 1) * KEY_CHUNK)
        for j in range(2):
            e = jnp.exp2(s_buf[j, keys, :].reshape(groups8, SUBLANES, tq) - m[j])
            part = jnp.sum(e, axis=0)
            lrun[j] = part if lrun[j] is None else lrun[j] + part
            pv = _dot(vt_ref[0, vrows[j], keys], e.reshape(KEY_CHUNK, tq).astype(BF16))
            acc[j] = pv if acc[j] is None else acc[j] + pv
            yield
    for j in range(2):
        acc_scr[j, :acc[j].shape[0], :] = acc[j]
        l_scr[j] = lrun[j]


def _stream_attn_kernel(*refs, n_items, masked, finish):
    qf_ref, qn_ref, kn_ref, vt_ref = refs[:4]
    extra = refs[4:-7]
    o_ref, s_a, m_a, s_b, m_b, acc_scr, l_scr = refs[-7:]
    t = pl.program_id(0)
    dv = vt_ref.shape[1]
    vrows = [slice(None)] * 2 if masked else [slice(0, dv // 2), slice(dv // 2, dv)]
    rows = dv if masked else dv // 2

    @pl.when(t == 0)
    def _():
        for _ in _logit_chunks(qf_ref, kn_ref, s_a, m_a):
            pass
        acc_scr[...] = jnp.zeros_like(acc_scr)
        l_scr[...] = jnp.ones_like(l_scr)

    def finish_previous():
        result = [(acc_scr[j, :rows, :], jnp.sum(l_scr[j], axis=0, keepdims=True)) for j in range(2)]
        o_ref[...] = finish(result, *extra).astype(o_ref.dtype)

    def step(nxt, cur):
        finish_previous()
        logits = _logit_chunks(qn_ref, kn_ref, nxt[0], nxt[1])
        softmax = _softmax_pv_chunks(cur[0], cur[1], vt_ref, vrows, acc_scr, l_scr)
        ratio = (SEQ // KEY_CHUNK) * 2 // (SEQ // LOGIT_CHUNK)
        for i, _ in enumerate(softmax):
            if i % ratio == 0:
                next(logits, None)
        for _ in logits:
            pass

    live = t < n_items
    pl.when(live & ((t & 1) == 0))(lambda: step((s_b, m_b), (s_a, m_a)))
    pl.when(live & ((t & 1) == 1))(lambda: step((s_a, m_a), (s_b, m_b)))
    pl.when(t == n_items)(finish_previous)


def _stream_attn(q, k, vt, extra, finish, *, groups, masked, name):
    tq = Q_TILE
    nq = SEQ // tq
    n_items = BATCH * groups * nq
    qw = q.shape[1] // groups

    def rows(t):
        return (t // (groups * nq)) * nq + t % nq

    def group(t):
        return (t // nq) % groups

    def batch(t):
        return t // (groups * nq)

    nxt = lambda t: jnp.minimum(t + 1, n_items - 1)
    cur = lambda t: jnp.minimum(t, n_items - 1)
    prv = lambda t: jnp.maximum(t - 1, 0)
    in_specs = [
        pl.BlockSpec((tq, qw), lambda t: (0, 0)),
        pl.BlockSpec((tq, qw), lambda t: (rows(nxt(t)), group(nxt(t)))),
        pl.BlockSpec((SEQ, qw), lambda t: (batch(nxt(t)), group(nxt(t)))),
        pl.BlockSpec((1, LANES, SEQ), lambda t: (batch(cur(t)), group(cur(t)), 0)),
    ] + [_resident(a.shape) for a in extra]
    sbuf = pltpu.VMEM((2, SEQ, tq), F32)
    mbuf = pltpu.VMEM((2, SUBLANES, tq), F32)
    return pl.pallas_call(
        functools.partial(_stream_attn_kernel, n_items=n_items, masked=masked, finish=finish),
        grid=(n_items + 1,),
        in_specs=in_specs,
        out_specs=pl.BlockSpec((tq, LANES), lambda t: (rows(prv(t)), group(prv(t)))),
        out_shape=jax.ShapeDtypeStruct((TOKENS, groups * LANES), BF16),
        scratch_shapes=[sbuf, mbuf, sbuf, mbuf,
                        pltpu.VMEM((2, LANES, tq), F32), pltpu.VMEM((2, SUBLANES, tq), F32)],
        compiler_params=_params("arbitrary"),
        name=name,
    )(q, q, k, vt, *extra)


def _diff_finish(result, lam_ref, g_ref, *, lam_init):
    (acc1, l1), (acc2, l2) = result
    lf = lam_ref[...]
    lam = (jnp.exp(jnp.sum(lf[0:1] * lf[1:2], axis=-1, keepdims=True))
           - jnp.exp(jnp.sum(lf[2:3] * lf[3:4], axis=-1, keepdims=True)) + lam_init)
    o = (acc1 / l1 - lam * (acc2 / l2)).T
    return _rms(o, g_ref[...]) * (1.0 - lam_init)


def _diff_attn(qa, ka, va, lam_vec, subln_g, lam_init):
    return _stream_attn(qa, ka, va, (lam_vec, subln_g),
                        functools.partial(_diff_finish, lam_init=lam_init),
                        groups=A_HEADS, masked=True, name="diff_attn")


NA_QROWS = 4
NA_KROWS = NA_QROWS + NA_ROWS
NA_BLOCKS = GRID_ROWS // NA_QROWS
NA_PLACEMENTS = 3


def _na_window_start(g):
    lo, hi = 0, GRID_ROWS - NA_KROWS
    s = g * NA_QROWS - NA_ROWS // 2
    if isinstance(g, int):
        return min(max(s, lo), hi)
    return jnp.clip(s, lo, hi)


def _na_attn_kernel(q_ref, k_ref, v_ref, bias_ref, o_ref):
    nq = NA_QROWS * GRID_W
    nk = NA_KROWS * GRID_W
    lane = lax.broadcasted_iota(jnp.int32, (nq, 2 * B_DH), 1)
    first = lane < B_DH

    def block(g, carry):
        var = jnp.where(g == 0, 0, jnp.where(g == NA_BLOCKS - 1, 2, 1))
        qstart = pl.multiple_of(g * nq, nq)
        kstart = pl.multiple_of(_na_window_start(g) * GRID_W, GRID_W)
        q = q_ref[pl.ds(qstart, nq), :]
        k = k_ref[pl.ds(kstart, nk), :]
        v = v_ref[pl.ds(kstart, nk), :]
        zero = jnp.zeros_like(q)
        outs = []
        for j in range(2):
            qj = jnp.where(first if j == 0 else jnp.logical_not(first), q, zero)
            s = lax.dot_general(qj, k, _NT, preferred_element_type=F32) + bias_ref[j, var]
            e = jnp.exp2(s - jnp.max(s, axis=-1, keepdims=True))
            inv = 1.0 / jnp.sum(e, axis=-1, keepdims=True)
            outs.append(_dot(e.astype(BF16), v) * inv)
        o_ref[pl.ds(qstart, nq), :] = jnp.where(first, outs[0], outs[1]).astype(BF16)
        return carry

    lax.fori_loop(0, NA_BLOCKS, block, 0, unroll=2)


def _na_attn(qb, kb, vb, bias):
    hw = 2 * B_DH
    spec = pl.BlockSpec((SEQ, hw), lambda b, p: (b, p))
    bspec = pl.BlockSpec((2,) + bias.shape[1:], lambda b, p: (p, 0, 0, 0))
    return pl.pallas_call(
        _na_attn_kernel,
        grid=(BATCH, B_HEADS // 2),
        in_specs=[spec, spec, spec, bspec],
        out_specs=spec,
        out_shape=jax.ShapeDtypeStruct((TOKENS, B_W), BF16),
        compiler_params=_params("parallel", "parallel"),
        name="na_attn",
    )(qb, kb, vb, bias)


def _na_bias_table(rpb):
    c = np.arange(GRID_W)
    c0 = np.clip(c - NA_COLS // 2, 0, GRID_W - NA_COLS)
    col_in = (c[None, :] >= c0[:, None]) & (c[None, :] < c0[:, None] + NA_COLS)
    dc = np.clip(c[None, :] - c[:, None], -(NA_COLS - 1), NA_COLS - 1) + NA_COLS - 1
    rpb = rpb.astype(F32) * LOG2E
    n_dr = 2 * NA_ROWS - 1
    toe = jnp.zeros((B_HEADS, GRID_W, n_dr, GRID_W), F32)
    for d in range(2 * NA_COLS - 1):
        toe = jnp.where((dc == d)[:, None, :], rpb[:, None, :, d, None], toe)
    toe = jnp.where(col_in[:, None, :], toe, -jnp.inf).reshape(B_HEADS, GRID_W, n_dr * GRID_W)
    masked = lambda n: jnp.full((B_HEADS, GRID_W, n * GRID_W), -jnp.inf, F32)
    rows = []
    for g in (0, 1, NA_BLOCKS - 1):
        ws = _na_window_start(g)
        for ql in range(NA_QROWS):
            qr = g * NA_QROWS + ql
            r0 = min(max(qr - NA_ROWS // 2, 0), GRID_ROWS - NA_ROWS)
            lead = r0 - ws
            trail = NA_KROWS - NA_ROWS - lead
            first = r0 - qr + NA_ROWS - 1
            own = toe[:, :, first * GRID_W:(first + NA_ROWS) * GRID_W]
            rows.append(jnp.concatenate(
                ([masked(lead)] if lead else []) + [own] + ([masked(trail)] if trail else []), axis=-1))
    return jnp.stack(rows, axis=1).reshape(
        B_HEADS, NA_PLACEMENTS, NA_QROWS * GRID_W, NA_KROWS * GRID_W)


def _odd_in_kernel(x_ref, g_ref, w_ref, gq_ref, gkv_ref, wuq_ref, wk_ref, wvt_ref,
                   cos_ref, sin_ref, xc_ref, gc_ref, q_ref, k_ref, vt_ref):
    h = _rms(x_ref[...], g_ref[...]).astype(BF16)
    cos, sin = cos_ref[...], sin_ref[...]
    lane = lax.broadcasted_iota(jnp.int32, cos.shape, 1)
    lo = lane < D_NOPE + D_ROPE // 2
    scale = (D_NOPE + D_ROPE) ** -0.5 * LOG2E
    o = 0
    xc_ref[...] = _dot(h, w_ref[:, o:o + C_WIDTH]); o += C_WIDTH
    gc_ref[...] = _dot(h, w_ref[:, o:o + C_WIDTH]); o += C_WIDTH
    cq = _dot(h, w_ref[:, o:o + D_Q_RANK]); o += D_Q_RANK
    ckv = _dot(h, w_ref[:, o:o + D_KV_RANK]); o += D_KV_RANK
    kr = _dot(h, w_ref[:, o:o + PAD_DK])
    q = _dot(_rms(cq, gq_ref[...]).astype(BF16), wuq_ref[...])
    ckvn = _rms(ckv, gkv_ref[...]).astype(BF16)
    kn = _dot(ckvn, wk_ref[...])
    kpe = _rope_lanes(kr, cos, sin, D_ROPE // 2, lo)
    for j in range(D_HEADS):
        sl = slice(j * PAD_DK, (j + 1) * PAD_DK)
        q_ref[:, sl] = (_rope_lanes(q[:, sl], cos, sin, D_ROPE // 2, lo) * scale).astype(BF16)
        k_ref[:, sl] = (kn[:, sl] + kpe).astype(BF16)
    vt_ref[0] = lax.dot_general(wvt_ref[...], ckvn, _NT, preferred_element_type=F32).astype(BF16)


def _odd_in(x, g, w_in, gq, gkv, wuq, wk, wvt, cos, sin):
    tm = TOKEN_TILE
    per_seq = SEQ // tm
    tok = lambda w: pl.BlockSpec((tm, w), lambda i: (i, 0))
    pos = pl.BlockSpec((tm, LANES), lambda i: (i % per_seq, 0))
    sds = lambda w, dt: jax.ShapeDtypeStruct((TOKENS, w), dt)
    qw = D_HEADS * PAD_DK
    vw = D_HEADS * D_VDIM
    return pl.pallas_call(
        _odd_in_kernel,
        grid=(TOKENS // tm,),
        in_specs=[tok(D_MODEL)] + [_resident(a.shape) for a in (g, w_in, gq, gkv, wuq, wk, wvt)] + [pos, pos],
        out_specs=[tok(C_WIDTH), tok(C_WIDTH), tok(qw), tok(qw),
                   pl.BlockSpec((1, vw, tm), lambda i: (i // per_seq, 0, i % per_seq))],
        out_shape=[sds(C_WIDTH, F32), sds(C_WIDTH, F32), sds(qw, BF16), sds(qw, BF16),
                   jax.ShapeDtypeStruct((BATCH, vw, SEQ), BF16)],
        compiler_params=_params("parallel"),
        name="odd_in",
    )(x, g, w_in, gq, gkv, wuq, wk, wvt, cos, sin)


def _rglru_kernel(xc_ref, gc_ref, cw_ref, cb_ref, wgate_ref, bgate_ref, lam_ref, o_ref,
                  a_scr, b_scr, h_scr):
    cw = cw_ref[...]
    x = xc_ref[...]
    row = lax.broadcasted_iota(jnp.int32, x.shape, 0)

    def shifted(d):
        rolled = pltpu.roll(x, (-d) % SEQ, 0)
        valid = (row + d >= 0) & (row + d < SEQ)
        return jnp.where(valid, rolled, 0.0)

    lp = (CONV_W - 1) // 2
    u = cb_ref[...] + sum((x if j == lp else shifted(j - lp)) * cw[j:j + 1] for j in range(CONV_W))
    gates = _dot(u.astype(BF16), wgate_ref[0]) + bgate_ref[0]
    lam = lam_ref[0]
    for d in range(2):
        o = 2 * d * LANES
        r_t = _sigmoid(gates[:, o:o + LANES])
        i_t = _sigmoid(gates[:, o + LANES:o + 2 * LANES])
        log_a = (-RG_C * r_t) * jax.nn.softplus(-lam[:, d * LANES:(d + 1) * LANES])
        a = jnp.exp(log_a)
        y = -jnp.tanh(log_a) * (a * a + 1.0)
        mult = jnp.where(y > 0.0, y * lax.rsqrt(y), 0.0)
        mult = jnp.where(row == (SEQ - 1 if d else 0), 1.0, mult)
        a_scr[d] = a
        b_scr[d] = mult * i_t * u

    n_tiles = SEQ // SUBLANES
    srow = lax.broadcasted_iota(jnp.int32, (SUBLANES, LANES), 0)

    def tile_scan(a, b, h_prev, reverse):
        for d in (1, 2, 4):
            if reverse:
                keep = srow < SUBLANES - d
                shift = SUBLANES - d
            else:
                keep = srow >= d
                shift = d
            a_s = jnp.where(keep, pltpu.roll(a, shift, 0), 1.0)
            b_s = jnp.where(keep, pltpu.roll(b, shift, 0), 0.0)
            b = a * b_s + b
            a = a * a_s
        return a * h_prev + b

    def step(i, carry):
        hf, hr = carry
        tf = pl.multiple_of(i * SUBLANES, SUBLANES)
        h = tile_scan(a_scr[0, pl.ds(tf, SUBLANES), :], b_scr[0, pl.ds(tf, SUBLANES), :], hf, False)
        h_scr[0, pl.ds(tf, SUBLANES), :] = h
        hf = jnp.broadcast_to(h[SUBLANES - 1:SUBLANES, :], h.shape)
        tr = pl.multiple_of((n_tiles - 1 - i) * SUBLANES, SUBLANES)
        h = tile_scan(a_scr[1, pl.ds(tr, SUBLANES), :], b_scr[1, pl.ds(tr, SUBLANES), :], hr, True)
        h_scr[1, pl.ds(tr, SUBLANES), :] = h
        hr = jnp.broadcast_to(h[0:1, :], h.shape)
        return hf, hr

    z = jnp.zeros((SUBLANES, LANES), F32)
    lax.fori_loop(0, n_tiles, step, (z, z), unroll=SCAN_UNROLL)
    o_ref[...] = (jax.nn.gelu(gc_ref[...]) * (h_scr[0] + h_scr[1])).astype(BF16)


def _rglru(xc, gc, conv_w, conv_b, wgate, bgate, lam):
    ng = C_WIDTH // LANES
    seq = pl.BlockSpec((SEQ, LANES), lambda b, g: (b, g))
    grp = lambda a: pl.BlockSpec((1,) + a.shape[1:], lambda b, g: (g, 0, 0))
    return pl.pallas_call(
        _rglru_kernel,
        grid=(BATCH, ng),
        in_specs=[seq, seq,
                  pl.BlockSpec((CONV_W, LANES), lambda b, g: (0, g)),
                  pl.BlockSpec((1, LANES), lambda b, g: (0, g)),
                  grp(wgate), grp(bgate), grp(lam)],
        out_specs=seq,
        out_shape=jax.ShapeDtypeStruct((TOKENS, C_WIDTH), BF16),
        scratch_shapes=[pltpu.VMEM((2, SEQ, LANES), F32)] * 3,
        compiler_params=_params("parallel", "parallel"),
        name="rglru",
    )(xc, gc, conv_w, conv_b, wgate, bgate, lam)


def _rglru_gate_params(rg_wa, rg_ba, rg_wx, rg_bx, rg_lam):
    ng = C_WIDTH // LANES
    per = LANES // C_BW

    def dense(w):
        w = w.reshape(ng, per, C_BW, C_BW)
        eye = jnp.eye(per, dtype=w.dtype)
        return jnp.einsum('gpcd,pq->gpcqd', w, eye).reshape(ng, LANES, LANES)

    wgate = jnp.concatenate([dense(rg_wa[0]), dense(rg_wx[0]), dense(rg_wa[1]), dense(rg_wx[1])], axis=-1)
    grp = lambda v: v.reshape(ng, 1, LANES)
    bgate = jnp.concatenate([grp(rg_ba[0]), grp(rg_bx[0]), grp(rg_ba[1]), grp(rg_bx[1])], axis=-1)
    lam = jnp.concatenate([grp(rg_lam[0]), grp(rg_lam[1])], axis=-1)
    return wgate.astype(BF16), bgate.astype(F32), lam.astype(F32)


def _mla_finish(result):
    (acc1, l1), (acc2, l2) = result
    return jnp.concatenate([acc1 / l1, acc2 / l2], axis=0).T


def _mla_attn(q, k, v):
    return _stream_attn(q, k, v, (), _mla_finish, groups=D_HEADS // 2, masked=False, name="mla_attn")


def _rope_angles(dim):
    inv = 1.0 / (ROPE_THETA ** (jnp.arange(0, dim, 2, dtype=F32) / dim))
    ang = jnp.arange(SEQ, dtype=F32)[:, None] * inv[None, :]
    return jnp.cos(ang), jnp.sin(ang)


def _even_rope_tables():
    cos, sin = _rope_angles(A_DH)
    reps = LANES // A_DH
    return (jnp.tile(jnp.concatenate([cos, cos], -1), (1, reps)),
            jnp.tile(jnp.concatenate([-sin, sin], -1), (1, reps)))


def _odd_rope_tables():
    cos, sin = _rope_angles(D_ROPE)
    ones = jnp.ones((SEQ, D_NOPE), F32)
    zn = jnp.zeros((SEQ, D_NOPE), F32)
    zp = jnp.zeros((SEQ, PAD_DK - D_NOPE - D_ROPE), F32)
    return (jnp.concatenate([ones, cos, cos, zp], -1),
            jnp.concatenate([zn, -sin, sin, zp], -1))


def _odd_weights(w_in, wuq, wukv):
    base = 2 * C_WIDTH + D_Q_RANK + D_KV_RANK
    zl = jnp.zeros((D_MODEL, D_NOPE), w_in.dtype)
    zr = jnp.zeros((D_MODEL, PAD_DK - D_NOPE - D_ROPE), w_in.dtype)
    w_in_p = jnp.concatenate([w_in[:, :base], zl, w_in[:, base:], zr], axis=-1)
    dqk = D_NOPE + D_ROPE
    wuq_p = jnp.pad(wuq.reshape(D_Q_RANK, D_HEADS, dqk), ((0, 0), (0, 0), (0, PAD_DK - dqk)))
    wukv_h = wukv.reshape(D_KV_RANK, D_HEADS, D_NOPE + D_VDIM)
    wk_p = jnp.pad(wukv_h[:, :, :D_NOPE], ((0, 0), (0, 0), (0, PAD_DK - D_NOPE)))
    wv = wukv_h[:, :, D_NOPE:]
    return (w_in_p.astype(BF16), wuq_p.reshape(D_Q_RANK, -1).astype(BF16),
            wk_p.reshape(D_KV_RANK, -1).astype(BF16), wv.reshape(D_KV_RANK, -1).T.astype(BF16))


def kernel(x, ffn1_norm, ffn1_wg, ffn1_wu, ffn1_wd, mix_norm, ffn2_norm, ffn2_wg, ffn2_wu, ffn2_wd, final_norm, ev_w_in, ev_w_out, diff_lam, diff_subln, na_rpb, od_w_in, od_w_out, conv_w, conv_b, rg_wa, rg_ba, rg_wx, rg_bx, rg_lam, mla_gq, mla_gkv, mla_wuq, mla_wukv):
    row = lambda v: v.reshape(1, -1).astype(F32)
    bf = lambda w: w.astype(BF16)
    xt = x.reshape(TOKENS, D_MODEL)

    ffn1 = (bf(ffn1_wg), bf(ffn1_wu), bf(ffn1_wd))
    ffn2 = (bf(ffn2_wg), bf(ffn2_wu), bf(ffn2_wd))

    xt = _ffn(xt, row(ffn1_norm[0]), *ffn1, 0)
    cos_a, sin_a = _even_rope_tables()
    wvt_a = bf(ev_w_in[0][:, 2 * A_QK:2 * A_QK + A_V].T)
    qa, ka, va, qb, kb, vb = _even_in(xt, row(mix_norm[0]), bf(ev_w_in[0]), wvt_a, cos_a, sin_a)
    lam_init0 = 0.8 - 0.6 * math.exp(-0.3 * 0)
    oa = _diff_attn(qa, ka, va, diff_lam[0].astype(F32), row(diff_subln[0]), lam_init0)
    ob = _na_attn(qb, kb, vb, _na_bias_table(na_rpb[0]))
    xt = _ffn(xt, row(ffn2_norm[0]), *ffn2, 0, mix=(oa, ob, bf(ev_w_out[0])))

    xt = _ffn(xt, row(ffn1_norm[1]), *ffn1, 1)
    cos_d, sin_d = _odd_rope_tables()
    w_in_p, wuq_p, wk_p, wv_p = _odd_weights(od_w_in[0], mla_wuq[0], mla_wukv[0])
    xc, gc, q, k, v = _odd_in(xt, row(mix_norm[1]), w_in_p, row(mla_gq[0]), row(mla_gkv[0]),
                              wuq_p, wk_p, wv_p, cos_d, sin_d)
    wgate, bgate, lam = _rglru_gate_params(rg_wa[0], rg_ba[0], rg_wx[0], rg_bx[0], rg_lam[0])
    oc = _rglru(xc, gc, conv_w[0].astype(F32), row(conv_b[0]), wgate, bgate, lam)
    od = _mla_attn(q, k, v)
    xt = _ffn(xt, row(ffn2_norm[1]), *ffn2, 1, mix=(oc, od, bf(od_w_out[0])), final_g=row(final_norm))
    return xt.reshape(BATCH, SEQ, D_MODEL)
```

```python
import functools
import itertools
import math

import jax
import jax.numpy as jnp
import numpy as np
from jax import lax
from jax.experimental import pallas as pl
from jax.experimental.pallas import tpu as pltpu

F32 = jnp.float32
BF16 = jnp.bfloat16

D_MODEL = 1024
BATCH = 4
SEQ = 4096
DEPTH = 2
TOKENS = BATCH * SEQ
RMS_EPS = 1e-6
ROPE_THETA = 10000.0
GRID_W = 64
GRID_ROWS = SEQ // GRID_W
D_FF = 2816

A_HEADS = 4
A_DH = 64
A_QK = A_HEADS * 2 * A_DH
A_V = A_HEADS * 2 * A_DH
B_HEADS = 8
B_DH = 64
B_W = B_HEADS * B_DH
NA_ROWS = 8
NA_COLS = 16
C_WIDTH = 512
C_BLOCKS = 8
C_BW = C_WIDTH // C_BLOCKS
CONV_W = 4
RG_C = 8.0
D_HEADS = 8
D_NOPE = 64
D_ROPE = 32
D_VDIM = 64
D_Q_RANK = 256
D_KV_RANK = 128
EVEN_IN = 2 * A_QK + A_V + 3 * B_W
ODD_IN = 2 * C_WIDTH + D_Q_RANK + D_KV_RANK + D_ROPE

LANES = 128
SUBLANES = 8
VMEM_LIMIT_BYTES = 56 * 1024 * 1024

TOKEN_TILE = 512
FF_CHUNK = 256
Q_TILE = 256
PV_TILE = 256
KEY_CHUNK = 1024
LOGIT_CHUNK = 1024
PAD_DK = 128
SCAN_UNROLL = 8

_NT = (((1,), (1,)), ((), ()))

LOG2E = math.log2(math.e)


def _params(*sem):
    return pltpu.CompilerParams(dimension_semantics=sem, vmem_limit_bytes=VMEM_LIMIT_BYTES)


def _resident(shape):
    nd = len(shape)
    return pl.BlockSpec(shape, lambda *_: (0,) * nd, pipeline_mode=pl.Buffered(1))


def _rms(x, g):
    return x * lax.rsqrt(jnp.mean(x * x, axis=-1, keepdims=True) + RMS_EPS) * g


def _dot(a, b):
    return jnp.dot(a, b, preferred_element_type=F32)


def _sigmoid(x):
    return 0.5 * (jnp.tanh(0.5 * x) + 1.0)


def _rope_lanes(x, cos, sin_signed, half, lo_mask):
    n = x.shape[-1]
    partner = jnp.where(lo_mask, pltpu.roll(x, n - half, 1), pltpu.roll(x, half, 1))
    return x * cos + partner * sin_signed


def _ffn_kernel(*refs, has_mix, has_final):
    it = iter(refs)
    x_ref = next(it)
    if has_mix:
        o1_ref, o2_ref, wo_ref = next(it), next(it), next(it)
    g_ref, wg_ref, wu_ref, wd_ref = next(it), next(it), next(it), next(it)
    gf_ref = next(it) if has_final else None
    y_ref = next(it)
    acc_ref = next(it)

    x = x_ref[...]
    if has_mix:
        half = wo_ref.shape[0] // 2
        x = x + _dot(o1_ref[...], wo_ref[:half, :]) + _dot(o2_ref[...], wo_ref[half:, :])
    n = _rms(x, g_ref[...]).astype(BF16)
    for c in range(D_FF // FF_CHUNK):
        sl = slice(c * FF_CHUNK, (c + 1) * FF_CHUNK)
        gate = _dot(n, wg_ref[:, sl])
        up = _dot(n, wu_ref[:, sl])
        h = (gate * jax.nn.sigmoid(gate) * up).astype(BF16)
        d = _dot(h, wd_ref[sl, :])
        if c == 0:
            acc_ref[...] = d
        else:
            acc_ref[...] += d
    y = x + 0.5 * acc_ref[...]
    if has_final:
        y = _rms(y, gf_ref[...])
    y_ref[...] = y


def _ffn(x, g, wg, wu, wd, layer, mix=None, final_g=None):
    tm = TOKEN_TILE
    tok = lambda w: pl.BlockSpec((tm, w), lambda i: (i, 0))
    slab = lambda a: pl.BlockSpec((None,) + a.shape[1:], lambda i: (layer, 0, 0),
                                  pipeline_mode=pl.Buffered(1))
    args, specs = [x], [tok(D_MODEL)]
    if mix is not None:
        o1, o2, wo = mix
        args += [o1, o2, wo]
        specs += [tok(o1.shape[1]), tok(o2.shape[1]), _resident(wo.shape)]
    args += [g, wg, wu, wd]
    specs += [_resident(g.shape), slab(wg), slab(wu), slab(wd)]
    if final_g is not None:
        args.append(final_g)
        specs.append(_resident(final_g.shape))
    return pl.pallas_call(
        functools.partial(_ffn_kernel, has_mix=mix is not None, has_final=final_g is not None),
        grid=(TOKENS // tm,),
        in_specs=specs,
        out_specs=tok(D_MODEL),
        out_shape=jax.ShapeDtypeStruct((TOKENS, D_MODEL), F32),
        scratch_shapes=[pltpu.VMEM((tm, D_MODEL), F32)],
        compiler_params=_params("parallel"),
        name="ffn_mix" if mix is not None else "ffn",
    )(*args)


def _even_in_kernel(x_ref, g_ref, w_ref, wvt_ref, cos_ref, sin_ref,
                    qa_ref, ka_ref, vt_ref, qb_ref, kb_ref):
    h = _rms(x_ref[...], g_ref[...]).astype(BF16)
    cos, sin = cos_ref[...], sin_ref[...]
    lane = lax.broadcasted_iota(jnp.int32, cos.shape, 1)
    lo = (lane % A_DH) < (A_DH // 2)
    scale = A_DH ** -0.5 * LOG2E

    def proj(i):
        return _dot(h, w_ref[:, i * A_QK:(i + 1) * A_QK])

    qa, ka = proj(0), proj(1)
    for j in range(A_QK // LANES):
        sl = slice(j * LANES, (j + 1) * LANES)
        qa_ref[:, sl] = (_rope_lanes(qa[:, sl], cos, sin, A_DH // 2, lo) * scale).astype(BF16)
        ka_ref[:, sl] = _rope_lanes(ka[:, sl], cos, sin, A_DH // 2, lo).astype(BF16)
    vt_ref[0] = lax.dot_general(wvt_ref[...], h, _NT, preferred_element_type=F32).astype(BF16)
    qb_ref[...] = (proj(3) * (B_DH ** -0.5 * LOG2E)).astype(BF16)
    kb_ref[...] = proj(4).astype(BF16)


def _even_in(x, g, w_in, wvt, cos, sin):
    tm = TOKEN_TILE
    per_seq = SEQ // tm
    tok = lambda w: pl.BlockSpec((tm, w), lambda i: (i, 0))
    pos = pl.BlockSpec((tm, LANES), lambda i: (i % per_seq, 0))
    vw = wvt.shape[0]
    tr = pl.BlockSpec((1, vw, tm), lambda i: (i // per_seq, 0, i % per_seq))
    out = jax.ShapeDtypeStruct((TOKENS, A_QK), BF16)
    out_t = jax.ShapeDtypeStruct((BATCH, vw, SEQ), BF16)
    return pl.pallas_call(
        _even_in_kernel,
        grid=(TOKENS // tm,),
        in_specs=[tok(D_MODEL), _resident(g.shape), _resident(w_in.shape), _resident(wvt.shape), pos, pos],
        out_specs=[tok(A_QK), tok(A_QK), tr, tok(B_W), tok(B_W)],
        out_shape=[out, out, out_t, out, out],
        compiler_params=_params("parallel"),
        name="even_in",
    )(x, g, w_in, wvt, cos, sin)


def _logit_chunks(q_ref, k_ref, s_buf, m_buf):
    q = q_ref[...]
    tq, qw = q.shape
    lane = lax.broadcasted_iota(jnp.int32, q.shape, 1)
    zero = jnp.zeros_like(q)
    q2 = jnp.concatenate([jnp.where(lane < qw // 2, q, zero), jnp.where(lane >= qw // 2, q, zero)], axis=0)
    groups8 = LOGIT_CHUNK // SUBLANES
    mrun = [None, None]
    for c in range(SEQ // LOGIT_CHUNK):
        keys = slice(c * LOGIT_CHUNK, (c + 1) * LOGIT_CHUNK)
        s2 = lax.dot_general(k_ref[keys, :], q2, _NT, preferred_element_type=F32)
        for j in range(2):
            s = s2[:, j * tq:(j + 1) * tq]
            s_buf[j, keys, :] = s
            part = jnp.max(s.reshape(groups8, SUBLANES, tq), axis=0)
            mrun[j] = part if mrun[j] is None else jnp.maximum(mrun[j], part)
        yield
    for j in range(2):
        m_buf[j] = jnp.broadcast_to(jnp.max(mrun[j], axis=0, keepdims=True), (SUBLANES, tq))


def _softmax_pv_chunks(s_buf, m_buf, vt_ref, vrows, acc_scr, l_scr):
    tq = s_buf.shape[-1]
    groups8 = KEY_CHUNK // SUBLANES
    m = [m_buf[j][None] for j in range(2)]
    acc = [None, None]
    lrun = [None, None]
    for c in range(SEQ // KEY_CHUNK):
        keys = slice(c * KEY_CHUNK, (c + 1) * KEY_CHUNK)
        for j in range(2):
            e = jnp.exp2(s_buf[j, keys, :].reshape(groups8, SUBLANES, tq) - m[j])
            part = jnp.sum(e, axis=0)
            lrun[j] = part if lrun[j] is None else lrun[j] + part
            pv = _dot(vt_ref[0, vrows[j], keys], e.reshape(KEY_CHUNK, tq).astype(BF16))
            acc[j] = pv if acc[j] is None else acc[j] + pv
            yield
    for j in range(2):
        acc_scr[j, :acc[j].shape[0], :] = acc[j]
        l_scr[j] = lrun[j]


def _stream_attn_kernel(*refs, n_items, masked, finish):
    qf_ref, qn_ref, kn_ref, vt_ref = refs[:4]
    extra = refs[4:-7]
    o_ref, s_a, m_a, s_b, m_b, acc_scr, l_scr = refs[-7:]
    t = pl.program_id(0)
    dv = vt_ref.shape[1]
    vrows = [slice(None)] * 2 if masked else [slice(0, dv // 2), slice(dv // 2, dv)]
    rows = dv if masked else dv // 2

    @pl.when(t == 0)
    def _():
        for _ in _logit_chunks(qf_ref, kn_ref, s_a, m_a):
            pass
        acc_scr[...] = jnp.zeros_like(acc_scr)
        l_scr[...] = jnp.ones_like(l_scr)

    def finish_previous():
        result = [(acc_scr[j, :rows, :], jnp.sum(l_scr[j], axis=0, keepdims=True)) for j in range(2)]
        o_ref[...] = finish(result, *extra).astype(o_ref.dtype)

    def step(nxt, cur):
        finish_previous()
        logits = _logit_chunks(qn_ref, kn_ref, nxt[0], nxt[1])
        softmax = _softmax_pv_chunks(cur[0], cur[1], vt_ref, vrows, acc_scr, l_scr)
        ratio = (SEQ // KEY_CHUNK) * 2 // (SEQ // LOGIT_CHUNK)
        for i, _ in enumerate(softmax):
            if i % ratio == 0:
                next(logits, None)
        for _ in logits:
            pass

    live = t < n_items
    pl.when(live & ((t & 1) == 0))(lambda: step((s_b, m_b), (s_a, m_a)))
    pl.when(live & ((t & 1) == 1))(lambda: step((s_a, m_a), (s_b, m_b)))
    pl.when(t == n_items)(finish_previous)


def _stream_attn(q, k, vt, extra, finish, *, groups, masked, name):
    tq = Q_TILE
    nq = SEQ // tq
    n_items = BATCH * groups * nq
    qw = q.shape[1] // groups

    def rows(t):
        return (t // (groups * nq)) * nq + t % nq

    def group(t):
        return (t // nq) % groups

    def batch(t):
        return t // (groups * nq)

    nxt = lambda t: jnp.minimum(t + 1, n_items - 1)
    cur = lambda t: jnp.minimum(t, n_items - 1)
    prv = lambda t: jnp.maximum(t - 1, 0)
    in_specs = [
        pl.BlockSpec((tq, qw), lambda t: (0, 0)),
        pl.BlockSpec((tq, qw), lambda t: (rows(nxt(t)), group(nxt(t)))),
        pl.BlockSpec((SEQ, qw), lambda t: (batch(nxt(t)), group(nxt(t)))),
        pl.BlockSpec((1, LANES, SEQ), lambda t: (batch(cur(t)), group(cur(t)), 0)),
    ] + [_resident(a.shape) for a in extra]
    sbuf = pltpu.VMEM((2, SEQ, tq), F32)
    mbuf = pltpu.VMEM((2, SUBLANES, tq), F32)
    return pl.pallas_call(
        functools.partial(_stream_attn_kernel, n_items=n_items, masked=masked, finish=finish),
        grid=(n_items + 1,),
        in_specs=in_specs,
        out_specs=pl.BlockSpec((tq, LANES), lambda t: (rows(prv(t)), group(prv(t)))),
        out_shape=jax.ShapeDtypeStruct((TOKENS, groups * LANES), BF16),
        scratch_shapes=[sbuf, mbuf, sbuf, mbuf,
                        pltpu.VMEM((2, LANES, tq), F32), pltpu.VMEM((2, SUBLANES, tq), F32)],
        compiler_params=_params("arbitrary"),
        name=name,
    )(q, q, k, vt, *extra)


def _diff_finish(result, lam_ref, g_ref, *, lam_init):
    (acc1, l1), (acc2, l2) = result
    lf = lam_ref[...]
    lam = (jnp.exp(jnp.sum(lf[0:1] * lf[1:2], axis=-1, keepdims=True))
           - jnp.exp(jnp.sum(lf[2:3] * lf[3:4], axis=-1, keepdims=True)) + lam_init)
    o = (acc1 / l1 - lam * (acc2 / l2)).T
    return _rms(o, g_ref[...]) * (1.0 - lam_init)


def _diff_attn(qa, ka, va, lam_vec, subln_g, lam_init):
    return _stream_attn(qa, ka, va, (lam_vec, subln_g),
                        functools.partial(_diff_finish, lam_init=lam_init),
                        groups=A_HEADS, masked=True, name="diff_attn")


NA_QROWS = 4
NA_KROWS = NA_QROWS + NA_ROWS
NA_BLOCKS = GRID_ROWS // NA_QROWS
NA_PLACEMENTS = 3


def _na_window_start(g):
    lo, hi = 0, GRID_ROWS - NA_KROWS
    s = g * NA_QROWS - NA_ROWS // 2
    if isinstance(g, int):
        return min(max(s, lo), hi)
    return jnp.clip(s, lo, hi)


def _na_attn_kernel(q_ref, k_ref, vt_ref, bias_ref, o_ref, s_a, m_a, s_b, m_b):
    nq = NA_QROWS * GRID_W
    nk = NA_KROWS * GRID_W
    groups8 = nk // SUBLANES
    lane = lax.broadcasted_iota(jnp.int32, (nq, 2 * B_DH), 1)

    def key_start(g):
        return pl.multiple_of(_na_window_start(g) * GRID_W, NA_QROWS * GRID_W)

    def logits(g, s_buf, m_buf):
        var = jnp.where(g == 0, 0, jnp.where(g == NA_BLOCKS - 1, 2, 1))
        q = q_ref[pl.ds(pl.multiple_of(g * nq, nq), nq), :]
        zero = jnp.zeros_like(q)
        q2 = jnp.concatenate([jnp.where(lane < B_DH, q, zero), jnp.where(lane >= B_DH, q, zero)], axis=0)
        s2 = lax.dot_general(k_ref[pl.ds(key_start(g), nk), :], q2, _NT, preferred_element_type=F32)
        for j in range(2):
            s = s2[:, j * nq:(j + 1) * nq] + bias_ref[j, var]
            s_buf[j] = s
            m = jnp.max(jnp.max(s.reshape(groups8, SUBLANES, nq), axis=0), axis=0, keepdims=True)
            m_buf[j] = jnp.broadcast_to(m, (SUBLANES, nq))

    def softmax_pv(g, s_buf, m_buf):
        outs = []
        for j in range(2):
            e = jnp.exp2(s_buf[j].reshape(groups8, SUBLANES, nq) - m_buf[j][None])
            l = jnp.sum(jnp.sum(e, axis=0), axis=0, keepdims=True)
            vt = vt_ref[0, j * B_DH:(j + 1) * B_DH, pl.ds(key_start(g), nk)]
            outs.append(_dot(vt, e.reshape(nk, nq).astype(BF16)) / l)
        o_ref[pl.ds(pl.multiple_of(g * nq, nq), nq), :] = jnp.concatenate(outs, axis=0).T.astype(BF16)

    logits(0, s_a, m_a)

    def pair(i, carry):
        g = 2 * i
        logits(g + 1, s_b, m_b)
        softmax_pv(g, s_a, m_a)
        logits(jnp.minimum(g + 2, NA_BLOCKS - 1), s_a, m_a)
        softmax_pv(g + 1, s_b, m_b)
        return carry

    lax.fori_loop(0, NA_BLOCKS // 2, pair, 0)


def _na_attn(qb, kb, vt, bias):
    hw = 2 * B_DH
    first_block = A_V // hw
    spec = pl.BlockSpec((SEQ, hw), lambda b, p: (b, p))
    vspec = pl.BlockSpec((1, hw, SEQ), lambda b, p: (b, first_block + p, 0))
    bspec = pl.BlockSpec((2,) + bias.shape[1:], lambda b, p: (p, 0, 0, 0))
    return pl.pallas_call(
        _na_attn_kernel,
        grid=(BATCH, B_HEADS // 2),
        in_specs=[spec, spec, vspec, bspec],
        out_specs=spec,
        out_shape=jax.ShapeDtypeStruct((TOKENS, B_W), BF16),
        scratch_shapes=[pltpu.VMEM((2,) + bias.shape[2:], F32), pltpu.VMEM((2, SUBLANES, bias.shape[3]), F32)] * 2,
        compiler_params=_params("parallel", "parallel"),
        name="na_attn",
    )(qb, kb, vt, bias)


def _na_bias_table(rpb):
    c = np.arange(GRID_W)
    c0 = np.clip(c - NA_COLS // 2, 0, GRID_W - NA_COLS)
    col_in = (c[None, :] >= c0[:, None]) & (c[None, :] < c0[:, None] + NA_COLS)
    dc = np.clip(c[None, :] - c[:, None], -(NA_COLS - 1), NA_COLS - 1) + NA_COLS - 1
    rpb = rpb.astype(F32) * LOG2E
    n_dr = 2 * NA_ROWS - 1
    toe = jnp.zeros((B_HEADS, GRID_W, n_dr, GRID_W), F32)
    for d in range(2 * NA_COLS - 1):
        toe = jnp.where((dc.T == d)[:, None, :], rpb[:, None, ::-1, d, None], toe)
    toe = jnp.where(col_in.T[:, None, :], toe, -jnp.inf).reshape(B_HEADS, GRID_W, n_dr * GRID_W)
    masked = lambda n: [jnp.full((B_HEADS, GRID_W, n * GRID_W), -jnp.inf, F32)] if n else []
    first_key_row = lambda qr: min(max(qr - NA_ROWS // 2, 0), GRID_ROWS - NA_ROWS)
    slabs = []
    for g in (0, 1, NA_BLOCKS - 1):
        for kl in range(NA_KROWS):
            kr = _na_window_start(g) + kl
            qrs = [g * NA_QROWS + ql for ql in range(NA_QROWS)]
            ok = [ql for ql, qr in enumerate(qrs) if first_key_row(qr) <= kr < first_key_row(qr) + NA_ROWS]
            if not ok:
                slabs.append(masked(NA_QROWS)[0])
                continue
            lo, hi = ok[0], ok[-1]
            assert ok == list(range(lo, hi + 1))
            first = n_dr - 1 - (kr - qrs[lo] + NA_ROWS - 1)
            own = toe[:, :, first * GRID_W:(first + hi - lo + 1) * GRID_W]
            slabs.append(jnp.concatenate(masked(lo) + [own] + masked(NA_QROWS - 1 - hi), axis=-1))
    return jnp.stack(slabs, axis=1).reshape(
        B_HEADS, NA_PLACEMENTS, NA_KROWS * GRID_W, NA_QROWS * GRID_W)


def _odd_in_kernel(x_ref, g_ref, w_ref, gq_ref, gkv_ref, wuq_ref, wk_ref, wvt_ref,
                   cos_ref, sin_ref, xc_ref, gc_ref, q_ref, k_ref, vt_ref):
    h = _rms(x_ref[...], g_ref[...]).astype(BF16)
    cos, sin = cos_ref[...], sin_ref[...]
    lane = lax.broadcasted_iota(jnp.int32, cos.shape, 1)
    lo = lane < D_NOPE + D_ROPE // 2
    scale = (D_NOPE + D_ROPE) ** -0.5 * LOG2E
    o = 0
    xc_ref[...] = _dot(h, w_ref[:, o:o + C_WIDTH]); o += C_WIDTH
    gc_ref[...] = _dot(h, w_ref[:, o:o + C_WIDTH]); o += C_WIDTH
    cq = _dot(h, w_ref[:, o:o + D_Q_RANK]); o += D_Q_RANK
    ckv = _dot(h, w_ref[:, o:o + D_KV_RANK]); o += D_KV_RANK
    kr = _dot(h, w_ref[:, o:o + PAD_DK])
    q = _dot(_rms(cq, gq_ref[...]).astype(BF16), wuq_ref[...])
    ckvn = _rms(ckv, gkv_ref[...]).astype(BF16)
    kn = _dot(ckvn, wk_ref[...])
    kpe = _rope_lanes(kr, cos, sin, D_ROPE // 2, lo)
    for j in range(D_HEADS):
        sl = slice(j * PAD_DK, (j + 1) * PAD_DK)
        q_ref[:, sl] = (_rope_lanes(q[:, sl], cos, sin, D_ROPE // 2, lo) * scale).astype(BF16)
        k_ref[:, sl] = (kn[:, sl] + kpe).astype(BF16)
    vt_ref[0] = lax.dot_general(wvt_ref[...], ckvn, _NT, preferred_element_type=F32).astype(BF16)


def _odd_in(x, g, w_in, gq, gkv, wuq, wk, wvt, cos, sin):
    tm = TOKEN_TILE
    per_seq = SEQ // tm
    tok = lambda w: pl.BlockSpec((tm, w), lambda i: (i, 0))
    pos = pl.BlockSpec((tm, LANES), lambda i: (i % per_seq, 0))
    sds = lambda w, dt: jax.ShapeDtypeStruct((TOKENS, w), dt)
    qw = D_HEADS * PAD_DK
    vw = D_HEADS * D_VDIM
    return pl.pallas_call(
        _odd_in_kernel,
        grid=(TOKENS // tm,),
        in_specs=[tok(D_MODEL)] + [_resident(a.shape) for a in (g, w_in, gq, gkv, wuq, wk, wvt)] + [pos, pos],
        out_specs=[tok(C_WIDTH), tok(C_WIDTH), tok(qw), tok(qw),
                   pl.BlockSpec((1, vw, tm), lambda i: (i // per_seq, 0, i % per_seq))],
        out_shape=[sds(C_WIDTH, F32), sds(C_WIDTH, F32), sds(qw, BF16), sds(qw, BF16),
                   jax.ShapeDtypeStruct((BATCH, vw, SEQ), BF16)],
        compiler_params=_params("parallel"),
        name="odd_in",
    )(x, g, w_in, gq, gkv, wuq, wk, wvt, cos, sin)


def _rglru_kernel(xc_ref, gc_ref, cw_ref, cb_ref, wgate_ref, bgate_ref, lam_ref, o_ref,
                  a_scr, b_scr, h_scr):
    cw = cw_ref[...]
    x = xc_ref[...]
    row = lax.broadcasted_iota(jnp.int32, x.shape, 0)

    def shifted(d):
        rolled = pltpu.roll(x, (-d) % SEQ, 0)
        valid = (row + d >= 0) & (row + d < SEQ)
        return jnp.where(valid, rolled, 0.0)

    lp = (CONV_W - 1) // 2
    u = cb_ref[...] + sum((x if j == lp else shifted(j - lp)) * cw[j:j + 1] for j in range(CONV_W))
    gates = _dot(u.astype(BF16), wgate_ref[0]) + bgate_ref[0]
    lam = lam_ref[0]
    for d in range(2):
        o = 2 * d * LANES
        r_t = _sigmoid(gates[:, o:o + LANES])
        i_t = _sigmoid(gates[:, o + LANES:o + 2 * LANES])
        log_a = (-RG_C * r_t) * jax.nn.softplus(-lam[:, d * LANES:(d + 1) * LANES])
        a = jnp.exp(log_a)
        y = -jnp.tanh(log_a) * (a * a + 1.0)
        mult = jnp.where(y > 0.0, y * lax.rsqrt(y), 0.0)
        mult = jnp.where(row == (SEQ - 1 if d else 0), 1.0, mult)
        a_scr[d] = a
        b_scr[d] = mult * i_t * u

    n_tiles = SEQ // SUBLANES
    srow = lax.broadcasted_iota(jnp.int32, (SUBLANES, LANES), 0)

    def tile_scan(a, b, h_prev, reverse):
        for d in (1, 2, 4):
            if reverse:
                keep = srow < SUBLANES - d
                shift = SUBLANES - d
            else:
                keep = srow >= d
                shift = d
            a_s = jnp.where(keep, pltpu.roll(a, shift, 0), 1.0)
            b_s = jnp.where(keep, pltpu.roll(b, shift, 0), 0.0)
            b = a * b_s + b
            a = a * a_s
        return a * h_prev + b

    def step(i, carry):
        hf, hr = carry
        tf = pl.multiple_of(i * SUBLANES, SUBLANES)
        h = tile_scan(a_scr[0, pl.ds(tf, SUBLANES), :], b_scr[0, pl.ds(tf, SUBLANES), :], hf, False)
        h_scr[0, pl.ds(tf, SUBLANES), :] = h
        hf = jnp.broadcast_to(h[SUBLANES - 1:SUBLANES, :], h.shape)
        tr = pl.multiple_of((n_tiles - 1 - i) * SUBLANES, SUBLANES)
        h = tile_scan(a_scr[1, pl.ds(tr, SUBLANES), :], b_scr[1, pl.ds(tr, SUBLANES), :], hr, True)
        h_scr[1, pl.ds(tr, SUBLANES), :] = h
        hr = jnp.broadcast_to(h[0:1, :], h.shape)
        return hf, hr

    z = jnp.zeros((SUBLANES, LANES), F32)
    lax.fori_loop(0, n_tiles, step, (z, z), unroll=SCAN_UNROLL)
    o_ref[...] = (jax.nn.gelu(gc_ref[...]) * (h_scr[0] + h_scr[1])).astype(BF16)


def _rglru(xc, gc, conv_w, conv_b, wgate, bgate, lam):
    ng = C_WIDTH // LANES
    seq = pl.BlockSpec((SEQ, LANES), lambda b, g: (b, g))
    grp = lambda a: pl.BlockSpec((1,) + a.shape[1:], lambda b, g: (g, 0, 0))
    return pl.pallas_call(
        _rglru_kernel,
        grid=(BATCH, ng),
        in_specs=[seq, seq,
                  pl.BlockSpec((CONV_W, LANES), lambda b, g: (0, g)),
                  pl.BlockSpec((1, LANES), lambda b, g: (0, g)),
                  grp(wgate), grp(bgate), grp(lam)],
        out_specs=seq,
        out_shape=jax.ShapeDtypeStruct((TOKENS, C_WIDTH), BF16),
        scratch_shapes=[pltpu.VMEM((2, SEQ, LANES), F32)] * 3,
        compiler_params=_params("parallel", "parallel"),
        name="rglru",
    )(xc, gc, conv_w, conv_b, wgate, bgate, lam)


def _rglru_gate_params(rg_wa, rg_ba, rg_wx, rg_bx, rg_lam):
    ng = C_WIDTH // LANES
    per = LANES // C_BW

    def dense(w):
        w = w.reshape(ng, per, C_BW, C_BW)
        eye = jnp.eye(per, dtype=w.dtype)
        return jnp.einsum('gpcd,pq->gpcqd', w, eye).reshape(ng, LANES, LANES)

    wgate = jnp.concatenate([dense(rg_wa[0]), dense(rg_wx[0]), dense(rg_wa[1]), dense(rg_wx[1])], axis=-1)
    grp = lambda v: v.reshape(ng, 1, LANES)
    bgate = jnp.concatenate([grp(rg_ba[0]), grp(rg_bx[0]), grp(rg_ba[1]), grp(rg_bx[1])], axis=-1)
    lam = jnp.concatenate([grp(rg_lam[0]), grp(rg_lam[1])], axis=-1)
    return wgate.astype(BF16), bgate.astype(F32), lam.astype(F32)


def _mla_finish(result):
    (acc1, l1), (acc2, l2) = result
    return jnp.concatenate([acc1 / l1, acc2 / l2], axis=0).T


def _mla_attn(q, k, v):
    return _stream_attn(q, k, v, (), _mla_finish, groups=D_HEADS // 2, masked=False, name="mla_attn")


def _rope_angles(dim):
    inv = 1.0 / (ROPE_THETA ** (jnp.arange(0, dim, 2, dtype=F32) / dim))
    ang = jnp.arange(SEQ, dtype=F32)[:, None] * inv[None, :]
    return jnp.cos(ang), jnp.sin(ang)


def _even_rope_tables():
    cos, sin = _rope_angles(A_DH)
    reps = LANES // A_DH
    return (jnp.tile(jnp.concatenate([cos, cos], -1), (1, reps)),
            jnp.tile(jnp.concatenate([-sin, sin], -1), (1, reps)))


def _odd_rope_tables():
    cos, sin = _rope_angles(D_ROPE)
    ones = jnp.ones((SEQ, D_NOPE), F32)
    zn = jnp.zeros((SEQ, D_NOPE), F32)
    zp = jnp.zeros((SEQ, PAD_DK - D_NOPE - D_ROPE), F32)
    return (jnp.concatenate([ones, cos, cos, zp], -1),
            jnp.concatenate([zn, -sin, sin, zp], -1))


def _odd_weights(w_in, wuq, wukv):
    base = 2 * C_WIDTH + D_Q_RANK + D_KV_RANK
    zl = jnp.zeros((D_MODEL, D_NOPE), w_in.dtype)
    zr = jnp.zeros((D_MODEL, PAD_DK - D_NOPE - D_ROPE), w_in.dtype)
    w_in_p = jnp.concatenate([w_in[:, :base], zl, w_in[:, base:], zr], axis=-1)
    dqk = D_NOPE + D_ROPE
    wuq_p = jnp.pad(wuq.reshape(D_Q_RANK, D_HEADS, dqk), ((0, 0), (0, 0), (0, PAD_DK - dqk)))
    wukv_h = wukv.reshape(D_KV_RANK, D_HEADS, D_NOPE + D_VDIM)
    wk_p = jnp.pad(wukv_h[:, :, :D_NOPE], ((0, 0), (0, 0), (0, PAD_DK - D_NOPE)))
    wv = wukv_h[:, :, D_NOPE:]
    return (w_in_p.astype(BF16), wuq_p.reshape(D_Q_RANK, -1).astype(BF16),
            wk_p.reshape(D_KV_RANK, -1).astype(BF16), wv.reshape(D_KV_RANK, -1).T.astype(BF16))


def kernel(x, ffn1_norm, ffn1_wg, ffn1_wu, ffn1_wd, mix_norm, ffn2_norm, ffn2_wg, ffn2_wu, ffn2_wd, final_norm, ev_w_in, ev_w_out, diff_lam, diff_subln, na_rpb, od_w_in, od_w_out, conv_w, conv_b, rg_wa, rg_ba, rg_wx, rg_bx, rg_lam, mla_gq, mla_gkv, mla_wuq, mla_wukv):
    row = lambda v: v.reshape(1, -1).astype(F32)
    bf = lambda w: w.astype(BF16)
    xt = x.reshape(TOKENS, D_MODEL)

    ffn1 = (bf(ffn1_wg), bf(ffn1_wu), bf(ffn1_wd))
    ffn2 = (bf(ffn2_wg), bf(ffn2_wu), bf(ffn2_wd))

    xt = _ffn(xt, row(ffn1_norm[0]), *ffn1, 0)
    cos_a, sin_a = _even_rope_tables()
    wv_cols = jnp.concatenate([ev_w_in[0][:, 2 * A_QK:2 * A_QK + A_V], ev_w_in[0][:, EVEN_IN - B_W:]], axis=1)
    qa, ka, vt, qb, kb = _even_in(xt, row(mix_norm[0]), bf(ev_w_in[0]), bf(wv_cols.T), cos_a, sin_a)
    lam_init0 = 0.8 - 0.6 * math.exp(-0.3 * 0)
    oa = _diff_attn(qa, ka, vt, diff_lam[0].astype(F32), row(diff_subln[0]), lam_init0)
    ob = _na_attn(qb, kb, vt, _na_bias_table(na_rpb[0]))
    xt = _ffn(xt, row(ffn2_norm[0]), *ffn2, 0, mix=(oa, ob, bf(ev_w_out[0])))

    xt = _ffn(xt, row(ffn1_norm[1]), *ffn1, 1)
    cos_d, sin_d = _odd_rope_tables()
    w_in_p, wuq_p, wk_p, wv_p = _odd_weights(od_w_in[0], mla_wuq[0], mla_wukv[0])
    xc, gc, q, k, v = _odd_in(xt, row(mix_norm[1]), w_in_p, row(mla_gq[0]), row(mla_gkv[0]),
                              wuq_p, wk_p, wv_p, cos_d, sin_d)
    wgate, bgate, lam = _rglru_gate_params(rg_wa[0], rg_ba[0], rg_wx[0], rg_bx[0], rg_lam[0])
    oc = _rglru(xc, gc, conv_w[0].astype(F32), row(conv_b[0]), wgate, bgate, lam)
    od = _mla_attn(q, k, v)
    xt = _ffn(xt, row(ffn2_norm[1]), *ffn2, 1, mix=(oc, od, bf(od_w_out[0])), final_g=row(final_norm))
    return xt.reshape(BATCH, SEQ, D_MODEL)
```

```python
import functools
import itertools
import math

import jax
import jax.numpy as jnp
import numpy as np
from jax import lax
from jax.experimental import pallas as pl
from jax.experimental.pallas import tpu as pltpu

F32 = jnp.float32
BF16 = jnp.bfloat16

D_MODEL = 1024
BATCH = 4
SEQ = 4096
DEPTH = 2
TOKENS = BATCH * SEQ
RMS_EPS = 1e-6
ROPE_THETA = 10000.0
GRID_W = 64
GRID_ROWS = SEQ // GRID_W
D_FF = 2816

A_HEADS = 4
A_DH = 64
A_QK = A_HEADS * 2 * A_DH
A_V = A_HEADS * 2 * A_DH
B_HEADS = 8
B_DH = 64
B_W = B_HEADS * B_DH
NA_ROWS = 8
NA_COLS = 16
C_WIDTH = 512
C_BLOCKS = 8
C_BW = C_WIDTH // C_BLOCKS
CONV_W = 4
RG_C = 8.0
D_HEADS = 8
D_NOPE = 64
D_ROPE = 32
D_VDIM = 64
D_Q_RANK = 256
D_KV_RANK = 128
EVEN_IN = 2 * A_QK + A_V + 3 * B_W
ODD_IN = 2 * C_WIDTH + D_Q_RANK + D_KV_RANK + D_ROPE

LANES = 128
SUBLANES = 8
VMEM_LIMIT_BYTES = 58 * 1024 * 1024

TOKEN_TILE = 512
FFN_TILE = 1024
FF_CHUNK = 256
Q_TILE = 256
PV_TILE = 256
KEY_CHUNK = 1024
LOGIT_CHUNK = 1024
PAD_DK = 128
SCAN_UNROLL = 8

_NT = (((1,), (1,)), ((), ()))

LOG2E = math.log2(math.e)


def _params(*sem):
    return pltpu.CompilerParams(dimension_semantics=sem, vmem_limit_bytes=VMEM_LIMIT_BYTES)


def _resident(shape):
    nd = len(shape)
    return pl.BlockSpec(shape, lambda *_: (0,) * nd, pipeline_mode=pl.Buffered(1))


def _rms(x, g):
    return x * lax.rsqrt(jnp.mean(x * x, axis=-1, keepdims=True) + RMS_EPS) * g


def _dot(a, b):
    return jnp.dot(a, b, preferred_element_type=F32)


def _sigmoid(x):
    return 0.5 * (jnp.tanh(0.5 * x) + 1.0)


def _rope_lanes(x, cos, sin_signed, half, lo_mask):
    n = x.shape[-1]
    partner = jnp.where(lo_mask, pltpu.roll(x, n - half, 1), pltpu.roll(x, half, 1))
    return x * cos + partner * sin_signed


def _ffn_kernel(*refs, layer, has_mix, has_final):
    it = iter(refs)
    x_ref = next(it)
    if has_mix:
        o1_ref, o2_ref, wo_ref = next(it), next(it), next(it)
    g_ref, wg_hbm, wu_hbm, wd_hbm = next(it), next(it), next(it), next(it)
    gf_ref = next(it) if has_final else None
    y_ref = next(it)
    acc_ref, wg_ref, wu_ref, wd_ref, stage_in, stage_out, sem = (next(it) for _ in range(7))
    n_chunks = D_FF // FF_CHUNK

    def chunk_copies(c):
        slot = c % 2
        sl = slice(c * FF_CHUNK, (c + 1) * FF_CHUNK)
        return (pltpu.make_async_copy(wg_hbm.at[layer, :, sl], stage_in.at[slot, 0], sem.at[slot, 0]),
                pltpu.make_async_copy(wu_hbm.at[layer, :, sl], stage_in.at[slot, 1], sem.at[slot, 1]),
                pltpu.make_async_copy(wd_hbm.at[layer, sl, :], stage_out.at[slot], sem.at[slot, 2]))

    def body(load_weights):
        if load_weights:
            for cp in chunk_copies(0):
                cp.start()
        x = x_ref[...]
        if has_mix:
            half = wo_ref.shape[0] // 2
            x = x + _dot(o1_ref[...], wo_ref[:half, :]) + _dot(o2_ref[...], wo_ref[half:, :])
        n = _rms(x, g_ref[...]).astype(BF16)
        for c in range(n_chunks):
            sl = slice(c * FF_CHUNK, (c + 1) * FF_CHUNK)
            if load_weights:
                if c + 1 < n_chunks:
                    for cp in chunk_copies(c + 1):
                        cp.start()
                for cp in chunk_copies(c):
                    cp.wait()
                wg_ref[:, sl] = stage_in[c % 2, 0].astype(BF16)
                wu_ref[:, sl] = stage_in[c % 2, 1].astype(BF16)
                wd_ref[sl, :] = stage_out[c % 2].astype(BF16)
            gate = _dot(n, wg_ref[:, sl])
            up = _dot(n, wu_ref[:, sl])
            h = (gate * jax.nn.sigmoid(gate) * up).astype(BF16)
            d = _dot(h, wd_ref[sl, :])
            if c == 0:
                acc_ref[...] = d
            else:
                acc_ref[...] += d
        y = x + 0.5 * acc_ref[...]
        if has_final:
            y = _rms(y, gf_ref[...])
        y_ref[...] = y

    first = pl.program_id(0) == 0
    pl.when(first)(lambda: body(True))
    pl.when(jnp.logical_not(first))(lambda: body(False))


def _ffn(x, g, wg, wu, wd, layer, mix=None, final_g=None):
    tm = FFN_TILE
    tok = lambda w: pl.BlockSpec((tm, w), lambda i: (i, 0))
    hbm = pl.BlockSpec(memory_space=pl.ANY)
    args, specs = [x], [tok(D_MODEL)]
    if mix is not None:
        o1, o2, wo = mix
        args += [o1, o2, wo]
        specs += [tok(o1.shape[1]), tok(o2.shape[1]), _resident(wo.shape)]
    args += [g, wg, wu, wd]
    specs += [_resident(g.shape), hbm, hbm, hbm]
    if final_g is not None:
        args.append(final_g)
        specs.append(_resident(final_g.shape))
    return pl.pallas_call(
        functools.partial(_ffn_kernel, layer=layer, has_mix=mix is not None, has_final=final_g is not None),
        grid=(TOKENS // tm,),
        in_specs=specs,
        out_specs=tok(D_MODEL),
        out_shape=jax.ShapeDtypeStruct((TOKENS, D_MODEL), F32),
        scratch_shapes=[pltpu.VMEM((tm, D_MODEL), F32),
                        pltpu.VMEM((D_MODEL, D_FF), BF16), pltpu.VMEM((D_MODEL, D_FF), BF16),
                        pltpu.VMEM((D_FF, D_MODEL), BF16),
                        pltpu.VMEM((2, 2, D_MODEL, FF_CHUNK), F32), pltpu.VMEM((2, FF_CHUNK, D_MODEL), F32),
                        pltpu.SemaphoreType.DMA((2, 3))],
        compiler_params=_params("arbitrary"),
        name="ffn_mix" if mix is not None else "ffn",
    )(*args)


def _even_in_kernel(x_ref, g_ref, w_ref, wvt_ref, cos_ref, sin_ref,
                    qa_ref, ka_ref, vt_ref, qb_ref, kb_ref):
    h = _rms(x_ref[...], g_ref[...]).astype(BF16)
    cos, sin = cos_ref[...], sin_ref[...]
    lane = lax.broadcasted_iota(jnp.int32, cos.shape, 1)
    lo = (lane % A_DH) < (A_DH // 2)
    scale = A_DH ** -0.5 * LOG2E

    def proj(i):
        return _dot(h, w_ref[:, i * A_QK:(i + 1) * A_QK])

    qa, ka = proj(0), proj(1)
    for j in range(A_QK // LANES):
        sl = slice(j * LANES, (j + 1) * LANES)
        qa_ref[:, sl] = (_rope_lanes(qa[:, sl], cos, sin, A_DH // 2, lo) * scale).astype(BF16)
        ka_ref[:, sl] = _rope_lanes(ka[:, sl], cos, sin, A_DH // 2, lo).astype(BF16)
    vt_ref[0] = lax.dot_general(wvt_ref[...], h, _NT, preferred_element_type=F32).astype(BF16)
    qb_ref[...] = (proj(3) * (B_DH ** -0.5 * LOG2E)).astype(BF16)
    kb_ref[...] = proj(4).astype(BF16)


def _even_in(x, g, w_in, wvt, cos, sin):
    tm = TOKEN_TILE
    per_seq = SEQ // tm
    tok = lambda w: pl.BlockSpec((tm, w), lambda i: (i, 0))
    pos = pl.BlockSpec((tm, LANES), lambda i: (i % per_seq, 0))
    vw = wvt.shape[0]
    tr = pl.BlockSpec((1, vw, tm), lambda i: (i // per_seq, 0, i % per_seq))
    out = jax.ShapeDtypeStruct((TOKENS, A_QK), BF16)
    out_t = jax.ShapeDtypeStruct((BATCH, vw, SEQ), BF16)
    return pl.pallas_call(
        _even_in_kernel,
        grid=(TOKENS // tm,),
        in_specs=[tok(D_MODEL), _resident(g.shape), _resident(w_in.shape), _resident(wvt.shape), pos, pos],
        out_specs=[tok(A_QK), tok(A_QK), tr, tok(B_W), tok(B_W)],
        out_shape=[out, out, out_t, out, out],
        compiler_params=_params("parallel"),
        name="even_in",
    )(x, g, w_in, wvt, cos, sin)


def _logit_chunks(q_ref, k_ref, s_buf, m_buf):
    q = q_ref[...]
    tq, qw = q.shape
    lane = lax.broadcasted_iota(jnp.int32, q.shape, 1)
    zero = jnp.zeros_like(q)
    q2 = jnp.concatenate([jnp.where(lane < qw // 2, q, zero), jnp.where(lane >= qw // 2, q, zero)], axis=0)
    groups8 = LOGIT_CHUNK // SUBLANES
    mrun = [None, None]
    for c in range(SEQ // LOGIT_CHUNK):
        keys = slice(c * LOGIT_CHUNK, (c + 1) * LOGIT_CHUNK)
        s2 = lax.dot_general(k_ref[keys, :], q2, _NT, preferred_element_type=F32)
        for j in range(2):
            s = s2[:, j * tq:(j + 1) * tq]
            s_buf[j, keys, :] = s
            part = jnp.max(s.reshape(groups8, SUBLANES, tq), axis=0)
            mrun[j] = part if mrun[j] is None else jnp.maximum(mrun[j], part)
        yield
    for j in range(2):
        m_buf[j] = jnp.broadcast_to(jnp.max(mrun[j], axis=0, keepdims=True), (SUBLANES, tq))


def _softmax_pv_chunks(s_buf, m_buf, vt_ref, vrows, acc_scr, l_scr):
    tq = s_buf.shape[-1]
    groups8 = KEY_CHUNK // SUBLANES
    m = [m_buf[j][None] for j in range(2)]
    acc = [None, None]
    lrun = [None, None]
    for c in range(SEQ // KEY_CHUNK):
        keys = slice(c * KEY_CHUNK, (c + 1) * KEY_CHUNK)
        for j in range(2):
            e = jnp.exp2(s_buf[j, keys, :].reshape(groups8, SUBLANES, tq) - m[j])
            part = jnp.sum(e, axis=0)
            lrun[j] = part if lrun[j] is None else lrun[j] + part
            pv = _dot(vt_ref[0, vrows[j], keys], e.reshape(KEY_CHUNK, tq).astype(BF16))
            acc[j] = pv if acc[j] is None else acc[j] + pv
            yield
    for j in range(2):
        acc_scr[j, :acc[j].shape[0], :] = acc[j]
        l_scr[j] = lrun[j]


def _stream_attn_kernel(*refs, n_items, masked, finish):
    qf_ref, qn_ref, kn_ref, vt_ref = refs[:4]
    extra = refs[4:-7]
    o_ref, s_a, m_a, s_b, m_b, acc_scr, l_scr = refs[-7:]
    t = pl.program_id(0)
    dv = vt_ref.shape[1]
    vrows = [slice(None)] * 2 if masked else [slice(0, dv // 2), slice(dv // 2, dv)]
    rows = dv if masked else dv // 2

    @pl.when(t == 0)
    def _():
        for _ in _logit_chunks(qf_ref, kn_ref, s_a, m_a):
            pass
        acc_scr[...] = jnp.zeros_like(acc_scr)
        l_scr[...] = jnp.ones_like(l_scr)

    def finish_previous():
        result = [(acc_scr[j, :rows, :], jnp.sum(l_scr[j], axis=0, keepdims=True)) for j in range(2)]
        o_ref[...] = finish(result, *extra).astype(o_ref.dtype)

    def step(nxt, cur):
        finish_previous()
        logits = _logit_chunks(qn_ref, kn_ref, nxt[0], nxt[1])
        softmax = _softmax_pv_chunks(cur[0], cur[1], vt_ref, vrows, acc_scr, l_scr)
        ratio = (SEQ // KEY_CHUNK) * 2 // (SEQ // LOGIT_CHUNK)
        for i, _ in enumerate(softmax):
            if i % ratio == 0:
                next(logits, None)
        for _ in logits:
            pass

    live = t < n_items
    pl.when(live & ((t & 1) == 0))(lambda: step((s_b, m_b), (s_a, m_a)))
    pl.when(live & ((t & 1) == 1))(lambda: step((s_a, m_a), (s_b, m_b)))
    pl.when(t == n_items)(finish_previous)


def _stream_attn(q, k, vt, extra, finish, *, groups, masked, name):
    tq = Q_TILE
    nq = SEQ // tq
    n_items = BATCH * groups * nq
    qw = q.shape[1] // groups

    def rows(t):
        return (t // (groups * nq)) * nq + t % nq

    def group(t):
        return (t // nq) % groups

    def batch(t):
        return t // (groups * nq)

    nxt = lambda t: jnp.minimum(t + 1, n_items - 1)
    cur = lambda t: jnp.minimum(t, n_items - 1)
    prv = lambda t: jnp.maximum(t - 1, 0)
    in_specs = [
        pl.BlockSpec((tq, qw), lambda t: (0, 0)),
        pl.BlockSpec((tq, qw), lambda t: (rows(nxt(t)), group(nxt(t)))),
        pl.BlockSpec((SEQ, qw), lambda t: (batch(nxt(t)), group(nxt(t)))),
        pl.BlockSpec((1, LANES, SEQ), lambda t: (batch(cur(t)), group(cur(t)), 0)),
    ] + [_resident(a.shape) for a in extra]
    sbuf = pltpu.VMEM((2, SEQ, tq), F32)
    mbuf = pltpu.VMEM((2, SUBLANES, tq), F32)
    return pl.pallas_call(
        functools.partial(_stream_attn_kernel, n_items=n_items, masked=masked, finish=finish),
        grid=(n_items + 1,),
        in_specs=in_specs,
        out_specs=pl.BlockSpec((tq, LANES), lambda t: (rows(prv(t)), group(prv(t)))),
        out_shape=jax.ShapeDtypeStruct((TOKENS, groups * LANES), BF16),
        scratch_shapes=[sbuf, mbuf, sbuf, mbuf,
                        pltpu.VMEM((2, LANES, tq), F32), pltpu.VMEM((2, SUBLANES, tq), F32)],
        compiler_params=_params("arbitrary"),
        name=name,
    )(q, q, k, vt, *extra)


def _diff_finish(result, lam_ref, g_ref, *, lam_init):
    (acc1, l1), (acc2, l2) = result
    lf = lam_ref[...]
    lam = (jnp.exp(jnp.sum(lf[0:1] * lf[1:2], axis=-1, keepdims=True))
           - jnp.exp(jnp.sum(lf[2:3] * lf[3:4], axis=-1, keepdims=True)) + lam_init)
    o = (acc1 / l1 - lam * (acc2 / l2)).T
    return _rms(o, g_ref[...]) * (1.0 - lam_init)


def _diff_attn(qa, ka, va, lam_vec, subln_g, lam_init):
    return _stream_attn(qa, ka, va, (lam_vec, subln_g),
                        functools.partial(_diff_finish, lam_init=lam_init),
                        groups=A_HEADS, masked=True, name="diff_attn")


NA_QROWS = 4
NA_KROWS = NA_QROWS + NA_ROWS
NA_BLOCKS = GRID_ROWS // NA_QROWS
NA_PLACEMENTS = 3


def _na_window_start(g):
    lo, hi = 0, GRID_ROWS - NA_KROWS
    s = g * NA_QROWS - NA_ROWS // 2
    if isinstance(g, int):
        return min(max(s, lo), hi)
    return jnp.clip(s, lo, hi)


def _na_attn_kernel(q_ref, k_ref, vt_ref, bias_ref, o_ref, s_a, m_a, s_b, m_b):
    nq = NA_QROWS * GRID_W
    nk = NA_KROWS * GRID_W
    groups8 = nk // SUBLANES
    lane = lax.broadcasted_iota(jnp.int32, (nq, 2 * B_DH), 1)

    def key_start(g):
        return pl.multiple_of(_na_window_start(g) * GRID_W, NA_QROWS * GRID_W)

    def logits(g, s_buf, m_buf):
        var = jnp.where(g == 0, 0, jnp.where(g == NA_BLOCKS - 1, 2, 1))
        q = q_ref[pl.ds(pl.multiple_of(g * nq, nq), nq), :]
        zero = jnp.zeros_like(q)
        q2 = jnp.concatenate([jnp.where(lane < B_DH, q, zero), jnp.where(lane >= B_DH, q, zero)], axis=0)
        s2 = lax.dot_general(k_ref[pl.ds(key_start(g), nk), :], q2, _NT, preferred_element_type=F32)
        for j in range(2):
            s = s2[:, j * nq:(j + 1) * nq] + bias_ref[j, var]
            s_buf[j] = s
            m = jnp.max(jnp.max(s.reshape(groups8, SUBLANES, nq), axis=0), axis=0, keepdims=True)
            m_buf[j] = jnp.broadcast_to(m, (SUBLANES, nq))

    def softmax_pv(g, s_buf, m_buf):
        outs = []
        for j in range(2):
            e = jnp.exp2(s_buf[j].reshape(groups8, SUBLANES, nq) - m_buf[j][None])
            l = jnp.sum(jnp.sum(e, axis=0), axis=0, keepdims=True)
            vt = vt_ref[0, j * B_DH:(j + 1) * B_DH, pl.ds(key_start(g), nk)]
            outs.append(_dot(vt, e.reshape(nk, nq).astype(BF16)) / l)
        o_ref[pl.ds(pl.multiple_of(g * nq, nq), nq), :] = jnp.concatenate(outs, axis=0).T.astype(BF16)

    logits(0, s_a, m_a)

    def pair(i, carry):
        g = 2 * i
        logits(g + 1, s_b, m_b)
        softmax_pv(g, s_a, m_a)
        logits(jnp.minimum(g + 2, NA_BLOCKS - 1), s_a, m_a)
        softmax_pv(g + 1, s_b, m_b)
        return carry

    lax.fori_loop(0, NA_BLOCKS // 2, pair, 0)


def _na_attn(qb, kb, vt, bias):
    hw = 2 * B_DH
    first_block = A_V // hw
    spec = pl.BlockSpec((SEQ, hw), lambda p, b: (b, p))
    vspec = pl.BlockSpec((1, hw, SEQ), lambda p, b: (b, first_block + p, 0))
    bspec = pl.BlockSpec((2,) + bias.shape[1:], lambda p, b: (p, 0, 0, 0))
    return pl.pallas_call(
        _na_attn_kernel,
        grid=(B_HEADS // 2, BATCH),
        in_specs=[spec, spec, vspec, bspec],
        out_specs=spec,
        out_shape=jax.ShapeDtypeStruct((TOKENS, B_W), BF16),
        scratch_shapes=[pltpu.VMEM((2,) + bias.shape[2:], F32), pltpu.VMEM((2, SUBLANES, bias.shape[3]), F32)] * 2,
        compiler_params=_params("parallel", "parallel"),
        name="na_attn",
    )(qb, kb, vt, bias)


def _na_bias_table(rpb):
    c = np.arange(GRID_W)
    c0 = np.clip(c - NA_COLS // 2, 0, GRID_W - NA_COLS)
    col_in = (c[None, :] >= c0[:, None]) & (c[None, :] < c0[:, None] + NA_COLS)
    dc = np.clip(c[None, :] - c[:, None], -(NA_COLS - 1), NA_COLS - 1) + NA_COLS - 1
    rpb = rpb.astype(F32) * LOG2E
    n_dr = 2 * NA_ROWS - 1
    toe = jnp.zeros((B_HEADS, GRID_W, n_dr, GRID_W), F32)
    for d in range(2 * NA_COLS - 1):
        toe = jnp.where((dc.T == d)[:, None, :], rpb[:, None, ::-1, d, None], toe)
    toe = jnp.where(col_in.T[:, None, :], toe, -jnp.inf).reshape(B_HEADS, GRID_W, n_dr * GRID_W)
    masked = lambda n: [jnp.full((B_HEADS, GRID_W, n * GRID_W), -jnp.inf, F32)] if n else []
    first_key_row = lambda qr: min(max(qr - NA_ROWS // 2, 0), GRID_ROWS - NA_ROWS)
    slabs = []
    for g in (0, 1, NA_BLOCKS - 1):
        for kl in range(NA_KROWS):
            kr = _na_window_start(g) + kl
            qrs = [g * NA_QROWS + ql for ql in range(NA_QROWS)]
            ok = [ql for ql, qr in enumerate(qrs) if first_key_row(qr) <= kr < first_key_row(qr) + NA_ROWS]
            if not ok:
                slabs.append(masked(NA_QROWS)[0])
                continue
            lo, hi = ok[0], ok[-1]
            assert ok == list(range(lo, hi + 1))
            first = n_dr - 1 - (kr - qrs[lo] + NA_ROWS - 1)
            own = toe[:, :, first * GRID_W:(first + hi - lo + 1) * GRID_W]
            slabs.append(jnp.concatenate(masked(lo) + [own] + masked(NA_QROWS - 1 - hi), axis=-1))
    return jnp.stack(slabs, axis=1).reshape(
        B_HEADS, NA_PLACEMENTS, NA_KROWS * GRID_W, NA_QROWS * GRID_W)


def _odd_in_kernel(x_ref, g_ref, w_ref, gq_ref, gkv_ref, wuq_ref, wk_ref, wvt_ref,
                   cos_ref, sin_ref, xc_ref, gc_ref, q_ref, k_ref, vt_ref):
    h = _rms(x_ref[...], g_ref[...]).astype(BF16)
    cos, sin = cos_ref[...], sin_ref[...]
    lane = lax.broadcasted_iota(jnp.int32, cos.shape, 1)
    lo = lane < D_NOPE + D_ROPE // 2
    scale = (D_NOPE + D_ROPE) ** -0.5 * LOG2E
    o = 0
    xc_ref[...] = _dot(h, w_ref[:, o:o + C_WIDTH]); o += C_WIDTH
    gc_ref[...] = _dot(h, w_ref[:, o:o + C_WIDTH]); o += C_WIDTH
    cq = _dot(h, w_ref[:, o:o + D_Q_RANK]); o += D_Q_RANK
    ckv = _dot(h, w_ref[:, o:o + D_KV_RANK]); o += D_KV_RANK
    kr = _dot(h, w_ref[:, o:o + PAD_DK])
    q = _dot(_rms(cq, gq_ref[...]).astype(BF16), wuq_ref[...])
    ckvn = _rms(ckv, gkv_ref[...]).astype(BF16)
    kn = _dot(ckvn, wk_ref[...])
    kpe = _rope_lanes(kr, cos, sin, D_ROPE // 2, lo)
    for j in range(D_HEADS):
        sl = slice(j * PAD_DK, (j + 1) * PAD_DK)
        q_ref[:, sl] = (_rope_lanes(q[:, sl], cos, sin, D_ROPE // 2, lo) * scale).astype(BF16)
        k_ref[:, sl] = (kn[:, sl] + kpe).astype(BF16)
    vt_ref[0] = lax.dot_general(wvt_ref[...], ckvn, _NT, preferred_element_type=F32).astype(BF16)


def _odd_in(x, g, w_in, gq, gkv, wuq, wk, wvt, cos, sin):
    tm = TOKEN_TILE
    per_seq = SEQ // tm
    tok = lambda w: pl.BlockSpec((tm, w), lambda i: (i, 0))
    pos = pl.BlockSpec((tm, LANES), lambda i: (i % per_seq, 0))
    sds = lambda w, dt: jax.ShapeDtypeStruct((TOKENS, w), dt)
    qw = D_HEADS * PAD_DK
    vw = D_HEADS * D_VDIM
    return pl.pallas_call(
        _odd_in_kernel,
        grid=(TOKENS // tm,),
        in_specs=[tok(D_MODEL)] + [_resident(a.shape) for a in (g, w_in, gq, gkv, wuq, wk, wvt)] + [pos, pos],
        out_specs=[tok(C_WIDTH), tok(C_WIDTH), tok(qw), tok(qw),
                   pl.BlockSpec((1, vw, tm), lambda i: (i // per_seq, 0, i % per_seq))],
        out_shape=[sds(C_WIDTH, F32), sds(C_WIDTH, F32), sds(qw, BF16), sds(qw, BF16),
                   jax.ShapeDtypeStruct((BATCH, vw, SEQ), BF16)],
        compiler_params=_params("parallel"),
        name="odd_in",
    )(x, g, w_in, gq, gkv, wuq, wk, wvt, cos, sin)


def _rglru_kernel(xc_ref, gc_ref, cw_ref, cb_ref, wgate_ref, bgate_ref, lam_ref, o_ref,
                  a_scr, b_scr, h_scr):
    cw = cw_ref[...]
    x = xc_ref[...]
    row = lax.broadcasted_iota(jnp.int32, x.shape, 0)

    def shifted(d):
        rolled = pltpu.roll(x, (-d) % SEQ, 0)
        valid = (row + d >= 0) & (row + d < SEQ)
        return jnp.where(valid, rolled, 0.0)

    lp = (CONV_W - 1) // 2
    u = cb_ref[...] + sum((x if j == lp else shifted(j - lp)) * cw[j:j + 1] for j in range(CONV_W))
    gates = _dot(u.astype(BF16), wgate_ref[0]) + bgate_ref[0]
    lam = lam_ref[0]
    for d in range(2):
        o = 2 * d * LANES
        r_t = _sigmoid(gates[:, o:o + LANES])
        i_t = _sigmoid(gates[:, o + LANES:o + 2 * LANES])
        log_a = (-RG_C * r_t) * jax.nn.softplus(-lam[:, d * LANES:(d + 1) * LANES])
        a = jnp.exp(log_a)
        y = -jnp.tanh(log_a) * (a * a + 1.0)
        mult = jnp.where(y > 0.0, y * lax.rsqrt(y), 0.0)
        mult = jnp.where(row == (SEQ - 1 if d else 0), 1.0, mult)
        a_scr[d] = a
        b_scr[d] = mult * i_t * u

    n_tiles = SEQ // SUBLANES
    srow = lax.broadcasted_iota(jnp.int32, (SUBLANES, LANES), 0)

    def tile_scan(a, b, h_prev, reverse):
        for d in (1, 2, 4):
            if reverse:
                keep = srow < SUBLANES - d
                shift = SUBLANES - d
            else:
                keep = srow >= d
                shift = d
            a_s = jnp.where(keep, pltpu.roll(a, shift, 0), 1.0)
            b_s = jnp.where(keep, pltpu.roll(b, shift, 0), 0.0)
            b = a * b_s + b
            a = a * a_s
        return a * h_prev + b

    def step(i, carry):
        hf, hr = carry
        tf = pl.multiple_of(i * SUBLANES, SUBLANES)
        h = tile_scan(a_scr[0, pl.ds(tf, SUBLANES), :], b_scr[0, pl.ds(tf, SUBLANES), :], hf, False)
        h_scr[0, pl.ds(tf, SUBLANES), :] = h
        hf = jnp.broadcast_to(h[SUBLANES - 1:SUBLANES, :], h.shape)
        tr = pl.multiple_of((n_tiles - 1 - i) * SUBLANES, SUBLANES)
        h = tile_scan(a_scr[1, pl.ds(tr, SUBLANES), :], b_scr[1, pl.ds(tr, SUBLANES), :], hr, True)
        h_scr[1, pl.ds(tr, SUBLANES), :] = h
        hr = jnp.broadcast_to(h[0:1, :], h.shape)
        return hf, hr

    z = jnp.zeros((SUBLANES, LANES), F32)
    lax.fori_loop(0, n_tiles, step, (z, z), unroll=SCAN_UNROLL)
    o_ref[...] = (jax.nn.gelu(gc_ref[...]) * (h_scr[0] + h_scr[1])).astype(BF16)


def _rglru(xc, gc, conv_w, conv_b, wgate, bgate, lam):
    ng = C_WIDTH // LANES
    seq = pl.BlockSpec((SEQ, LANES), lambda b, g: (b, g))
    grp = lambda a: pl.BlockSpec((1,) + a.shape[1:], lambda b, g: (g, 0, 0))
    return pl.pallas_call(
        _rglru_kernel,
        grid=(BATCH, ng),
        in_specs=[seq, seq,
                  pl.BlockSpec((CONV_W, LANES), lambda b, g: (0, g)),
                  pl.BlockSpec((1, LANES), lambda b, g: (0, g)),
                  grp(wgate), grp(bgate), grp(lam)],
        out_specs=seq,
        out_shape=jax.ShapeDtypeStruct((TOKENS, C_WIDTH), BF16),
        scratch_shapes=[pltpu.VMEM((2, SEQ, LANES), F32)] * 3,
        compiler_params=_params("parallel", "parallel"),
        name="rglru",
    )(xc, gc, conv_w, conv_b, wgate, bgate, lam)


def _rglru_gate_params(rg_wa, rg_ba, rg_wx, rg_bx, rg_lam):
    ng = C_WIDTH // LANES
    per = LANES // C_BW

    def dense(w):
        w = w.reshape(ng, per, C_BW, C_BW)
        eye = jnp.eye(per, dtype=w.dtype)
        return jnp.einsum('gpcd,pq->gpcqd', w, eye).reshape(ng, LANES, LANES)

    wgate = jnp.concatenate([dense(rg_wa[0]), dense(rg_wx[0]), dense(rg_wa[1]), dense(rg_wx[1])], axis=-1)
    grp = lambda v: v.reshape(ng, 1, LANES)
    bgate = jnp.concatenate([grp(rg_ba[0]), grp(rg_bx[0]), grp(rg_ba[1]), grp(rg_bx[1])], axis=-1)
    lam = jnp.concatenate([grp(rg_lam[0]), grp(rg_lam[1])], axis=-1)
    return wgate.astype(BF16), bgate.astype(F32), lam.astype(F32)


def _mla_finish(result):
    (acc1, l1), (acc2, l2) = result
    return jnp.concatenate([acc1 / l1, acc2 / l2], axis=0).T


def _mla_attn(q, k, v):
    return _stream_attn(q, k, v, (), _mla_finish, groups=D_HEADS // 2, masked=False, name="mla_attn")


def _rope_angles(dim):
    inv = 1.0 / (ROPE_THETA ** (jnp.arange(0, dim, 2, dtype=F32) / dim))
    ang = jnp.arange(SEQ, dtype=F32)[:, None] * inv[None, :]
    return jnp.cos(ang), jnp.sin(ang)


def _even_rope_tables():
    cos, sin = _rope_angles(A_DH)
    reps = LANES // A_DH
    return (jnp.tile(jnp.concatenate([cos, cos], -1), (1, reps)),
            jnp.tile(jnp.concatenate([-sin, sin], -1), (1, reps)))


def _odd_rope_tables():
    cos, sin = _rope_angles(D_ROPE)
    ones = jnp.ones((SEQ, D_NOPE), F32)
    zn = jnp.zeros((SEQ, D_NOPE), F32)
    zp = jnp.zeros((SEQ, PAD_DK - D_NOPE - D_ROPE), F32)
    return (jnp.concatenate([ones, cos, cos, zp], -1),
            jnp.concatenate([zn, -sin, sin, zp], -1))


def _odd_weights(w_in, wuq, wukv):
    base = 2 * C_WIDTH + D_Q_RANK + D_KV_RANK
    zl = jnp.zeros((D_MODEL, D_NOPE), w_in.dtype)
    zr = jnp.zeros((D_MODEL, PAD_DK - D_NOPE - D_ROPE), w_in.dtype)
    w_in_p = jnp.concatenate([w_in[:, :base], zl, w_in[:, base:], zr], axis=-1)
    dqk = D_NOPE + D_ROPE
    wuq_p = jnp.pad(wuq.reshape(D_Q_RANK, D_HEADS, dqk), ((0, 0), (0, 0), (0, PAD_DK - dqk)))
    wukv_h = wukv.reshape(D_KV_RANK, D_HEADS, D_NOPE + D_VDIM)
    wk_p = jnp.pad(wukv_h[:, :, :D_NOPE], ((0, 0), (0, 0), (0, PAD_DK - D_NOPE)))
    wv = wukv_h[:, :, D_NOPE:]
    return (w_in_p.astype(BF16), wuq_p.reshape(D_Q_RANK, -1).astype(BF16),
            wk_p.reshape(D_KV_RANK, -1).astype(BF16), wv.reshape(D_KV_RANK, -1).T.astype(BF16))


def kernel(x, ffn1_norm, ffn1_wg, ffn1_wu, ffn1_wd, mix_norm, ffn2_norm, ffn2_wg, ffn2_wu, ffn2_wd, final_norm, ev_w_in, ev_w_out, diff_lam, diff_subln, na_rpb, od_w_in, od_w_out, conv_w, conv_b, rg_wa, rg_ba, rg_wx, rg_bx, rg_lam, mla_gq, mla_gkv, mla_wuq, mla_wukv):
    row = lambda v: v.reshape(1, -1).astype(F32)
    bf = lambda w: w.astype(BF16)
    xt = x.reshape(TOKENS, D_MODEL)

    f32 = lambda w: w.astype(F32)
    ffn1 = (f32(ffn1_wg), f32(ffn1_wu), f32(ffn1_wd))
    ffn2 = (f32(ffn2_wg), f32(ffn2_wu), f32(ffn2_wd))

    xt = _ffn(xt, row(ffn1_norm[0]), *ffn1, 0)
    cos_a, sin_a = _even_rope_tables()
    wv_cols = jnp.concatenate([ev_w_in[0][:, 2 * A_QK:2 * A_QK + A_V], ev_w_in[0][:, EVEN_IN - B_W:]], axis=1)
    qa, ka, vt, qb, kb = _even_in(xt, row(mix_norm[0]), bf(ev_w_in[0]), bf(wv_cols.T), cos_a, sin_a)
    lam_init0 = 0.8 - 0.6 * math.exp(-0.3 * 0)
    oa = _diff_attn(qa, ka, vt, diff_lam[0].astype(F32), row(diff_subln[0]), lam_init0)
    ob = _na_attn(qb, kb, vt, _na_bias_table(na_rpb[0]))
    xt = _ffn(xt, row(ffn2_norm[0]), *ffn2, 0, mix=(oa, ob, bf(ev_w_out[0])))

    xt = _ffn(xt, row(ffn1_norm[1]), *ffn1, 1)
    cos_d, sin_d = _odd_rope_tables()
    w_in_p, wuq_p, wk_p, wv_p = _odd_weights(od_w_in[0], mla_wuq[0], mla_wukv[0])
    xc, gc, q, k, v = _odd_in(xt, row(mix_norm[1]), w_in_p, row(mla_gq[0]), row(mla_gkv[0]),
                              wuq_p, wk_p, wv_p, cos_d, sin_d)
    wgate, bgate, lam = _rglru_gate_params(rg_wa[0], rg_ba[0], rg_wx[0], rg_bx[0], rg_lam[0])
    oc = _rglru(xc, gc, conv_w[0].astype(F32), row(conv_b[0]), wgate, bgate, lam)
    od = _mla_attn(q, k, v)
    xt = _ffn(xt, row(ffn2_norm[1]), *ffn2, 1, mix=(oc, od, bf(od_w_out[0])), final_g=row(final_norm))
    return xt.reshape(BATCH, SEQ, D_MODEL)
```

```python
import functools
import itertools
import math

import jax
import jax.numpy as jnp
import numpy as np
from jax import lax
from jax.experimental import pallas as pl
from jax.experimental.pallas import tpu as pltpu

F32 = jnp.float32
BF16 = jnp.bfloat16

D_MODEL = 1024
BATCH = 4
SEQ = 4096
DEPTH = 2
TOKENS = BATCH * SEQ
RMS_EPS = 1e-6
ROPE_THETA = 10000.0
GRID_W = 64
GRID_ROWS = SEQ // GRID_W
D_FF = 2816

A_HEADS = 4
A_DH = 64
A_QK = A_HEADS * 2 * A_DH
A_V = A_HEADS * 2 * A_DH
B_HEADS = 8
B_DH = 64
B_W = B_HEADS * B_DH
NA_ROWS = 8
NA_COLS = 16
C_WIDTH = 512
C_BLOCKS = 8
C_BW = C_WIDTH // C_BLOCKS
CONV_W = 4
RG_C = 8.0
D_HEADS = 8
D_NOPE = 64
D_ROPE = 32
D_VDIM = 64
D_Q_RANK = 256
D_KV_RANK = 128
EVEN_IN = 2 * A_QK + A_V + 3 * B_W
ODD_IN = 2 * C_WIDTH + D_Q_RANK + D_KV_RANK + D_ROPE

LANES = 128
SUBLANES = 8
VMEM_LIMIT_BYTES = 56 * 1024 * 1024

TOKEN_TILE = 1024
FFN_TILE = 1024
FF_CHUNK = 256
PV_TILE = 256
KEY_CHUNK = 1024
LOGIT_CHUNK = 1024
PAD_DK = 128
SCAN_UNROLL = 8

_NT = (((1,), (1,)), ((), ()))

LOG2E = math.log2(math.e)


def _params(*sem):
    return pltpu.CompilerParams(dimension_semantics=sem, vmem_limit_bytes=VMEM_LIMIT_BYTES)


def _resident(shape):
    nd = len(shape)
    return pl.BlockSpec(shape, lambda *_: (0,) * nd, pipeline_mode=pl.Buffered(1))


def _rms(x, g):
    return x * lax.rsqrt(jnp.mean(x * x, axis=-1, keepdims=True) + RMS_EPS) * g


def _dot(a, b):
    return jnp.dot(a, b, preferred_element_type=F32)


def _sigmoid(x):
    return 0.5 * (jnp.tanh(0.5 * x) + 1.0)


def _rope_lanes(x, cos, sin_signed, half, lo_mask):
    n = x.shape[-1]
    partner = jnp.where(lo_mask, pltpu.roll(x, n - half, 1), pltpu.roll(x, half, 1))
    return x * cos + partner * sin_signed


def _ffn_kernel(*refs, has_mix, has_final):
    it = iter(refs)
    x_ref = next(it)
    if has_mix:
        o1_ref, o2_ref, wo_ref = next(it), next(it), next(it)
    g_ref, wg_ref, wu_ref, wd_ref = next(it), next(it), next(it), next(it)
    gf_ref = next(it) if has_final else None
    y_ref = next(it)
    acc_ref = next(it)

    x = x_ref[...]
    if has_mix:
        half = wo_ref.shape[0] // 2
        x = x + _dot(o1_ref[...], wo_ref[:half, :]) + _dot(o2_ref[...], wo_ref[half:, :])
    n = _rms(x, g_ref[...]).astype(BF16)
    for c in range(D_FF // FF_CHUNK):
        sl = slice(c * FF_CHUNK, (c + 1) * FF_CHUNK)
        gate = _dot(n, wg_ref[:, sl])
        up = _dot(n, wu_ref[:, sl])
        h = (gate * jax.nn.sigmoid(gate) * up).astype(BF16)
        d = _dot(h, wd_ref[sl, :])
        if c == 0:
            acc_ref[...] = d
        else:
            acc_ref[...] += d
    y = x + 0.5 * acc_ref[...]
    if has_final:
        y = _rms(y, gf_ref[...])
    y_ref[...] = y


def _ffn(x, g, wg, wu, wd, layer, mix=None, final_g=None):
    tm = FFN_TILE
    tok = lambda w: pl.BlockSpec((tm, w), lambda i: (i, 0))
    slab = lambda a: pl.BlockSpec((None,) + a.shape[1:], lambda i: (layer, 0, 0),
                                  pipeline_mode=pl.Buffered(1))
    args, specs = [x], [tok(D_MODEL)]
    if mix is not None:
        o1, o2, wo = mix
        args += [o1, o2, wo]
        specs += [tok(o1.shape[1]), tok(o2.shape[1]), _resident(wo.shape)]
    args += [g, wg, wu, wd]
    specs += [_resident(g.shape), slab(wg), slab(wu), slab(wd)]
    if final_g is not None:
        args.append(final_g)
        specs.append(_resident(final_g.shape))
    return pl.pallas_call(
        functools.partial(_ffn_kernel, has_mix=mix is not None, has_final=final_g is not None),
        grid=(TOKENS // tm,),
        in_specs=specs,
        out_specs=tok(D_MODEL),
        out_shape=jax.ShapeDtypeStruct((TOKENS, D_MODEL), F32),
        scratch_shapes=[pltpu.VMEM((tm, D_MODEL), F32)],
        compiler_params=_params("parallel"),
        name="ffn_mix" if mix is not None else "ffn",
    )(*args)


def _even_in_kernel(x_ref, g_ref, w_ref, wvt_ref, cos_ref, sin_ref,
                    qa_ref, ka_ref, vt_ref, qb_ref, kb_ref):
    h = _rms(x_ref[...], g_ref[...]).astype(BF16)
    cos, sin = cos_ref[...], sin_ref[...]
    lane = lax.broadcasted_iota(jnp.int32, cos.shape, 1)
    lo = (lane % A_DH) < (A_DH // 2)
    scale = A_DH ** -0.5 * LOG2E

    def proj(i):
        return _dot(h, w_ref[:, i * A_QK:(i + 1) * A_QK])

    qa, ka = proj(0), proj(1)
    for j in range(A_QK // LANES):
        sl = slice(j * LANES, (j + 1) * LANES)
        qa_ref[:, sl] = (_rope_lanes(qa[:, sl], cos, sin, A_DH // 2, lo) * scale).astype(BF16)
        ka_ref[:, sl] = _rope_lanes(ka[:, sl], cos, sin, A_DH // 2, lo).astype(BF16)
    vt_ref[0] = lax.dot_general(wvt_ref[...], h, _NT, preferred_element_type=F32).astype(BF16)
    qb_ref[...] = (proj(3) * (B_DH ** -0.5 * LOG2E)).astype(BF16)
    kb_ref[...] = proj(4).astype(BF16)


def _even_in(x, g, w_in, wvt, cos, sin):
    tm = TOKEN_TILE
    per_seq = SEQ // tm
    tok = lambda w: pl.BlockSpec((tm, w), lambda i: (i, 0))
    pos = pl.BlockSpec((tm, LANES), lambda i: (i % per_seq, 0))
    vw = wvt.shape[0]
    tr = pl.BlockSpec((1, vw, tm), lambda i: (i // per_seq, 0, i % per_seq))
    out = jax.ShapeDtypeStruct((TOKENS, A_QK), BF16)
    out_t = jax.ShapeDtypeStruct((BATCH, vw, SEQ), BF16)
    return pl.pallas_call(
        _even_in_kernel,
        grid=(TOKENS // tm,),
        in_specs=[tok(D_MODEL), _resident(g.shape), _resident(w_in.shape), _resident(wvt.shape), pos, pos],
        out_specs=[tok(A_QK), tok(A_QK), tr, tok(B_W), tok(B_W)],
        out_shape=[out, out, out_t, out, out],
        compiler_params=_params("parallel"),
        name="even_in",
    )(x, g, w_in, wvt, cos, sin)


def _logit_chunks(q, k_ref, s_buf, m_buf):
    tq, qw = q.shape
    lane = lax.broadcasted_iota(jnp.int32, q.shape, 1)
    zero = jnp.zeros_like(q)
    q2 = jnp.concatenate([jnp.where(lane < qw // 2, q, zero), jnp.where(lane >= qw // 2, q, zero)], axis=0)
    groups8 = LOGIT_CHUNK // SUBLANES
    mrun = [None, None]
    for c in range(SEQ // LOGIT_CHUNK):
        keys = slice(c * LOGIT_CHUNK, (c + 1) * LOGIT_CHUNK)
        s2 = lax.dot_general(k_ref[keys, :], q2, _NT, preferred_element_type=F32)
        for j in range(2):
            s = s2[:, j * tq:(j + 1) * tq]
            s_buf[j, keys, :] = s
            part = jnp.max(s.reshape(groups8, SUBLANES, tq), axis=0)
            mrun[j] = part if mrun[j] is None else jnp.maximum(mrun[j], part)
        yield
    for j in range(2):
        m_buf[j] = jnp.broadcast_to(jnp.max(mrun[j], axis=0, keepdims=True), (SUBLANES, tq))


def _softmax_pv_chunks(s_buf, m_buf, vt_ref, vrows, acc_scr, l_scr):
    tq = s_buf.shape[-1]
    groups8 = KEY_CHUNK // SUBLANES
    m = [m_buf[j][None] for j in range(2)]
    acc = [None, None]
    lrun = [None, None]
    for c in range(SEQ // KEY_CHUNK):
        keys = slice(c * KEY_CHUNK, (c + 1) * KEY_CHUNK)
        for j in range(2):
            e = jnp.exp2(s_buf[j, keys, :].reshape(groups8, SUBLANES, tq) - m[j])
            part = jnp.sum(e, axis=0)
            lrun[j] = part if lrun[j] is None else lrun[j] + part
            pv = _dot(vt_ref[0, vrows[j], keys], e.reshape(KEY_CHUNK, tq).astype(BF16))
            acc[j] = pv if acc[j] is None else acc[j] + pv
            yield
    for j in range(2):
        acc_scr[j, :acc[j].shape[0], :] = acc[j]
        l_scr[j] = lrun[j]


def _stream_attn_kernel(*refs, n_items, masked, finish):
    qf_ref, qn_ref, kn_ref, vt_ref = refs[:4]
    extra = refs[4:-7]
    o_ref, s_a, m_a, s_b, m_b, acc_scr, l_scr = refs[-7:]
    t = pl.program_id(0)
    dv = vt_ref.shape[1]
    vrows = [slice(None)] * 2 if masked else [slice(0, dv // 2), slice(dv // 2, dv)]
    rows = dv if masked else dv // 2

    n_sub = s_a.shape[0]

    def logit_items(q_ref, s_buf, m_buf):
        for sub in range(n_sub):
            q = q_ref[sub * PV_TILE:(sub + 1) * PV_TILE, :]
            yield from _logit_chunks(q, kn_ref, s_buf.at[sub], m_buf.at[sub])

    def softmax_items(s_buf, m_buf):
        for sub in range(n_sub):
            yield from _softmax_pv_chunks(s_buf.at[sub], m_buf.at[sub], vt_ref, vrows,
                                          acc_scr.at[sub], l_scr.at[sub])

    @pl.when(t == 0)
    def _():
        for _ in logit_items(qf_ref, s_a, m_a):
            pass
        acc_scr[...] = jnp.zeros_like(acc_scr)
        l_scr[...] = jnp.ones_like(l_scr)

    def finish_previous():
        for sub in range(n_sub):
            result = [(acc_scr[sub, j, :rows, :], jnp.sum(l_scr[sub, j], axis=0, keepdims=True))
                      for j in range(2)]
            o_ref[sub * PV_TILE:(sub + 1) * PV_TILE, :] = finish(result, *extra).astype(o_ref.dtype)

    def step(nxt, cur):
        finish_previous()
        logits = logit_items(qn_ref, nxt[0], nxt[1])
        softmax = softmax_items(cur[0], cur[1])
        ratio = (SEQ // KEY_CHUNK) * 2 // (SEQ // LOGIT_CHUNK)
        for i, _ in enumerate(softmax):
            if i % ratio == 0:
                next(logits, None)
        for _ in logits:
            pass

    live = t < n_items
    pl.when(live & ((t & 1) == 0))(lambda: step((s_b, m_b), (s_a, m_a)))
    pl.when(live & ((t & 1) == 1))(lambda: step((s_a, m_a), (s_b, m_b)))
    pl.when(t == n_items)(finish_previous)


def _stream_attn(q, k, vt, extra, finish, *, groups, masked, q_tile, name):
    tq = q_tile
    n_sub = tq // PV_TILE
    nq = SEQ // tq
    n_items = BATCH * groups * nq
    qw = q.shape[1] // groups

    def rows(t):
        return (t // (groups * nq)) * nq + t % nq

    def group(t):
        return (t // nq) % groups

    def batch(t):
        return t // (groups * nq)

    nxt = lambda t: jnp.minimum(t + 1, n_items - 1)
    cur = lambda t: jnp.minimum(t, n_items - 1)
    prv = lambda t: jnp.maximum(t - 1, 0)
    in_specs = [
        pl.BlockSpec((tq, qw), lambda t: (0, 0)),
        pl.BlockSpec((tq, qw), lambda t: (rows(nxt(t)), group(nxt(t)))),
        pl.BlockSpec((SEQ, qw), lambda t: (batch(nxt(t)), group(nxt(t)))),
        pl.BlockSpec((1, LANES, SEQ), lambda t: (batch(cur(t)), group(cur(t)), 0)),
    ] + [_resident(a.shape) for a in extra]
    sbuf = pltpu.VMEM((n_sub, 2, SEQ, PV_TILE), F32)
    mbuf = pltpu.VMEM((n_sub, 2, SUBLANES, PV_TILE), F32)
    return pl.pallas_call(
        functools.partial(_stream_attn_kernel, n_items=n_items, masked=masked, finish=finish),
        grid=(n_items + 1,),
        in_specs=in_specs,
        out_specs=pl.BlockSpec((tq, LANES), lambda t: (rows(prv(t)), group(prv(t)))),
        out_shape=jax.ShapeDtypeStruct((TOKENS, groups * LANES), BF16),
        scratch_shapes=[sbuf, mbuf, sbuf, mbuf,
                        pltpu.VMEM((n_sub, 2, LANES, PV_TILE), F32),
                        pltpu.VMEM((n_sub, 2, SUBLANES, PV_TILE), F32)],
        compiler_params=_params("arbitrary"),
        name=name,
    )(q, q, k, vt, *extra)


def _diff_finish(result, lam_ref, g_ref, *, lam_init):
    (acc1, l1), (acc2, l2) = result
    lf = lam_ref[...]
    lam = (jnp.exp(jnp.sum(lf[0:1] * lf[1:2], axis=-1, keepdims=True))
           - jnp.exp(jnp.sum(lf[2:3] * lf[3:4], axis=-1, keepdims=True)) + lam_init)
    o = (acc1 / l1 - lam * (acc2 / l2)).T
    return _rms(o, g_ref[...]) * (1.0 - lam_init)


def _diff_attn(qa, ka, va, lam_vec, subln_g, lam_init):
    return _stream_attn(qa, ka, va, (lam_vec, subln_g),
                        functools.partial(_diff_finish, lam_init=lam_init),
                        groups=A_HEADS, masked=True, q_tile=2 * PV_TILE, name="diff_attn")


NA_QROWS = 4
NA_KROWS = NA_QROWS + NA_ROWS
NA_BLOCKS = GRID_ROWS // NA_QROWS
NA_PLACEMENTS = 3


def _na_window_start(g):
    lo, hi = 0, GRID_ROWS - NA_KROWS
    s = g * NA_QROWS - NA_ROWS // 2
    if isinstance(g, int):
        return min(max(s, lo), hi)
    return jnp.clip(s, lo, hi)


def _na_attn_kernel(q_ref, k_ref, vt_ref, bias_ref, o_ref, s_a, m_a, s_b, m_b):
    nq = NA_QROWS * GRID_W
    nk = NA_KROWS * GRID_W
    groups8 = nk // SUBLANES
    lane = lax.broadcasted_iota(jnp.int32, (nq, 2 * B_DH), 1)

    def key_start(g):
        return pl.multiple_of(_na_window_start(g) * GRID_W, NA_QROWS * GRID_W)

    def logits(g, s_buf, m_buf):
        var = jnp.where(g == 0, 0, jnp.where(g == NA_BLOCKS - 1, 2, 1))
        q = q_ref[pl.ds(pl.multiple_of(g * nq, nq), nq), :]
        zero = jnp.zeros_like(q)
        q2 = jnp.concatenate([jnp.where(lane < B_DH, q, zero), jnp.where(lane >= B_DH, q, zero)], axis=0)
        s2 = lax.dot_general(k_ref[pl.ds(key_start(g), nk), :], q2, _NT, preferred_element_type=F32)
        for j in range(2):
            s = s2[:, j * nq:(j + 1) * nq] + bias_ref[j, var]
            s_buf[j] = s
            m = jnp.max(jnp.max(s.reshape(groups8, SUBLANES, nq), axis=0), axis=0, keepdims=True)
            m_buf[j] = jnp.broadcast_to(m, (SUBLANES, nq))

    def softmax_pv(g, s_buf, m_buf):
        outs = []
        for j in range(2):
            e = jnp.exp2(s_buf[j].reshape(groups8, SUBLANES, nq) - m_buf[j][None])
            l = jnp.sum(jnp.sum(e, axis=0), axis=0, keepdims=True)
            vt = vt_ref[0, j * B_DH:(j + 1) * B_DH, pl.ds(key_start(g), nk)]
            outs.append(_dot(vt, e.reshape(nk, nq).astype(BF16)) / l)
        o_ref[pl.ds(pl.multiple_of(g * nq, nq), nq), :] = jnp.concatenate(outs, axis=0).T.astype(BF16)

    logits(0, s_a, m_a)

    def pair(i, carry):
        g = 2 * i
        logits(g + 1, s_b, m_b)
        softmax_pv(g, s_a, m_a)
        logits(jnp.minimum(g + 2, NA_BLOCKS - 1), s_a, m_a)
        softmax_pv(g + 1, s_b, m_b)
        return carry

    lax.fori_loop(0, NA_BLOCKS // 2, pair, 0)


def _na_attn(qb, kb, vt, bias):
    hw = 2 * B_DH
    first_block = A_V // hw
    spec = pl.BlockSpec((SEQ, hw), lambda p, b: (b, p))
    vspec = pl.BlockSpec((1, hw, SEQ), lambda p, b: (b, first_block + p, 0))
    bspec = pl.BlockSpec((2,) + bias.shape[1:], lambda p, b: (p, 0, 0, 0))
    return pl.pallas_call(
        _na_attn_kernel,
        grid=(B_HEADS // 2, BATCH),
        in_specs=[spec, spec, vspec, bspec],
        out_specs=spec,
        out_shape=jax.ShapeDtypeStruct((TOKENS, B_W), BF16),
        scratch_shapes=[pltpu.VMEM((2,) + bias.shape[2:], F32), pltpu.VMEM((2, SUBLANES, bias.shape[3]), F32)] * 2,
        compiler_params=_params("parallel", "parallel"),
        name="na_attn",
    )(qb, kb, vt, bias)


def _na_bias_table(rpb):
    c = np.arange(GRID_W)
    c0 = np.clip(c - NA_COLS // 2, 0, GRID_W - NA_COLS)
    col_in = (c[None, :] >= c0[:, None]) & (c[None, :] < c0[:, None] + NA_COLS)
    dc = np.clip(c[None, :] - c[:, None], -(NA_COLS - 1), NA_COLS - 1) + NA_COLS - 1
    rpb = rpb.astype(F32) * LOG2E
    n_dr = 2 * NA_ROWS - 1
    toe = jnp.zeros((B_HEADS, GRID_W, n_dr, GRID_W), F32)
    for d in range(2 * NA_COLS - 1):
        toe = jnp.where((dc.T == d)[:, None, :], rpb[:, None, ::-1, d, None], toe)
    toe = jnp.where(col_in.T[:, None, :], toe, -jnp.inf).reshape(B_HEADS, GRID_W, n_dr * GRID_W)
    masked = lambda n: [jnp.full((B_HEADS, GRID_W, n * GRID_W), -jnp.inf, F32)] if n else []
    first_key_row = lambda qr: min(max(qr - NA_ROWS // 2, 0), GRID_ROWS - NA_ROWS)
    slabs = []
    for g in (0, 1, NA_BLOCKS - 1):
        for kl in range(NA_KROWS):
            kr = _na_window_start(g) + kl
            qrs = [g * NA_QROWS + ql for ql in range(NA_QROWS)]
            ok = [ql for ql, qr in enumerate(qrs) if first_key_row(qr) <= kr < first_key_row(qr) + NA_ROWS]
            if not ok:
                slabs.append(masked(NA_QROWS)[0])
                continue
            lo, hi = ok[0], ok[-1]
            assert ok == list(range(lo, hi + 1))
            first = n_dr - 1 - (kr - qrs[lo] + NA_ROWS - 1)
            own = toe[:, :, first * GRID_W:(first + hi - lo + 1) * GRID_W]
            slabs.append(jnp.concatenate(masked(lo) + [own] + masked(NA_QROWS - 1 - hi), axis=-1))
    return jnp.stack(slabs, axis=1).reshape(
        B_HEADS, NA_PLACEMENTS, NA_KROWS * GRID_W, NA_QROWS * GRID_W)


def _odd_in_kernel(x_ref, g_ref, w_ref, gq_ref, gkv_ref, wuq_ref, wk_ref, wvt_ref,
                   cos_ref, sin_ref, xc_ref, gc_ref, q_ref, k_ref, vt_ref):
    h = _rms(x_ref[...], g_ref[...]).astype(BF16)
    cos, sin = cos_ref[...], sin_ref[...]
    lane = lax.broadcasted_iota(jnp.int32, cos.shape, 1)
    lo = lane < D_NOPE + D_ROPE // 2
    scale = (D_NOPE + D_ROPE) ** -0.5 * LOG2E
    o = 0
    xc_ref[...] = _dot(h, w_ref[:, o:o + C_WIDTH]); o += C_WIDTH
    gc_ref[...] = _dot(h, w_ref[:, o:o + C_WIDTH]); o += C_WIDTH
    cq = _dot(h, w_ref[:, o:o + D_Q_RANK]); o += D_Q_RANK
    ckv = _dot(h, w_ref[:, o:o + D_KV_RANK]); o += D_KV_RANK
    kr = _dot(h, w_ref[:, o:o + PAD_DK])
    q = _dot(_rms(cq, gq_ref[...]).astype(BF16), wuq_ref[...])
    ckvn = _rms(ckv, gkv_ref[...]).astype(BF16)
    kn = _dot(ckvn, wk_ref[...])
    kpe = _rope_lanes(kr, cos, sin, D_ROPE // 2, lo)
    for j in range(D_HEADS):
        sl = slice(j * PAD_DK, (j + 1) * PAD_DK)
        q_ref[:, sl] = (_rope_lanes(q[:, sl], cos, sin, D_ROPE // 2, lo) * scale).astype(BF16)
        k_ref[:, sl] = (kn[:, sl] + kpe).astype(BF16)
    vt_ref[0] = lax.dot_general(wvt_ref[...], ckvn, _NT, preferred_element_type=F32).astype(BF16)


def _odd_in(x, g, w_in, gq, gkv, wuq, wk, wvt, cos, sin):
    tm = TOKEN_TILE
    per_seq = SEQ // tm
    tok = lambda w: pl.BlockSpec((tm, w), lambda i: (i, 0))
    pos = pl.BlockSpec((tm, LANES), lambda i: (i % per_seq, 0))
    sds = lambda w, dt: jax.ShapeDtypeStruct((TOKENS, w), dt)
    qw = D_HEADS * PAD_DK
    vw = D_HEADS * D_VDIM
    return pl.pallas_call(
        _odd_in_kernel,
        grid=(TOKENS // tm,),
        in_specs=[tok(D_MODEL)] + [_resident(a.shape) for a in (g, w_in, gq, gkv, wuq, wk, wvt)] + [pos, pos],
        out_specs=[tok(C_WIDTH), tok(C_WIDTH), tok(qw), tok(qw),
                   pl.BlockSpec((1, vw, tm), lambda i: (i // per_seq, 0, i % per_seq))],
        out_shape=[sds(C_WIDTH, F32), sds(C_WIDTH, F32), sds(qw, BF16), sds(qw, BF16),
                   jax.ShapeDtypeStruct((BATCH, vw, SEQ), BF16)],
        compiler_params=_params("parallel"),
        name="odd_in",
    )(x, g, w_in, gq, gkv, wuq, wk, wvt, cos, sin)


def _rglru_kernel(xc_ref, gc_ref, cw_ref, cb_ref, wgate_ref, bgate_ref, lam_ref, o_ref,
                  a_scr, b_scr, h_scr):
    cw = cw_ref[...]
    x = xc_ref[...]
    row = lax.broadcasted_iota(jnp.int32, x.shape, 0)

    def shifted(d):
        rolled = pltpu.roll(x, (-d) % SEQ, 0)
        valid = (row + d >= 0) & (row + d < SEQ)
        return jnp.where(valid, rolled, 0.0)

    lp = (CONV_W - 1) // 2
    u = cb_ref[...] + sum((x if j == lp else shifted(j - lp)) * cw[j:j + 1] for j in range(CONV_W))
    gates = _dot(u.astype(BF16), wgate_ref[0]) + bgate_ref[0]
    lam = lam_ref[0]
    for d in range(2):
        o = 2 * d * LANES
        r_t = _sigmoid(gates[:, o:o + LANES])
        i_t = _sigmoid(gates[:, o + LANES:o + 2 * LANES])
        log_a = (-RG_C * r_t) * jax.nn.softplus(-lam[:, d * LANES:(d + 1) * LANES])
        a = jnp.exp(log_a)
        y = -jnp.tanh(log_a) * (a * a + 1.0)
        mult = jnp.where(y > 0.0, y * lax.rsqrt(y), 0.0)
        mult = jnp.where(row == (SEQ - 1 if d else 0), 1.0, mult)
        a_scr[d] = a
        b_scr[d] = mult * i_t * u

    n_tiles = SEQ // SUBLANES
    srow = lax.broadcasted_iota(jnp.int32, (SUBLANES, LANES), 0)

    def tile_scan(a, b, h_prev, reverse):
        for d in (1, 2, 4):
            if reverse:
                keep = srow < SUBLANES - d
                shift = SUBLANES - d
            else:
                keep = srow >= d
                shift = d
            a_s = jnp.where(keep, pltpu.roll(a, shift, 0), 1.0)
            b_s = jnp.where(keep, pltpu.roll(b, shift, 0), 0.0)
            b = a * b_s + b
            a = a * a_s
        return a * h_prev + b

    def step(i, carry):
        hf, hr = carry
        tf = pl.multiple_of(i * SUBLANES, SUBLANES)
        h = tile_scan(a_scr[0, pl.ds(tf, SUBLANES), :], b_scr[0, pl.ds(tf, SUBLANES), :], hf, False)
        h_scr[0, pl.ds(tf, SUBLANES), :] = h
        hf = jnp.broadcast_to(h[SUBLANES - 1:SUBLANES, :], h.shape)
        tr = pl.multiple_of((n_tiles - 1 - i) * SUBLANES, SUBLANES)
        h = tile_scan(a_scr[1, pl.ds(tr, SUBLANES), :], b_scr[1, pl.ds(tr, SUBLANES), :], hr, True)
        h_scr[1, pl.ds(tr, SUBLANES), :] = h
        hr = jnp.broadcast_to(h[0:1, :], h.shape)
        return hf, hr

    z = jnp.zeros((SUBLANES, LANES), F32)
    lax.fori_loop(0, n_tiles, step, (z, z), unroll=SCAN_UNROLL)
    o_ref[...] = (jax.nn.gelu(gc_ref[...]) * (h_scr[0] + h_scr[1])).astype(BF16)


def _rglru(xc, gc, conv_w, conv_b, wgate, bgate, lam):
    ng = C_WIDTH // LANES
    seq = pl.BlockSpec((SEQ, LANES), lambda b, g: (b, g))
    grp = lambda a: pl.BlockSpec((1,) + a.shape[1:], lambda b, g: (g, 0, 0))
    return pl.pallas_call(
        _rglru_kernel,
        grid=(BATCH, ng),
        in_specs=[seq, seq,
                  pl.BlockSpec((CONV_W, LANES), lambda b, g: (0, g)),
                  pl.BlockSpec((1, LANES), lambda b, g: (0, g)),
                  grp(wgate), grp(bgate), grp(lam)],
        out_specs=seq,
        out_shape=jax.ShapeDtypeStruct((TOKENS, C_WIDTH), BF16),
        scratch_shapes=[pltpu.VMEM((2, SEQ, LANES), F32)] * 3,
        compiler_params=_params("parallel", "parallel"),
        name="rglru",
    )(xc, gc, conv_w, conv_b, wgate, bgate, lam)


def _rglru_gate_params(rg_wa, rg_ba, rg_wx, rg_bx, rg_lam):
    ng = C_WIDTH // LANES
    per = LANES // C_BW

    def dense(w):
        w = w.reshape(ng, per, C_BW, C_BW)
        eye = jnp.eye(per, dtype=w.dtype)
        return jnp.einsum('gpcd,pq->gpcqd', w, eye).reshape(ng, LANES, LANES)

    wgate = jnp.concatenate([dense(rg_wa[0]), dense(rg_wx[0]), dense(rg_wa[1]), dense(rg_wx[1])], axis=-1)
    grp = lambda v: v.reshape(ng, 1, LANES)
    bgate = jnp.concatenate([grp(rg_ba[0]), grp(rg_bx[0]), grp(rg_ba[1]), grp(rg_bx[1])], axis=-1)
    lam = jnp.concatenate([grp(rg_lam[0]), grp(rg_lam[1])], axis=-1)
    return wgate.astype(BF16), bgate.astype(F32), lam.astype(F32)


def _mla_finish(result):
    (acc1, l1), (acc2, l2) = result
    return jnp.concatenate([acc1 / l1, acc2 / l2], axis=0).T


def _mla_attn(q, k, v):
    return _stream_attn(q, k, v, (), _mla_finish, groups=D_HEADS // 2, masked=False, q_tile=PV_TILE,
                        name="mla_attn")


def _rope_angles(dim):
    inv = 1.0 / (ROPE_THETA ** (jnp.arange(0, dim, 2, dtype=F32) / dim))
    ang = jnp.arange(SEQ, dtype=F32)[:, None] * inv[None, :]
    return jnp.cos(ang), jnp.sin(ang)


def _even_rope_tables():
    cos, sin = _rope_angles(A_DH)
    reps = LANES // A_DH
    return (jnp.tile(jnp.concatenate([cos, cos], -1), (1, reps)),
            jnp.tile(jnp.concatenate([-sin, sin], -1), (1, reps)))


def _odd_rope_tables():
    cos, sin = _rope_angles(D_ROPE)
    ones = jnp.ones((SEQ, D_NOPE), F32)
    zn = jnp.zeros((SEQ, D_NOPE), F32)
    zp = jnp.zeros((SEQ, PAD_DK - D_NOPE - D_ROPE), F32)
    return (jnp.concatenate([ones, cos, cos, zp], -1),
            jnp.concatenate([zn, -sin, sin, zp], -1))


def _odd_weights(w_in, wuq, wukv):
    base = 2 * C_WIDTH + D_Q_RANK + D_KV_RANK
    zl = jnp.zeros((D_MODEL, D_NOPE), w_in.dtype)
    zr = jnp.zeros((D_MODEL, PAD_DK - D_NOPE - D_ROPE), w_in.dtype)
    w_in_p = jnp.concatenate([w_in[:, :base], zl, w_in[:, base:], zr], axis=-1)
    dqk = D_NOPE + D_ROPE
    wuq_p = jnp.pad(wuq.reshape(D_Q_RANK, D_HEADS, dqk), ((0, 0), (0, 0), (0, PAD_DK - dqk)))
    wukv_h = wukv.reshape(D_KV_RANK, D_HEADS, D_NOPE + D_VDIM)
    wk_p = jnp.pad(wukv_h[:, :, :D_NOPE], ((0, 0), (0, 0), (0, PAD_DK - D_NOPE)))
    wv = wukv_h[:, :, D_NOPE:]
    return (w_in_p.astype(BF16), wuq_p.reshape(D_Q_RANK, -1).astype(BF16),
            wk_p.reshape(D_KV_RANK, -1).astype(BF16), wv.reshape(D_KV_RANK, -1).T.astype(BF16))


def kernel(x, ffn1_norm, ffn1_wg, ffn1_wu, ffn1_wd, mix_norm, ffn2_norm, ffn2_wg, ffn2_wu, ffn2_wd, final_norm, ev_w_in, ev_w_out, diff_lam, diff_subln, na_rpb, od_w_in, od_w_out, conv_w, conv_b, rg_wa, rg_ba, rg_wx, rg_bx, rg_lam, mla_gq, mla_gkv, mla_wuq, mla_wukv):
    row = lambda v: v.reshape(1, -1).astype(F32)
    bf = lambda w: w.astype(BF16)
    xt = x.reshape(TOKENS, D_MODEL)

    ffn1 = (bf(ffn1_wg), bf(ffn1_wu), bf(ffn1_wd))
    ffn2 = (bf(ffn2_wg), bf(ffn2_wu), bf(ffn2_wd))

    xt = _ffn(xt, row(ffn1_norm[0]), *ffn1, 0)
    cos_a, sin_a = _even_rope_tables()
    wv_cols = jnp.concatenate([ev_w_in[0][:, 2 * A_QK:2 * A_QK + A_V], ev_w_in[0][:, EVEN_IN - B_W:]], axis=1)
    qa, ka, vt, qb, kb = _even_in(xt, row(mix_norm[0]), bf(ev_w_in[0]), bf(wv_cols.T), cos_a, sin_a)
    lam_init0 = 0.8 - 0.6 * math.exp(-0.3 * 0)
    oa = _diff_attn(qa, ka, vt, diff_lam[0].astype(F32), row(diff_subln[0]), lam_init0)
    ob = _na_attn(qb, kb, vt, _na_bias_table(na_rpb[0]))
    xt = _ffn(xt, row(ffn2_norm[0]), *ffn2, 0, mix=(oa, ob, bf(ev_w_out[0])))

    xt = _ffn(xt, row(ffn1_norm[1]), *ffn1, 1)
    cos_d, sin_d = _odd_rope_tables()
    w_in_p, wuq_p, wk_p, wv_p = _odd_weights(od_w_in[0], mla_wuq[0], mla_wukv[0])
    xc, gc, q, k, v = _odd_in(xt, row(mix_norm[1]), w_in_p, row(mla_gq[0]), row(mla_gkv[0]),
                              wuq_p, wk_p, wv_p, cos_d, sin_d)
    wgate, bgate, lam = _rglru_gate_params(rg_wa[0], rg_ba[0], rg_wx[0], rg_bx[0], rg_lam[0])
    oc = _rglru(xc, gc, conv_w[0].astype(F32), row(conv_b[0]), wgate, bgate, lam)
    od = _mla_attn(q, k, v)
    xt = _ffn(xt, row(ffn2_norm[1]), *ffn2, 1, mix=(oc, od, bf(od_w_out[0])), final_g=row(final_norm))
    return xt.reshape(BATCH, SEQ, D_MODEL)
```

```python
import functools
import itertools
import math

import jax
import jax.numpy as jnp
import numpy as np
from jax import lax
from jax.experimental import pallas as pl
from jax.experimental.pallas import tpu as pltpu

F32 = jnp.float32
BF16 = jnp.bfloat16

D_MODEL = 1024
BATCH = 4
SEQ = 4096
DEPTH = 2
TOKENS = BATCH * SEQ
RMS_EPS = 1e-6
ROPE_THETA = 10000.0
GRID_W = 64
GRID_ROWS = SEQ // GRID_W
D_FF = 2816

A_HEADS = 4
A_DH = 64
A_QK = A_HEADS * 2 * A_DH
A_V = A_HEADS * 2 * A_DH
B_HEADS = 8
B_DH = 64
B_W = B_HEADS * B_DH
NA_ROWS = 8
NA_COLS = 16
C_WIDTH = 512
C_BLOCKS = 8
C_BW = C_WIDTH // C_BLOCKS
CONV_W = 4
RG_C = 8.0
D_HEADS = 8
D_NOPE = 64
D_ROPE = 32
D_VDIM = 64
D_Q_RANK = 256
D_KV_RANK = 128
EVEN_IN = 2 * A_QK + A_V + 3 * B_W
ODD_IN = 2 * C_WIDTH + D_Q_RANK + D_KV_RANK + D_ROPE

LANES = 128
SUBLANES = 8
VMEM_LIMIT_BYTES = 56 * 1024 * 1024

TOKEN_TILE = 1024
FFN_TILE = 1024
FF_CHUNK = 256
PV_TILE = 256
LOGIT_CHUNK = 1024
PAD_DK = 128
SCAN_UNROLL = 8

_NT = (((1,), (1,)), ((), ()))

LOG2E = math.log2(math.e)


def _params(*sem):
    return pltpu.CompilerParams(dimension_semantics=sem, vmem_limit_bytes=VMEM_LIMIT_BYTES)


def _resident(shape):
    nd = len(shape)
    return pl.BlockSpec(shape, lambda *_: (0,) * nd, pipeline_mode=pl.Buffered(1))


def _rms(x, g):
    return x * lax.rsqrt(jnp.mean(x * x, axis=-1, keepdims=True) + RMS_EPS) * g


def _dot(a, b):
    return jnp.dot(a, b, preferred_element_type=F32)


def _sigmoid(x):
    return 0.5 * (jnp.tanh(0.5 * x) + 1.0)


def _rope_lanes(x, cos, sin_signed, half, lo_mask):
    n = x.shape[-1]
    partner = jnp.where(lo_mask, pltpu.roll(x, n - half, 1), pltpu.roll(x, half, 1))
    return x * cos + partner * sin_signed


def _ffn_kernel(*refs, has_mix, has_final):
    it = iter(refs)
    x_ref = next(it)
    if has_mix:
        o1_ref, o2_ref, wo_ref = next(it), next(it), next(it)
    g_ref, wg_ref, wu_ref, wd_ref = next(it), next(it), next(it), next(it)
    gf_ref = next(it) if has_final else None
    y_ref = next(it)
    acc_ref = next(it)

    x = x_ref[...]
    if has_mix:
        half = wo_ref.shape[0] // 2
        x = x + _dot(o1_ref[...], wo_ref[:half, :]) + _dot(o2_ref[...], wo_ref[half:, :])
    n = _rms(x, g_ref[...]).astype(BF16)
    for c in range(D_FF // FF_CHUNK):
        sl = slice(c * FF_CHUNK, (c + 1) * FF_CHUNK)
        gate = _dot(n, wg_ref[:, sl])
        up = _dot(n, wu_ref[:, sl])
        h = (gate * jax.nn.sigmoid(gate) * up).astype(BF16)
        d = _dot(h, wd_ref[sl, :])
        if c == 0:
            acc_ref[...] = d
        else:
            acc_ref[...] += d
    y = x + 0.5 * acc_ref[...]
    if has_final:
        y = _rms(y, gf_ref[...])
    y_ref[...] = y


def _ffn(x, g, wg, wu, wd, layer, mix=None, final_g=None):
    tm = FFN_TILE
    tok = lambda w: pl.BlockSpec((tm, w), lambda i: (i, 0))
    slab = lambda a: pl.BlockSpec((None,) + a.shape[1:], lambda i: (layer, 0, 0),
                                  pipeline_mode=pl.Buffered(1))
    args, specs = [x], [tok(D_MODEL)]
    if mix is not None:
        o1, o2, wo = mix
        args += [o1, o2, wo]
        specs += [tok(o1.shape[1]), tok(o2.shape[1]), _resident(wo.shape)]
    args += [g, wg, wu, wd]
    specs += [_resident(g.shape), slab(wg), slab(wu), slab(wd)]
    if final_g is not None:
        args.append(final_g)
        specs.append(_resident(final_g.shape))
    return pl.pallas_call(
        functools.partial(_ffn_kernel, has_mix=mix is not None, has_final=final_g is not None),
        grid=(TOKENS // tm,),
        in_specs=specs,
        out_specs=tok(D_MODEL),
        out_shape=jax.ShapeDtypeStruct((TOKENS, D_MODEL), F32),
        scratch_shapes=[pltpu.VMEM((tm, D_MODEL), F32)],
        compiler_params=_params("parallel"),
        name="ffn_mix" if mix is not None else "ffn",
    )(*args)


def _even_in_kernel(x_ref, g_ref, w_ref, wvt_ref, cos_ref, sin_ref,
                    qa_ref, ka_ref, vt_ref, qb_ref, kb_ref):
    h = _rms(x_ref[...], g_ref[...]).astype(BF16)
    cos, sin = cos_ref[...], sin_ref[...]
    lane = lax.broadcasted_iota(jnp.int32, cos.shape, 1)
    lo = (lane % A_DH) < (A_DH // 2)
    scale = A_DH ** -0.5 * LOG2E

    def proj(i):
        return _dot(h, w_ref[:, i * A_QK:(i + 1) * A_QK])

    qa, ka = proj(0), proj(1)
    for j in range(A_QK // LANES):
        sl = slice(j * LANES, (j + 1) * LANES)
        qa_ref[:, sl] = (_rope_lanes(qa[:, sl], cos, sin, A_DH // 2, lo) * scale).astype(BF16)
        ka_ref[:, sl] = _rope_lanes(ka[:, sl], cos, sin, A_DH // 2, lo).astype(BF16)
    vt_ref[0] = lax.dot_general(wvt_ref[...], h, _NT, preferred_element_type=F32).astype(BF16)
    qb_ref[...] = (proj(3) * (B_DH ** -0.5 * LOG2E)).astype(BF16)
    kb_ref[...] = proj(4).astype(BF16)


def _even_in(x, g, w_in, wvt, cos, sin):
    tm = TOKEN_TILE
    per_seq = SEQ // tm
    tok = lambda w: pl.BlockSpec((tm, w), lambda i: (i, 0))
    pos = pl.BlockSpec((tm, LANES), lambda i: (i % per_seq, 0))
    vw = wvt.shape[0]
    tr = pl.BlockSpec((1, vw, tm), lambda i: (i // per_seq, 0, i % per_seq))
    out = jax.ShapeDtypeStruct((TOKENS, A_QK), BF16)
    out_t = jax.ShapeDtypeStruct((BATCH, vw, SEQ), BF16)
    return pl.pallas_call(
        _even_in_kernel,
        grid=(TOKENS // tm,),
        in_specs=[tok(D_MODEL), _resident(g.shape), _resident(w_in.shape), _resident(wvt.shape), pos, pos],
        out_specs=[tok(A_QK), tok(A_QK), tr, tok(B_W), tok(B_W)],
        out_shape=[out, out, out_t, out, out],
        compiler_params=_params("parallel"),
        name="even_in",
    )(x, g, w_in, wvt, cos, sin)


def _logit_chunks(q, k_ref, s_buf, m_buf):
    tq, qw = q.shape
    lane = lax.broadcasted_iota(jnp.int32, q.shape, 1)
    zero = jnp.zeros_like(q)
    q2 = jnp.concatenate([jnp.where(lane < qw // 2, q, zero), jnp.where(lane >= qw // 2, q, zero)], axis=0)
    groups8 = LOGIT_CHUNK // SUBLANES
    mrun = [None, None]
    for c in range(SEQ // LOGIT_CHUNK):
        keys = slice(c * LOGIT_CHUNK, (c + 1) * LOGIT_CHUNK)
        s2 = lax.dot_general(k_ref[keys, :], q2, _NT, preferred_element_type=F32)
        for j in range(2):
            s = s2[:, j * tq:(j + 1) * tq]
            s_buf[j, keys, :] = s
            part = jnp.max(s.reshape(groups8, SUBLANES, tq), axis=0)
            mrun[j] = part if mrun[j] is None else jnp.maximum(mrun[j], part)
        yield
    for j in range(2):
        m_buf[j] = jnp.broadcast_to(jnp.max(mrun[j], axis=0, keepdims=True), (SUBLANES, tq))


def _softmax_pv_chunks(s_buf, m_buf, vt_ref, vrows, acc_scr, l_scr, key_chunk):
    tq = s_buf.shape[-1]
    groups8 = key_chunk // SUBLANES
    m = [m_buf[j][None] for j in range(2)]
    acc = [None, None]
    lrun = [None, None]
    for c in range(SEQ // key_chunk):
        keys = slice(c * key_chunk, (c + 1) * key_chunk)
        for j in range(2):
            e = jnp.exp2(s_buf[j, keys, :].reshape(groups8, SUBLANES, tq) - m[j])
            part = jnp.sum(e, axis=0)
            lrun[j] = part if lrun[j] is None else lrun[j] + part
            pv = _dot(vt_ref[0, vrows[j], keys], e.reshape(key_chunk, tq).astype(BF16))
            acc[j] = pv if acc[j] is None else acc[j] + pv
            yield
    for j in range(2):
        acc_scr[j, :acc[j].shape[0], :] = acc[j]
        l_scr[j] = lrun[j]


def _stream_attn_kernel(*refs, n_items, masked, finish, key_chunk):
    qf_ref, qn_ref, kn_ref, vt_ref = refs[:4]
    extra = refs[4:-7]
    o_ref, s_a, m_a, s_b, m_b, acc_scr, l_scr = refs[-7:]
    t = pl.program_id(0)
    dv = vt_ref.shape[1]
    vrows = [slice(None)] * 2 if masked else [slice(0, dv // 2), slice(dv // 2, dv)]
    rows = dv if masked else dv // 2

    n_sub = s_a.shape[0]

    def logit_items(q_ref, s_buf, m_buf):
        for sub in range(n_sub):
            q = q_ref[sub * PV_TILE:(sub + 1) * PV_TILE, :]
            yield from _logit_chunks(q, kn_ref, s_buf.at[sub], m_buf.at[sub])

    def softmax_items(s_buf, m_buf):
        for sub in range(n_sub):
            yield from _softmax_pv_chunks(s_buf.at[sub], m_buf.at[sub], vt_ref, vrows,
                                          acc_scr.at[sub], l_scr.at[sub], key_chunk)

    @pl.when(t == 0)
    def _():
        for _ in logit_items(qf_ref, s_a, m_a):
            pass
        acc_scr[...] = jnp.zeros_like(acc_scr)
        l_scr[...] = jnp.ones_like(l_scr)

    def finish_previous():
        for sub in range(n_sub):
            result = [(acc_scr[sub, j, :rows, :], jnp.sum(l_scr[sub, j], axis=0, keepdims=True))
                      for j in range(2)]
            o_ref[sub * PV_TILE:(sub + 1) * PV_TILE, :] = finish(result, *extra).astype(o_ref.dtype)

    def step(nxt, cur):
        finish_previous()
        logits = logit_items(qn_ref, nxt[0], nxt[1])
        softmax = softmax_items(cur[0], cur[1])
        ratio = max((SEQ // key_chunk) * 2 // (SEQ // LOGIT_CHUNK), 1)
        for i, _ in enumerate(softmax):
            if i % ratio == 0:
                next(logits, None)
        for _ in logits:
            pass

    live = t < n_items
    pl.when(live & ((t & 1) == 0))(lambda: step((s_b, m_b), (s_a, m_a)))
    pl.when(live & ((t & 1) == 1))(lambda: step((s_a, m_a), (s_b, m_b)))
    pl.when(t == n_items)(finish_previous)


def _stream_attn(q, k, vt, extra, finish, *, groups, masked, q_tile, key_chunk, name):
    tq = q_tile
    n_sub = tq // PV_TILE
    nq = SEQ // tq
    n_items = BATCH * groups * nq
    qw = q.shape[1] // groups

    def rows(t):
        return (t // (groups * nq)) * nq + t % nq

    def group(t):
        return (t // nq) % groups

    def batch(t):
        return t // (groups * nq)

    nxt = lambda t: jnp.minimum(t + 1, n_items - 1)
    cur = lambda t: jnp.minimum(t, n_items - 1)
    prv = lambda t: jnp.maximum(t - 1, 0)
    in_specs = [
        pl.BlockSpec((tq, qw), lambda t: (0, 0)),
        pl.BlockSpec((tq, qw), lambda t: (rows(nxt(t)), group(nxt(t)))),
        pl.BlockSpec((SEQ, qw), lambda t: (batch(nxt(t)), group(nxt(t)))),
        pl.BlockSpec((1, LANES, SEQ), lambda t: (batch(cur(t)), group(cur(t)), 0)),
    ] + [_resident(a.shape) for a in extra]
    sbuf = pltpu.VMEM((n_sub, 2, SEQ, PV_TILE), F32)
    mbuf = pltpu.VMEM((n_sub, 2, SUBLANES, PV_TILE), F32)
    return pl.pallas_call(
        functools.partial(_stream_attn_kernel, n_items=n_items, masked=masked, finish=finish,
                          key_chunk=key_chunk),
        grid=(n_items + 1,),
        in_specs=in_specs,
        out_specs=pl.BlockSpec((tq, LANES), lambda t: (rows(prv(t)), group(prv(t)))),
        out_shape=jax.ShapeDtypeStruct((TOKENS, groups * LANES), BF16),
        scratch_shapes=[sbuf, mbuf, sbuf, mbuf,
                        pltpu.VMEM((n_sub, 2, LANES, PV_TILE), F32),
                        pltpu.VMEM((n_sub, 2, SUBLANES, PV_TILE), F32)],
        compiler_params=_params("arbitrary"),
        name=name,
    )(q, q, k, vt, *extra)


def _diff_finish(result, lam_ref, g_ref, *, lam_init):
    (acc1, l1), (acc2, l2) = result
    lf = lam_ref[...]
    lam = (jnp.exp(jnp.sum(lf[0:1] * lf[1:2], axis=-1, keepdims=True))
           - jnp.exp(jnp.sum(lf[2:3] * lf[3:4], axis=-1, keepdims=True)) + lam_init)
    o = (acc1 / l1 - lam * (acc2 / l2)).T
    return _rms(o, g_ref[...]) * (1.0 - lam_init)


def _diff_attn(qa, ka, va, lam_vec, subln_g, lam_init):
    return _stream_attn(qa, ka, va, (lam_vec, subln_g),
                        functools.partial(_diff_finish, lam_init=lam_init),
                        groups=A_HEADS, masked=True, q_tile=2 * PV_TILE, key_chunk=1024, name="diff_attn")


NA_QROWS = 4
NA_KROWS = NA_QROWS + NA_ROWS
NA_BLOCKS = GRID_ROWS // NA_QROWS
NA_PLACEMENTS = 3


def _na_window_start(g):
    lo, hi = 0, GRID_ROWS - NA_KROWS
    s = g * NA_QROWS - NA_ROWS // 2
    if isinstance(g, int):
        return min(max(s, lo), hi)
    return jnp.clip(s, lo, hi)


def _na_attn_kernel(q_ref, k_ref, vt_ref, bias_ref, o_ref, s_a, m_a, s_b, m_b):
    nq = NA_QROWS * GRID_W
    nk = NA_KROWS * GRID_W
    groups8 = nk // SUBLANES
    lane = lax.broadcasted_iota(jnp.int32, (nq, 2 * B_DH), 1)

    def key_start(g):
        return pl.multiple_of(_na_window_start(g) * GRID_W, NA_QROWS * GRID_W)

    def logits(g, s_buf, m_buf):
        var = jnp.where(g == 0, 0, jnp.where(g == NA_BLOCKS - 1, 2, 1))
        q = q_ref[pl.ds(pl.multiple_of(g * nq, nq), nq), :]
        zero = jnp.zeros_like(q)
        q2 = jnp.concatenate([jnp.where(lane < B_DH, q, zero), jnp.where(lane >= B_DH, q, zero)], axis=0)
        s2 = lax.dot_general(k_ref[pl.ds(key_start(g), nk), :], q2, _NT, preferred_element_type=F32)
        for j in range(2):
            s = s2[:, j * nq:(j + 1) * nq] + bias_ref[j, var]
            s_buf[j] = s
            m = jnp.max(jnp.max(s.reshape(groups8, SUBLANES, nq), axis=0), axis=0, keepdims=True)
            m_buf[j] = jnp.broadcast_to(m, (SUBLANES, nq))

    def softmax_pv(g, s_buf, m_buf):
        outs = []
        for j in range(2):
            e = jnp.exp2(s_buf[j].reshape(groups8, SUBLANES, nq) - m_buf[j][None])
            l = jnp.sum(jnp.sum(e, axis=0), axis=0, keepdims=True)
            vt = vt_ref[0, j * B_DH:(j + 1) * B_DH, pl.ds(key_start(g), nk)]
            outs.append(_dot(vt, e.reshape(nk, nq).astype(BF16)) / l)
        o_ref[pl.ds(pl.multiple_of(g * nq, nq), nq), :] = jnp.concatenate(outs, axis=0).T.astype(BF16)

    logits(0, s_a, m_a)

    def pair(i, carry):
        g = 2 * i
        logits(g + 1, s_b, m_b)
        softmax_pv(g, s_a, m_a)
        logits(jnp.minimum(g + 2, NA_BLOCKS - 1), s_a, m_a)
        softmax_pv(g + 1, s_b, m_b)
        return carry

    lax.fori_loop(0, NA_BLOCKS // 2, pair, 0)


def _na_attn(qb, kb, vt, bias):
    hw = 2 * B_DH
    first_block = A_V // hw
    spec = pl.BlockSpec((SEQ, hw), lambda p, b: (b, p))
    vspec = pl.BlockSpec((1, hw, SEQ), lambda p, b: (b, first_block + p, 0))
    bspec = pl.BlockSpec((2,) + bias.shape[1:], lambda p, b: (p, 0, 0, 0))
    return pl.pallas_call(
        _na_attn_kernel,
        grid=(B_HEADS // 2, BATCH),
        in_specs=[spec, spec, vspec, bspec],
        out_specs=spec,
        out_shape=jax.ShapeDtypeStruct((TOKENS, B_W), BF16),
        scratch_shapes=[pltpu.VMEM((2,) + bias.shape[2:], F32), pltpu.VMEM((2, SUBLANES, bias.shape[3]), F32)] * 2,
        compiler_params=_params("parallel", "parallel"),
        name="na_attn",
    )(qb, kb, vt, bias)


def _na_bias_table(rpb):
    c = np.arange(GRID_W)
    c0 = np.clip(c - NA_COLS // 2, 0, GRID_W - NA_COLS)
    col_in = (c[None, :] >= c0[:, None]) & (c[None, :] < c0[:, None] + NA_COLS)
    dc = np.clip(c[None, :] - c[:, None], -(NA_COLS - 1), NA_COLS - 1) + NA_COLS - 1
    rpb = rpb.astype(F32) * LOG2E
    n_dr = 2 * NA_ROWS - 1
    toe = jnp.zeros((B_HEADS, GRID_W, n_dr, GRID_W), F32)
    for d in range(2 * NA_COLS - 1):
        toe = jnp.where((dc.T == d)[:, None, :], rpb[:, None, ::-1, d, None], toe)
    toe = jnp.where(col_in.T[:, None, :], toe, -jnp.inf).reshape(B_HEADS, GRID_W, n_dr * GRID_W)
    masked = lambda n: [jnp.full((B_HEADS, GRID_W, n * GRID_W), -jnp.inf, F32)] if n else []
    first_key_row = lambda qr: min(max(qr - NA_ROWS // 2, 0), GRID_ROWS - NA_ROWS)
    slabs = []
    for g in (0, 1, NA_BLOCKS - 1):
        for kl in range(NA_KROWS):
            kr = _na_window_start(g) + kl
            qrs = [g * NA_QROWS + ql for ql in range(NA_QROWS)]
            ok = [ql for ql, qr in enumerate(qrs) if first_key_row(qr) <= kr < first_key_row(qr) + NA_ROWS]
            if not ok:
                slabs.append(masked(NA_QROWS)[0])
                continue
            lo, hi = ok[0], ok[-1]
            assert ok == list(range(lo, hi + 1))
            first = n_dr - 1 - (kr - qrs[lo] + NA_ROWS - 1)
            own = toe[:, :, first * GRID_W:(first + hi - lo + 1) * GRID_W]
            slabs.append(jnp.concatenate(masked(lo) + [own] + masked(NA_QROWS - 1 - hi), axis=-1))
    return jnp.stack(slabs, axis=1).reshape(
        B_HEADS, NA_PLACEMENTS, NA_KROWS * GRID_W, NA_QROWS * GRID_W)


def _odd_in_kernel(x_ref, g_ref, w_ref, gq_ref, gkv_ref, wuq_ref, wk_ref, wvt_ref,
                   cos_ref, sin_ref, xc_ref, gc_ref, q_ref, k_ref, vt_ref):
    h = _rms(x_ref[...], g_ref[...]).astype(BF16)
    cos, sin = cos_ref[...], sin_ref[...]
    lane = lax.broadcasted_iota(jnp.int32, cos.shape, 1)
    lo = lane < D_NOPE + D_ROPE // 2
    scale = (D_NOPE + D_ROPE) ** -0.5 * LOG2E
    o = 0
    xc_ref[...] = _dot(h, w_ref[:, o:o + C_WIDTH]); o += C_WIDTH
    gc_ref[...] = _dot(h, w_ref[:, o:o + C_WIDTH]); o += C_WIDTH
    cq = _dot(h, w_ref[:, o:o + D_Q_RANK]); o += D_Q_RANK
    ckv = _dot(h, w_ref[:, o:o + D_KV_RANK]); o += D_KV_RANK
    kr = _dot(h, w_ref[:, o:o + PAD_DK])
    q = _dot(_rms(cq, gq_ref[...]).astype(BF16), wuq_ref[...])
    ckvn = _rms(ckv, gkv_ref[...]).astype(BF16)
    kn = _dot(ckvn, wk_ref[...])
    kpe = _rope_lanes(kr, cos, sin, D_ROPE // 2, lo)
    for j in range(D_HEADS):
        sl = slice(j * PAD_DK, (j + 1) * PAD_DK)
        q_ref[:, sl] = (_rope_lanes(q[:, sl], cos, sin, D_ROPE // 2, lo) * scale).astype(BF16)
        k_ref[:, sl] = (kn[:, sl] + kpe).astype(BF16)
    vt_ref[0] = lax.dot_general(wvt_ref[...], ckvn, _NT, preferred_element_type=F32).astype(BF16)


def _odd_in(x, g, w_in, gq, gkv, wuq, wk, wvt, cos, sin):
    tm = TOKEN_TILE
    per_seq = SEQ // tm
    tok = lambda w: pl.BlockSpec((tm, w), lambda i: (i, 0))
    pos = pl.BlockSpec((tm, LANES), lambda i: (i % per_seq, 0))
    sds = lambda w, dt: jax.ShapeDtypeStruct((TOKENS, w), dt)
    qw = D_HEADS * PAD_DK
    vw = D_HEADS * D_VDIM
    return pl.pallas_call(
        _odd_in_kernel,
        grid=(TOKENS // tm,),
        in_specs=[tok(D_MODEL)] + [_resident(a.shape) for a in (g, w_in, gq, gkv, wuq, wk, wvt)] + [pos, pos],
        out_specs=[tok(C_WIDTH), tok(C_WIDTH), tok(qw), tok(qw),
                   pl.BlockSpec((1, vw, tm), lambda i: (i // per_seq, 0, i % per_seq))],
        out_shape=[sds(C_WIDTH, F32), sds(C_WIDTH, F32), sds(qw, BF16), sds(qw, BF16),
                   jax.ShapeDtypeStruct((BATCH, vw, SEQ), BF16)],
        compiler_params=_params("parallel"),
        name="odd_in",
    )(x, g, w_in, gq, gkv, wuq, wk, wvt, cos, sin)


def _rglru_kernel(xc_ref, gc_ref, cw_ref, cb_ref, wgate_ref, bgate_ref, lam_ref, o_ref,
                  a_scr, b_scr, h_scr):
    cw = cw_ref[...]
    x = xc_ref[...]
    row = lax.broadcasted_iota(jnp.int32, x.shape, 0)

    def shifted(d):
        rolled = pltpu.roll(x, (-d) % SEQ, 0)
        valid = (row + d >= 0) & (row + d < SEQ)
        return jnp.where(valid, rolled, 0.0)

    lp = (CONV_W - 1) // 2
    u = cb_ref[...] + sum((x if j == lp else shifted(j - lp)) * cw[j:j + 1] for j in range(CONV_W))
    gates = _dot(u.astype(BF16), wgate_ref[0]) + bgate_ref[0]
    lam = lam_ref[0]
    for d in range(2):
        o = 2 * d * LANES
        r_t = _sigmoid(gates[:, o:o + LANES])
        i_t = _sigmoid(gates[:, o + LANES:o + 2 * LANES])
        log_a = (-RG_C * r_t) * jax.nn.softplus(-lam[:, d * LANES:(d + 1) * LANES])
        a = jnp.exp(log_a)
        y = -jnp.tanh(log_a) * (a * a + 1.0)
        mult = jnp.where(y > 0.0, y * lax.rsqrt(y), 0.0)
        mult = jnp.where(row == (SEQ - 1 if d else 0), 1.0, mult)
        a_scr[d] = a
        b_scr[d] = mult * i_t * u

    n_tiles = SEQ // SUBLANES
    srow = lax.broadcasted_iota(jnp.int32, (SUBLANES, LANES), 0)

    def tile_scan(a, b, h_prev, reverse):
        for d in (1, 2, 4):
            if reverse:
                keep = srow < SUBLANES - d
                shift = SUBLANES - d
            else:
                keep = srow >= d
                shift = d
            a_s = jnp.where(keep, pltpu.roll(a, shift, 0), 1.0)
            b_s = jnp.where(keep, pltpu.roll(b, shift, 0), 0.0)
            b = a * b_s + b
            a = a * a_s
        return a * h_prev + b

    def step(i, carry):
        hf, hr = carry
        tf = pl.multiple_of(i * SUBLANES, SUBLANES)
        h = tile_scan(a_scr[0, pl.ds(tf, SUBLANES), :], b_scr[0, pl.ds(tf, SUBLANES), :], hf, False)
        h_scr[0, pl.ds(tf, SUBLANES), :] = h
        hf = jnp.broadcast_to(h[SUBLANES - 1:SUBLANES, :], h.shape)
        tr = pl.multiple_of((n_tiles - 1 - i) * SUBLANES, SUBLANES)
        h = tile_scan(a_scr[1, pl.ds(tr, SUBLANES), :], b_scr[1, pl.ds(tr, SUBLANES), :], hr, True)
        h_scr[1, pl.ds(tr, SUBLANES), :] = h
        hr = jnp.broadcast_to(h[0:1, :], h.shape)
        return hf, hr

    z = jnp.zeros((SUBLANES, LANES), F32)
    lax.fori_loop(0, n_tiles, step, (z, z), unroll=SCAN_UNROLL)
    o_ref[...] = (jax.nn.gelu(gc_ref[...]) * (h_scr[0] + h_scr[1])).astype(BF16)


def _rglru(xc, gc, conv_w, conv_b, wgate, bgate, lam):
    ng = C_WIDTH // LANES
    seq = pl.BlockSpec((SEQ, LANES), lambda b, g: (b, g))
    grp = lambda a: pl.BlockSpec((1,) + a.shape[1:], lambda b, g: (g, 0, 0))
    return pl.pallas_call(
        _rglru_kernel,
        grid=(BATCH, ng),
        in_specs=[seq, seq,
                  pl.BlockSpec((CONV_W, LANES), lambda b, g: (0, g)),
                  pl.BlockSpec((1, LANES), lambda b, g: (0, g)),
                  grp(wgate), grp(bgate), grp(lam)],
        out_specs=seq,
        out_shape=jax.ShapeDtypeStruct((TOKENS, C_WIDTH), BF16),
        scratch_shapes=[pltpu.VMEM((2, SEQ, LANES), F32)] * 3,
        compiler_params=_params("parallel", "parallel"),
        name="rglru",
    )(xc, gc, conv_w, conv_b, wgate, bgate, lam)


def _rglru_gate_params(rg_wa, rg_ba, rg_wx, rg_bx, rg_lam):
    ng = C_WIDTH // LANES
    per = LANES // C_BW

    def dense(w):
        w = w.reshape(ng, per, C_BW, C_BW)
        eye = jnp.eye(per, dtype=w.dtype)
        return jnp.einsum('gpcd,pq->gpcqd', w, eye).reshape(ng, LANES, LANES)

    wgate = jnp.concatenate([dense(rg_wa[0]), dense(rg_wx[0]), dense(rg_wa[1]), dense(rg_wx[1])], axis=-1)
    grp = lambda v: v.reshape(ng, 1, LANES)
    bgate = jnp.concatenate([grp(rg_ba[0]), grp(rg_bx[0]), grp(rg_ba[1]), grp(rg_bx[1])], axis=-1)
    lam = jnp.concatenate([grp(rg_lam[0]), grp(rg_lam[1])], axis=-1)
    return wgate.astype(BF16), bgate.astype(F32), lam.astype(F32)


def _mla_finish(result):
    (acc1, l1), (acc2, l2) = result
    return jnp.concatenate([acc1 / l1, acc2 / l2], axis=0).T


def _mla_attn(q, k, v):
    return _stream_attn(q, k, v, (), _mla_finish, groups=D_HEADS // 2, masked=False, q_tile=2 * PV_TILE,
                        key_chunk=2048, name="mla_attn")


def _rope_angles(dim):
    inv = 1.0 / (ROPE_THETA ** (jnp.arange(0, dim, 2, dtype=F32) / dim))
    ang = jnp.arange(SEQ, dtype=F32)[:, None] * inv[None, :]
    return jnp.cos(ang), jnp.sin(ang)


def _even_rope_tables():
    cos, sin = _rope_angles(A_DH)
    reps = LANES // A_DH
    return (jnp.tile(jnp.concatenate([cos, cos], -1), (1, reps)),
            jnp.tile(jnp.concatenate([-sin, sin], -1), (1, reps)))


def _odd_rope_tables():
    cos, sin = _rope_angles(D_ROPE)
    ones = jnp.ones((SEQ, D_NOPE), F32)
    zn = jnp.zeros((SEQ, D_NOPE), F32)
    zp = jnp.zeros((SEQ, PAD_DK - D_NOPE - D_ROPE), F32)
    return (jnp.concatenate([ones, cos, cos, zp], -1),
            jnp.concatenate([zn, -sin, sin, zp], -1))


def _odd_weights(w_in, wuq, wukv):
    base = 2 * C_WIDTH + D_Q_RANK + D_KV_RANK
    zl = jnp.zeros((D_MODEL, D_NOPE), w_in.dtype)
    zr = jnp.zeros((D_MODEL, PAD_DK - D_NOPE - D_ROPE), w_in.dtype)
    w_in_p = jnp.concatenate([w_in[:, :base], zl, w_in[:, base:], zr], axis=-1)
    dqk = D_NOPE + D_ROPE
    wuq_p = jnp.pad(wuq.reshape(D_Q_RANK, D_HEADS, dqk), ((0, 0), (0, 0), (0, PAD_DK - dqk)))
    wukv_h = wukv.reshape(D_KV_RANK, D_HEADS, D_NOPE + D_VDIM)
    wk_p = jnp.pad(wukv_h[:, :, :D_NOPE], ((0, 0), (0, 0), (0, PAD_DK - D_NOPE)))
    wv = wukv_h[:, :, D_NOPE:]
    return (w_in_p.astype(BF16), wuq_p.reshape(D_Q_RANK, -1).astype(BF16),
            wk_p.reshape(D_KV_RANK, -1).astype(BF16), wv.reshape(D_KV_RANK, -1).T.astype(BF16))


def kernel(x, ffn1_norm, ffn1_wg, ffn1_wu, ffn1_wd, mix_norm, ffn2_norm, ffn2_wg, ffn2_wu, ffn2_wd, final_norm, ev_w_in, ev_w_out, diff_lam, diff_subln, na_rpb, od_w_in, od_w_out, conv_w, conv_b, rg_wa, rg_ba, rg_wx, rg_bx, rg_lam, mla_gq, mla_gkv, mla_wuq, mla_wukv):
    row = lambda v: v.reshape(1, -1).astype(F32)
    bf = lambda w: w.astype(BF16)
    xt = x.reshape(TOKENS, D_MODEL)

    ffn1 = (bf(ffn1_wg), bf(ffn1_wu), bf(ffn1_wd))
    ffn2 = (bf(ffn2_wg), bf(ffn2_wu), bf(ffn2_wd))

    xt = _ffn(xt, row(ffn1_norm[0]), *ffn1, 0)
    cos_a, sin_a = _even_rope_tables()
    wv_cols = jnp.concatenate([ev_w_in[0][:, 2 * A_QK:2 * A_QK + A_V], ev_w_in[0][:, EVEN_IN - B_W:]], axis=1)
    qa, ka, vt, qb, kb = _even_in(xt, row(mix_norm[0]), bf(ev_w_in[0]), bf(wv_cols.T), cos_a, sin_a)
    lam_init0 = 0.8 - 0.6 * math.exp(-0.3 * 0)
    oa = _diff_attn(qa, ka, vt, diff_lam[0].astype(F32), row(diff_subln[0]), lam_init0)
    ob = _na_attn(qb, kb, vt, _na_bias_table(na_rpb[0]))
    xt = _ffn(xt, row(ffn2_norm[0]), *ffn2, 0, mix=(oa, ob, bf(ev_w_out[0])))

    xt = _ffn(xt, row(ffn1_norm[1]), *ffn1, 1)
    cos_d, sin_d = _odd_rope_tables()
    w_in_p, wuq_p, wk_p, wv_p = _odd_weights(od_w_in[0], mla_wuq[0], mla_wukv[0])
    xc, gc, q, k, v = _odd_in(xt, row(mix_norm[1]), w_in_p, row(mla_gq[0]), row(mla_gkv[0]),
                              wuq_p, wk_p, wv_p, cos_d, sin_d)
    wgate, bgate, lam = _rglru_gate_params(rg_wa[0], rg_ba[0], rg_wx[0], rg_bx[0], rg_lam[0])
    oc = _rglru(xc, gc, conv_w[0].astype(F32), row(conv_b[0]), wgate, bgate, lam)
    od = _mla_attn(q, k, v)
    xt = _ffn(xt, row(ffn2_norm[1]), *ffn2, 1, mix=(oc, od, bf(od_w_out[0])), final_g=row(final_norm))
    return xt.reshape(BATCH, SEQ, D_MODEL)
```

```python
import functools
import itertools
import math

import jax
import jax.numpy as jnp
import numpy as np
from jax import lax
from jax.experimental import pallas as pl
from jax.experimental.pallas import tpu as pltpu

F32 = jnp.float32
BF16 = jnp.bfloat16

D_MODEL = 1024
BATCH = 4
SEQ = 4096
DEPTH = 2
TOKENS = BATCH * SEQ
RMS_EPS = 1e-6
ROPE_THETA = 10000.0
GRID_W = 64
GRID_ROWS = SEQ // GRID_W
D_FF = 2816

A_HEADS = 4
A_DH = 64
A_QK = A_HEADS * 2 * A_DH
A_V = A_HEADS * 2 * A_DH
B_HEADS = 8
B_DH = 64
B_W = B_HEADS * B_DH
NA_ROWS = 8
NA_COLS = 16
C_WIDTH = 512
C_BLOCKS = 8
C_BW = C_WIDTH // C_BLOCKS
CONV_W = 4
RG_C = 8.0
D_HEADS = 8
D_NOPE = 64
D_ROPE = 32
D_VDIM = 64
D_Q_RANK = 256
D_KV_RANK = 128
EVEN_IN = 2 * A_QK + A_V + 3 * B_W
ODD_IN = 2 * C_WIDTH + D_Q_RANK + D_KV_RANK + D_ROPE

LANES = 128
SUBLANES = 8
VMEM_LIMIT_BYTES = 56 * 1024 * 1024

TOKEN_TILE = 1024
FFN_TILE = 1024
FF_CHUNK = 256
PV_TILE = 256
LOGIT_CHUNK = 1024
PAD_DK = 128
SCAN_UNROLL = 16

_NT = (((1,), (1,)), ((), ()))

LOG2E = math.log2(math.e)


def _params(*sem):
    return pltpu.CompilerParams(dimension_semantics=sem, vmem_limit_bytes=VMEM_LIMIT_BYTES)


def _resident(shape):
    nd = len(shape)
    return pl.BlockSpec(shape, lambda *_: (0,) * nd, pipeline_mode=pl.Buffered(1))


def _rms(x, g):
    return x * lax.rsqrt(jnp.mean(x * x, axis=-1, keepdims=True) + RMS_EPS) * g


def _dot(a, b):
    return jnp.dot(a, b, preferred_element_type=F32)


def _sigmoid(x):
    return 0.5 * (jnp.tanh(0.5 * x) + 1.0)


def _rope_lanes(x, cos, sin_signed, half, lo_mask):
    n = x.shape[-1]
    partner = jnp.where(lo_mask, pltpu.roll(x, n - half, 1), pltpu.roll(x, half, 1))
    return x * cos + partner * sin_signed


def _ffn_kernel(*refs, has_mix, has_final):
    it = iter(refs)
    x_ref = next(it)
    if has_mix:
        o1_ref, o2_ref, wo_ref = next(it), next(it), next(it)
    g_ref, wg_ref, wu_ref, wd_ref = next(it), next(it), next(it), next(it)
    gf_ref = next(it) if has_final else None
    y_ref = next(it)
    acc_ref = next(it)

    x = x_ref[...]
    if has_mix:
        half = wo_ref.shape[0] // 2
        x = x + _dot(o1_ref[...], wo_ref[:half, :]) + _dot(o2_ref[...], wo_ref[half:, :])
    n = _rms(x, g_ref[...]).astype(BF16)
    for c in range(D_FF // FF_CHUNK):
        sl = slice(c * FF_CHUNK, (c + 1) * FF_CHUNK)
        gate = _dot(n, wg_ref[:, sl])
        up = _dot(n, wu_ref[:, sl])
        h = (gate * jax.nn.sigmoid(gate) * up).astype(BF16)
        d = _dot(h, wd_ref[sl, :])
        if c == 0:
            acc_ref[...] = d
        else:
            acc_ref[...] += d
    y = x + 0.5 * acc_ref[...]
    if has_final:
        y = _rms(y, gf_ref[...])
    y_ref[...] = y


def _ffn(x, g, wg, wu, wd, layer, mix=None, final_g=None):
    tm = FFN_TILE
    tok = lambda w: pl.BlockSpec((tm, w), lambda i: (i, 0))
    slab = lambda a: pl.BlockSpec((None,) + a.shape[1:], lambda i: (layer, 0, 0),
                                  pipeline_mode=pl.Buffered(1))
    args, specs = [x], [tok(D_MODEL)]
    if mix is not None:
        o1, o2, wo = mix
        args += [o1, o2, wo]
        specs += [tok(o1.shape[1]), tok(o2.shape[1]), _resident(wo.shape)]
    args += [g, wg, wu, wd]
    specs += [_resident(g.shape), slab(wg), slab(wu), slab(wd)]
    if final_g is not None:
        args.append(final_g)
        specs.append(_resident(final_g.shape))
    return pl.pallas_call(
        functools.partial(_ffn_kernel, has_mix=mix is not None, has_final=final_g is not None),
        grid=(TOKENS // tm,),
        in_specs=specs,
        out_specs=tok(D_MODEL),
        out_shape=jax.ShapeDtypeStruct((TOKENS, D_MODEL), F32),
        scratch_shapes=[pltpu.VMEM((tm, D_MODEL), F32)],
        compiler_params=_params("parallel"),
        name="ffn_mix" if mix is not None else "ffn",
    )(*args)


def _even_in_kernel(x_ref, g_ref, w_ref, wvt_ref, cos_ref, sin_ref,
                    qa_ref, ka_ref, vt_ref, qb_ref, kb_ref):
    h = _rms(x_ref[...], g_ref[...]).astype(BF16)
    cos, sin = cos_ref[...], sin_ref[...]
    lane = lax.broadcasted_iota(jnp.int32, cos.shape, 1)
    lo = (lane % A_DH) < (A_DH // 2)
    scale = A_DH ** -0.5 * LOG2E

    def proj(i):
        return _dot(h, w_ref[:, i * A_QK:(i + 1) * A_QK])

    qa, ka = proj(0), proj(1)
    for j in range(A_QK // LANES):
        sl = slice(j * LANES, (j + 1) * LANES)
        qa_ref[:, sl] = (_rope_lanes(qa[:, sl], cos, sin, A_DH // 2, lo) * scale).astype(BF16)
        ka_ref[:, sl] = _rope_lanes(ka[:, sl], cos, sin, A_DH // 2, lo).astype(BF16)
    vt_ref[0] = lax.dot_general(wvt_ref[...], h, _NT, preferred_element_type=F32).astype(BF16)
    qb_ref[...] = (proj(3) * (B_DH ** -0.5 * LOG2E)).astype(BF16)
    kb_ref[...] = proj(4).astype(BF16)


def _even_in(x, g, w_in, wvt, cos, sin):
    tm = TOKEN_TILE
    per_seq = SEQ // tm
    tok = lambda w: pl.BlockSpec((tm, w), lambda i: (i, 0))
    pos = pl.BlockSpec((tm, LANES), lambda i: (i % per_seq, 0))
    vw = wvt.shape[0]
    tr = pl.BlockSpec((1, vw, tm), lambda i: (i // per_seq, 0, i % per_seq))
    out = jax.ShapeDtypeStruct((TOKENS, A_QK), BF16)
    out_t = jax.ShapeDtypeStruct((BATCH, vw, SEQ), BF16)
    return pl.pallas_call(
        _even_in_kernel,
        grid=(TOKENS // tm,),
        in_specs=[tok(D_MODEL), _resident(g.shape), _resident(w_in.shape), _resident(wvt.shape), pos, pos],
        out_specs=[tok(A_QK), tok(A_QK), tr, tok(B_W), tok(B_W)],
        out_shape=[out, out, out_t, out, out],
        compiler_params=_params("parallel"),
        name="even_in",
    )(x, g, w_in, wvt, cos, sin)


def _logit_chunks(q, k_ref, s_buf, m_buf):
    tq, qw = q.shape
    lane = lax.broadcasted_iota(jnp.int32, q.shape, 1)
    zero = jnp.zeros_like(q)
    q2 = jnp.concatenate([jnp.where(lane < qw // 2, q, zero), jnp.where(lane >= qw // 2, q, zero)], axis=0)
    groups8 = LOGIT_CHUNK // SUBLANES
    mrun = [None, None]
    for c in range(SEQ // LOGIT_CHUNK):
        keys = slice(c * LOGIT_CHUNK, (c + 1) * LOGIT_CHUNK)
        s2 = lax.dot_general(k_ref[keys, :], q2, _NT, preferred_element_type=F32)
        for j in range(2):
            s = s2[:, j * tq:(j + 1) * tq]
            s_buf[j, keys, :] = s
            part = jnp.max(s.reshape(groups8, SUBLANES, tq), axis=0)
            mrun[j] = part if mrun[j] is None else jnp.maximum(mrun[j], part)
        yield
    for j in range(2):
        m_buf[j] = jnp.broadcast_to(jnp.max(mrun[j], axis=0, keepdims=True), (SUBLANES, tq))


def _softmax_pv_chunks(s_buf, m_buf, vt_ref, vrows, acc_scr, l_scr, key_chunk):
    tq = s_buf.shape[-1]
    groups8 = key_chunk // SUBLANES
    m = [m_buf[j][None] for j in range(2)]
    acc = [None, None]
    lrun = [None, None]
    for c in range(SEQ // key_chunk):
        keys = slice(c * key_chunk, (c + 1) * key_chunk)
        for j in range(2):
            e = jnp.exp2(s_buf[j, keys, :].reshape(groups8, SUBLANES, tq) - m[j])
            part = jnp.sum(e, axis=0)
            lrun[j] = part if lrun[j] is None else lrun[j] + part
            pv = _dot(vt_ref[0, vrows[j], keys], e.reshape(key_chunk, tq).astype(BF16))
            acc[j] = pv if acc[j] is None else acc[j] + pv
            yield
    for j in range(2):
        acc_scr[j, :acc[j].shape[0], :] = acc[j]
        l_scr[j] = lrun[j]


def _stream_attn_kernel(*refs, n_items, n_extra, side_items, masked, finish, key_chunk):
    qf_ref, qn_ref, kn_ref, vt_ref = refs[:4]
    extra = refs[4:4 + n_extra]
    s_a, m_a, s_b, m_b, acc_scr, l_scr = refs[-6:]
    if side_items:
        side_q, side_k, side_vt, side_bias = refs[4 + n_extra:8 + n_extra]
        o_ref, side_o = refs[-8:-6]
    else:
        o_ref = refs[-7]
    t = pl.program_id(0)
    dv = vt_ref.shape[1]
    vrows = [slice(None)] * 2 if masked else [slice(0, dv // 2), slice(dv // 2, dv)]
    rows = dv if masked else dv // 2

    n_sub = s_a.shape[0]

    def logit_items(q_ref, s_buf, m_buf):
        for sub in range(n_sub):
            q = q_ref[sub * PV_TILE:(sub + 1) * PV_TILE, :]
            yield from _logit_chunks(q, kn_ref, s_buf.at[sub], m_buf.at[sub])

    def softmax_items(s_buf, m_buf):
        for sub in range(n_sub):
            yield from _softmax_pv_chunks(s_buf.at[sub], m_buf.at[sub], vt_ref, vrows,
                                          acc_scr.at[sub], l_scr.at[sub], key_chunk)

    @pl.when(t == 0)
    def _():
        for _ in logit_items(qf_ref, s_a, m_a):
            pass
        acc_scr[...] = jnp.zeros_like(acc_scr)
        l_scr[...] = jnp.ones_like(l_scr)

    def finish_previous():
        for sub in range(n_sub):
            result = [(acc_scr[sub, j, :rows, :], jnp.sum(l_scr[sub, j], axis=0, keepdims=True))
                      for j in range(2)]
            o_ref[sub * PV_TILE:(sub + 1) * PV_TILE, :] = finish(result, *extra).astype(o_ref.dtype)

    def side_stages():
        for sub in range(n_sub if side_items else 0):
            rows_ = slice(sub * PV_TILE, (sub + 1) * PV_TILE)
            block = (lax.rem(t, side_items // n_sub)) * n_sub + sub

            def store(o, rows_=rows_):
                side_o[rows_, :] = o

            yield from _na_block_stages(side_q[rows_, :], side_k, side_vt, side_bias, block, store)

    def step(nxt, cur):
        finish_previous()
        logits = logit_items(qn_ref, nxt[0], nxt[1])
        softmax = softmax_items(cur[0], cur[1])
        side = side_stages()
        ratio = max((SEQ // key_chunk) * 2 // (SEQ // LOGIT_CHUNK), 1)
        for i, _ in enumerate(softmax):
            next(side, None)
            if i % ratio == 0:
                next(logits, None)
        for _ in logits:
            pass
        for _ in side:
            pass

    live = t < n_items
    pl.when(live & ((t & 1) == 0))(lambda: step((s_b, m_b), (s_a, m_a)))
    pl.when(live & ((t & 1) == 1))(lambda: step((s_a, m_a), (s_b, m_b)))
    pl.when(t == n_items)(finish_previous)


def _stream_attn(q, k, vt, extra, finish, *, groups, masked, q_tile, key_chunk, name, side=None):
    tq = q_tile
    n_sub = tq // PV_TILE
    nq = SEQ // tq
    n_items = BATCH * groups * nq
    qw = q.shape[1] // groups

    def rows(t):
        return (t // (groups * nq)) * nq + t % nq

    def group(t):
        return (t // nq) % groups

    def batch(t):
        return t // (groups * nq)

    nxt = lambda t: jnp.minimum(t + 1, n_items - 1)
    cur = lambda t: jnp.minimum(t, n_items - 1)
    prv = lambda t: jnp.maximum(t - 1, 0)
    in_specs = [
        pl.BlockSpec((tq, qw), lambda t: (0, 0)),
        pl.BlockSpec((tq, qw), lambda t: (rows(nxt(t)), group(nxt(t)))),
        pl.BlockSpec((SEQ, qw), lambda t: (batch(nxt(t)), group(nxt(t)))),
        pl.BlockSpec((1, LANES, SEQ), lambda t: (batch(cur(t)), group(cur(t)), 0)),
    ] + [_resident(a.shape) for a in extra]
    args = [q, q, k, vt, *extra]
    out_sds = jax.ShapeDtypeStruct((TOKENS, groups * LANES), BF16)
    out_specs = pl.BlockSpec((tq, LANES), lambda t: (rows(prv(t)), group(prv(t))))
    out_shape = out_sds
    side_items = 0
    if side is not None:
        side_q, side_k, side_bias = side
        side_items = nq * n_sub
        assert side_items == NA_BLOCKS and PV_TILE == NA_QROWS * GRID_W and side_q.shape[1] == groups * LANES
        in_specs += [
            pl.BlockSpec((tq, LANES), lambda t: (rows(cur(t)), group(cur(t)))),
            pl.BlockSpec((SEQ, LANES), lambda t: (batch(cur(t)), group(cur(t)))),
            pl.BlockSpec((1, LANES, SEQ), lambda t: (batch(cur(t)), groups + group(cur(t)), 0)),
            pl.BlockSpec((2,) + side_bias.shape[1:], lambda t: (group(cur(t)), 0, 0, 0)),
        ]
        args += [side_q, side_k, vt, side_bias]
        out_specs = [out_specs, pl.BlockSpec((tq, LANES), lambda t: (rows(cur(t)), group(cur(t))))]
        out_shape = [out_sds, out_sds]
    sbuf = pltpu.VMEM((n_sub, 2, SEQ, PV_TILE), F32)
    mbuf = pltpu.VMEM((n_sub, 2, SUBLANES, PV_TILE), F32)
    return pl.pallas_call(
        functools.partial(_stream_attn_kernel, n_items=n_items, n_extra=len(extra), side_items=side_items,
                          masked=masked, finish=finish, key_chunk=key_chunk),
        grid=(n_items + 1,),
        in_specs=in_specs,
        out_specs=out_specs,
        out_shape=out_shape,
        scratch_shapes=[sbuf, mbuf, sbuf, mbuf,
                        pltpu.VMEM((n_sub, 2, LANES, PV_TILE), F32),
                        pltpu.VMEM((n_sub, 2, SUBLANES, PV_TILE), F32)],
        compiler_params=_params("arbitrary"),
        name=name,
    )(*args)


def _diff_finish(result, lam_ref, g_ref, *, lam_init):
    (acc1, l1), (acc2, l2) = result
    lf = lam_ref[...]
    lam = (jnp.exp(jnp.sum(lf[0:1] * lf[1:2], axis=-1, keepdims=True))
           - jnp.exp(jnp.sum(lf[2:3] * lf[3:4], axis=-1, keepdims=True)) + lam_init)
    o = (acc1 / l1 - lam * (acc2 / l2)).T
    return _rms(o, g_ref[...]) * (1.0 - lam_init)


def _diff_na_attn(qa, ka, vt, lam_vec, subln_g, lam_init, qb, kb, na_bias):
    return _stream_attn(qa, ka, vt, (lam_vec, subln_g),
                        functools.partial(_diff_finish, lam_init=lam_init),
                        groups=A_HEADS, masked=True, q_tile=2 * PV_TILE, key_chunk=1024, name="diff_na_attn",
                        side=(qb, kb, na_bias))


NA_QROWS = 4
NA_KROWS = NA_QROWS + NA_ROWS
NA_BLOCKS = GRID_ROWS // NA_QROWS
NA_PLACEMENTS = 3


def _na_window_start(g):
    lo, hi = 0, GRID_ROWS - NA_KROWS
    s = g * NA_QROWS - NA_ROWS // 2
    if isinstance(g, int):
        return min(max(s, lo), hi)
    return jnp.clip(s, lo, hi)


def _na_block_stages(q, k_ref, vt_ref, bias_ref, g, store):
    nq = NA_QROWS * GRID_W
    nk = NA_KROWS * GRID_W
    groups8 = nk // SUBLANES
    lane = lax.broadcasted_iota(jnp.int32, q.shape, 1)
    zero = jnp.zeros_like(q)
    var = jnp.where(g == 0, 0, jnp.where(g == NA_BLOCKS - 1, 2, 1))
    kstart = pl.multiple_of(_na_window_start(g) * GRID_W, NA_QROWS * GRID_W)
    q2 = jnp.concatenate([jnp.where(lane < B_DH, q, zero), jnp.where(lane >= B_DH, q, zero)], axis=0)
    s2 = lax.dot_general(k_ref[pl.ds(kstart, nk), :], q2, _NT, preferred_element_type=F32)
    yield
    outs = []
    for j in range(2):
        s = (s2[:, j * nq:(j + 1) * nq] + bias_ref[j, var]).reshape(groups8, SUBLANES, nq)
        m = jnp.max(jnp.max(s, axis=0), axis=0, keepdims=True)
        yield
        e = jnp.exp2(s - m[None])
        l = jnp.sum(jnp.sum(e, axis=0), axis=0, keepdims=True)
        yield
        vt = vt_ref[0, j * B_DH:(j + 1) * B_DH, pl.ds(kstart, nk)]
        outs.append(_dot(vt, e.reshape(nk, nq).astype(BF16)) / l)
        yield
    store(jnp.concatenate(outs, axis=0).T.astype(BF16))


def _na_bias_table(rpb):
    c = np.arange(GRID_W)
    c0 = np.clip(c - NA_COLS // 2, 0, GRID_W - NA_COLS)
    col_in = (c[None, :] >= c0[:, None]) & (c[None, :] < c0[:, None] + NA_COLS)
    dc = np.clip(c[None, :] - c[:, None], -(NA_COLS - 1), NA_COLS - 1) + NA_COLS - 1
    rpb = rpb.astype(F32) * LOG2E
    n_dr = 2 * NA_ROWS - 1
    toe = jnp.zeros((B_HEADS, GRID_W, n_dr, GRID_W), F32)
    for d in range(2 * NA_COLS - 1):
        toe = jnp.where((dc.T == d)[:, None, :], rpb[:, None, ::-1, d, None], toe)
    toe = jnp.where(col_in.T[:, None, :], toe, -jnp.inf).reshape(B_HEADS, GRID_W, n_dr * GRID_W)
    masked = lambda n: [jnp.full((B_HEADS, GRID_W, n * GRID_W), -jnp.inf, F32)] if n else []
    first_key_row = lambda qr: min(max(qr - NA_ROWS // 2, 0), GRID_ROWS - NA_ROWS)
    slabs = []
    for g in (0, 1, NA_BLOCKS - 1):
        for kl in range(NA_KROWS):
            kr = _na_window_start(g) + kl
            qrs = [g * NA_QROWS + ql for ql in range(NA_QROWS)]
            ok = [ql for ql, qr in enumerate(qrs) if first_key_row(qr) <= kr < first_key_row(qr) + NA_ROWS]
            if not ok:
                slabs.append(masked(NA_QROWS)[0])
                continue
            lo, hi = ok[0], ok[-1]
            assert ok == list(range(lo, hi + 1))
            first = n_dr - 1 - (kr - qrs[lo] + NA_ROWS - 1)
            own = toe[:, :, first * GRID_W:(first + hi - lo + 1) * GRID_W]
            slabs.append(jnp.concatenate(masked(lo) + [own] + masked(NA_QROWS - 1 - hi), axis=-1))
    return jnp.stack(slabs, axis=1).reshape(
        B_HEADS, NA_PLACEMENTS, NA_KROWS * GRID_W, NA_QROWS * GRID_W)


def _odd_in_kernel(x_ref, g_ref, w_ref, gq_ref, gkv_ref, wuq_ref, wk_ref, wvt_ref,
                   cos_ref, sin_ref, xc_ref, gc_ref, q_ref, k_ref, vt_ref):
    h = _rms(x_ref[...], g_ref[...]).astype(BF16)
    cos, sin = cos_ref[...], sin_ref[...]
    lane = lax.broadcasted_iota(jnp.int32, cos.shape, 1)
    lo = lane < D_NOPE + D_ROPE // 2
    scale = (D_NOPE + D_ROPE) ** -0.5 * LOG2E
    o = 0
    xc_ref[...] = _dot(h, w_ref[:, o:o + C_WIDTH]); o += C_WIDTH
    gc_ref[...] = _dot(h, w_ref[:, o:o + C_WIDTH]); o += C_WIDTH
    cq = _dot(h, w_ref[:, o:o + D_Q_RANK]); o += D_Q_RANK
    ckv = _dot(h, w_ref[:, o:o + D_KV_RANK]); o += D_KV_RANK
    kr = _dot(h, w_ref[:, o:o + PAD_DK])
    q = _dot(_rms(cq, gq_ref[...]).astype(BF16), wuq_ref[...])
    ckvn = _rms(ckv, gkv_ref[...]).astype(BF16)
    kn = _dot(ckvn, wk_ref[...])
    kpe = _rope_lanes(kr, cos, sin, D_ROPE // 2, lo)
    for j in range(D_HEADS):
        sl = slice(j * PAD_DK, (j + 1) * PAD_DK)
        q_ref[:, sl] = (_rope_lanes(q[:, sl], cos, sin, D_ROPE // 2, lo) * scale).astype(BF16)
        k_ref[:, sl] = (kn[:, sl] + kpe).astype(BF16)
    vt_ref[0] = lax.dot_general(wvt_ref[...], ckvn, _NT, preferred_element_type=F32).astype(BF16)


def _odd_in(x, g, w_in, gq, gkv, wuq, wk, wvt, cos, sin):
    tm = TOKEN_TILE
    per_seq = SEQ // tm
    tok = lambda w: pl.BlockSpec((tm, w), lambda i: (i, 0))
    pos = pl.BlockSpec((tm, LANES), lambda i: (i % per_seq, 0))
    sds = lambda w, dt: jax.ShapeDtypeStruct((TOKENS, w), dt)
    qw = D_HEADS * PAD_DK
    vw = D_HEADS * D_VDIM
    return pl.pallas_call(
        _odd_in_kernel,
        grid=(TOKENS // tm,),
        in_specs=[tok(D_MODEL)] + [_resident(a.shape) for a in (g, w_in, gq, gkv, wuq, wk, wvt)] + [pos, pos],
        out_specs=[tok(C_WIDTH), tok(C_WIDTH), tok(qw), tok(qw),
                   pl.BlockSpec((1, vw, tm), lambda i: (i // per_seq, 0, i % per_seq))],
        out_shape=[sds(C_WIDTH, F32), sds(C_WIDTH, F32), sds(qw, BF16), sds(qw, BF16),
                   jax.ShapeDtypeStruct((BATCH, vw, SEQ), BF16)],
        compiler_params=_params("parallel"),
        name="odd_in",
    )(x, g, w_in, gq, gkv, wuq, wk, wvt, cos, sin)


def _rglru_kernel(xc_ref, gc_ref, cw_ref, cb_ref, wgate_ref, bgate_ref, lam_ref, o_ref,
                  a_scr, b_scr, h_scr):
    cw = cw_ref[...]
    x = xc_ref[...]
    row = lax.broadcasted_iota(jnp.int32, x.shape, 0)

    def shifted(d):
        rolled = pltpu.roll(x, (-d) % SEQ, 0)
        valid = (row + d >= 0) & (row + d < SEQ)
        return jnp.where(valid, rolled, 0.0)

    lp = (CONV_W - 1) // 2
    u = cb_ref[...] + sum((x if j == lp else shifted(j - lp)) * cw[j:j + 1] for j in range(CONV_W))
    gates = _dot(u.astype(BF16), wgate_ref[0]) + bgate_ref[0]
    lam = lam_ref[0]
    for d in range(2):
        o = 2 * d * LANES
        r_t = _sigmoid(gates[:, o:o + LANES])
        i_t = _sigmoid(gates[:, o + LANES:o + 2 * LANES])
        log_a = (-RG_C * r_t) * jax.nn.softplus(-lam[:, d * LANES:(d + 1) * LANES])
        a = jnp.exp(log_a)
        y = -jnp.tanh(log_a) * (a * a + 1.0)
        mult = jnp.where(y > 0.0, y * lax.rsqrt(y), 0.0)
        mult = jnp.where(row == (SEQ - 1 if d else 0), 1.0, mult)
        a_scr[d] = a
        b_scr[d] = mult * i_t * u

    n_tiles = SEQ // SUBLANES
    srow = lax.broadcasted_iota(jnp.int32, (SUBLANES, LANES), 0)

    def tile_scan(a, b, h_prev, reverse):
        for d in (1, 2, 4):
            if reverse:
                keep = srow < SUBLANES - d
                shift = SUBLANES - d
            else:
                keep = srow >= d
                shift = d
            a_s = jnp.where(keep, pltpu.roll(a, shift, 0), 1.0)
            b_s = jnp.where(keep, pltpu.roll(b, shift, 0), 0.0)
            b = a * b_s + b
            a = a * a_s
        return a * h_prev + b

    def step(i, carry):
        hf, hr = carry
        tf = pl.multiple_of(i * SUBLANES, SUBLANES)
        h = tile_scan(a_scr[0, pl.ds(tf, SUBLANES), :], b_scr[0, pl.ds(tf, SUBLANES), :], hf, False)
        h_scr[0, pl.ds(tf, SUBLANES), :] = h
        hf = jnp.broadcast_to(h[SUBLANES - 1:SUBLANES, :], h.shape)
        tr = pl.multiple_of((n_tiles - 1 - i) * SUBLANES, SUBLANES)
        h = tile_scan(a_scr[1, pl.ds(tr, SUBLANES), :], b_scr[1, pl.ds(tr, SUBLANES), :], hr, True)
        h_scr[1, pl.ds(tr, SUBLANES), :] = h
        hr = jnp.broadcast_to(h[0:1, :], h.shape)
        return hf, hr

    z = jnp.zeros((SUBLANES, LANES), F32)
    lax.fori_loop(0, n_tiles, step, (z, z), unroll=SCAN_UNROLL)
    o_ref[...] = (jax.nn.gelu(gc_ref[...]) * (h_scr[0] + h_scr[1])).astype(BF16)


def _rglru(xc, gc, conv_w, conv_b, wgate, bgate, lam):
    ng = C_WIDTH // LANES
    seq = pl.BlockSpec((SEQ, LANES), lambda b, g: (b, g))
    grp = lambda a: pl.BlockSpec((1,) + a.shape[1:], lambda b, g: (g, 0, 0))
    return pl.pallas_call(
        _rglru_kernel,
        grid=(BATCH, ng),
        in_specs=[seq, seq,
                  pl.BlockSpec((CONV_W, LANES), lambda b, g: (0, g)),
                  pl.BlockSpec((1, LANES), lambda b, g: (0, g)),
                  grp(wgate), grp(bgate), grp(lam)],
        out_specs=seq,
        out_shape=jax.ShapeDtypeStruct((TOKENS, C_WIDTH), BF16),
        scratch_shapes=[pltpu.VMEM((2, SEQ, LANES), F32)] * 3,
        compiler_params=_params("parallel", "parallel"),
        name="rglru",
    )(xc, gc, conv_w, conv_b, wgate, bgate, lam)


def _rglru_gate_params(rg_wa, rg_ba, rg_wx, rg_bx, rg_lam):
    ng = C_WIDTH // LANES
    per = LANES // C_BW

    def dense(w):
        w = w.reshape(ng, per, C_BW, C_BW)
        eye = jnp.eye(per, dtype=w.dtype)
        return jnp.einsum('gpcd,pq->gpcqd', w, eye).reshape(ng, LANES, LANES)

    wgate = jnp.concatenate([dense(rg_wa[0]), dense(rg_wx[0]), dense(rg_wa[1]), dense(rg_wx[1])], axis=-1)
    grp = lambda v: v.reshape(ng, 1, LANES)
    bgate = jnp.concatenate([grp(rg_ba[0]), grp(rg_bx[0]), grp(rg_ba[1]), grp(rg_bx[1])], axis=-1)
    lam = jnp.concatenate([grp(rg_lam[0]), grp(rg_lam[1])], axis=-1)
    return wgate.astype(BF16), bgate.astype(F32), lam.astype(F32)


def _mla_finish(result):
    (acc1, l1), (acc2, l2) = result
    return jnp.concatenate([acc1 / l1, acc2 / l2], axis=0).T


def _mla_attn(q, k, v):
    return _stream_attn(q, k, v, (), _mla_finish, groups=D_HEADS // 2, masked=False, q_tile=2 * PV_TILE,
                        key_chunk=2048, name="mla_attn")


def _rope_angles(dim):
    inv = 1.0 / (ROPE_THETA ** (jnp.arange(0, dim, 2, dtype=F32) / dim))
    ang = jnp.arange(SEQ, dtype=F32)[:, None] * inv[None, :]
    return jnp.cos(ang), jnp.sin(ang)


def _even_rope_tables():
    cos, sin = _rope_angles(A_DH)
    reps = LANES // A_DH
    return (jnp.tile(jnp.concatenate([cos, cos], -1), (1, reps)),
            jnp.tile(jnp.concatenate([-sin, sin], -1), (1, reps)))


def _odd_rope_tables():
    cos, sin = _rope_angles(D_ROPE)
    ones = jnp.ones((SEQ, D_NOPE), F32)
    zn = jnp.zeros((SEQ, D_NOPE), F32)
    zp = jnp.zeros((SEQ, PAD_DK - D_NOPE - D_ROPE), F32)
    return (jnp.concatenate([ones, cos, cos, zp], -1),
            jnp.concatenate([zn, -sin, sin, zp], -1))


def _odd_weights(w_in, wuq, wukv):
    base = 2 * C_WIDTH + D_Q_RANK + D_KV_RANK
    zl = jnp.zeros((D_MODEL, D_NOPE), w_in.dtype)
    zr = jnp.zeros((D_MODEL, PAD_DK - D_NOPE - D_ROPE), w_in.dtype)
    w_in_p = jnp.concatenate([w_in[:, :base], zl, w_in[:, base:], zr], axis=-1)
    dqk = D_NOPE + D_ROPE
    wuq_p = jnp.pad(wuq.reshape(D_Q_RANK, D_HEADS, dqk), ((0, 0), (0, 0), (0, PAD_DK - dqk)))
    wukv_h = wukv.reshape(D_KV_RANK, D_HEADS, D_NOPE + D_VDIM)
    wk_p = jnp.pad(wukv_h[:, :, :D_NOPE], ((0, 0), (0, 0), (0, PAD_DK - D_NOPE)))
    wv = wukv_h[:, :, D_NOPE:]
    return (w_in_p.astype(BF16), wuq_p.reshape(D_Q_RANK, -1).astype(BF16),
            wk_p.reshape(D_KV_RANK, -1).astype(BF16), wv.reshape(D_KV_RANK, -1).T.astype(BF16))


def kernel(x, ffn1_norm, ffn1_wg, ffn1_wu, ffn1_wd, mix_norm, ffn2_norm, ffn2_wg, ffn2_wu, ffn2_wd, final_norm, ev_w_in, ev_w_out, diff_lam, diff_subln, na_rpb, od_w_in, od_w_out, conv_w, conv_b, rg_wa, rg_ba, rg_wx, rg_bx, rg_lam, mla_gq, mla_gkv, mla_wuq, mla_wukv):
    row = lambda v: v.reshape(1, -1).astype(F32)
    bf = lambda w: w.astype(BF16)
    xt = x.reshape(TOKENS, D_MODEL)

    ffn1 = (bf(ffn1_wg), bf(ffn1_wu), bf(ffn1_wd))
    ffn2 = (bf(ffn2_wg), bf(ffn2_wu), bf(ffn2_wd))

    xt = _ffn(xt, row(ffn1_norm[0]), *ffn1, 0)
    cos_a, sin_a = _even_rope_tables()
    wv_cols = jnp.concatenate([ev_w_in[0][:, 2 * A_QK:2 * A_QK + A_V], ev_w_in[0][:, EVEN_IN - B_W:]], axis=1)
    qa, ka, vt, qb, kb = _even_in(xt, row(mix_norm[0]), bf(ev_w_in[0]), bf(wv_cols.T), cos_a, sin_a)
    lam_init0 = 0.8 - 0.6 * math.exp(-0.3 * 0)
    oa, ob = _diff_na_attn(qa, ka, vt, diff_lam[0].astype(F32), row(diff_subln[0]), lam_init0,
                           qb, kb, _na_bias_table(na_rpb[0]))
    xt = _ffn(xt, row(ffn2_norm[0]), *ffn2, 0, mix=(oa, ob, bf(ev_w_out[0])))

    xt = _ffn(xt, row(ffn1_norm[1]), *ffn1, 1)
    cos_d, sin_d = _odd_rope_tables()
    w_in_p, wuq_p, wk_p, wv_p = _odd_weights(od_w_in[0], mla_wuq[0], mla_wukv[0])
    xc, gc, q, k, v = _odd_in(xt, row(mix_norm[1]), w_in_p, row(mla_gq[0]), row(mla_gkv[0]),
                              wuq_p, wk_p, wv_p, cos_d, sin_d)
    wgate, bgate, lam = _rglru_gate_params(rg_wa[0], rg_ba[0], rg_wx[0], rg_bx[0], rg_lam[0])
    oc = _rglru(xc, gc, conv_w[0].astype(F32), row(conv_b[0]), wgate, bgate, lam)
    od = _mla_attn(q, k, v)
    xt = _ffn(xt, row(ffn2_norm[1]), *ffn2, 1, mix=(oc, od, bf(od_w_out[0])), final_g=row(final_norm))
    return xt.reshape(BATCH, SEQ, D_MODEL)
```

```python
import functools
import math

import jax
import jax.numpy as jnp
import numpy as np
from jax import lax
from jax.experimental import pallas as pl
from jax.experimental.pallas import tpu as pltpu

F32 = jnp.float32
BF16 = jnp.bfloat16

D_MODEL = 1024
BATCH = 4
SEQ = 4096
DEPTH = 2
TOKENS = BATCH * SEQ
RMS_EPS = 1e-6
ROPE_THETA = 10000.0
GRID_W = 64
GRID_ROWS = SEQ // GRID_W
D_FF = 2816

A_HEADS = 4
A_DH = 64
A_QK = A_HEADS * 2 * A_DH
A_V = A_HEADS * 2 * A_DH
B_HEADS = 8
B_DH = 64
B_W = B_HEADS * B_DH
NA_ROWS = 8
NA_COLS = 16
C_WIDTH = 512
C_BLOCKS = 8
C_BW = C_WIDTH // C_BLOCKS
CONV_W = 4
RG_C = 8.0
D_HEADS = 8
D_NOPE = 64
D_ROPE = 32
D_VDIM = 64
D_Q_RANK = 256
D_KV_RANK = 128
EVEN_IN = 2 * A_QK + A_V + 3 * B_W

LANES = 128
SUBLANES = 8
VMEM_LIMIT_BYTES = 56 * 1024 * 1024

TOKEN_TILE = 1024
FFN_TILE = 1024
FF_CHUNK = 256
PV_TILE = 256
ATTN_ITEMS_PER_STEP = 2
LOGIT_CHUNK = 1024
DIFF_KEY_CHUNK = 1024
MLA_KEY_CHUNK = 2048
PAD_DK = 128
SCAN_UNROLL = 16

_NT = (((1,), (1,)), ((), ()))

LOG2E = math.log2(math.e)


def _params(*sem):
    return pltpu.CompilerParams(dimension_semantics=sem, vmem_limit_bytes=VMEM_LIMIT_BYTES)


def _resident(shape):
    nd = len(shape)
    return pl.BlockSpec(shape, lambda *_: (0,) * nd, pipeline_mode=pl.Buffered(1))


def _rms(x, g):
    return x * lax.rsqrt(jnp.mean(x * x, axis=-1, keepdims=True) + RMS_EPS) * g


def _dot(a, b):
    return jnp.dot(a, b, preferred_element_type=F32)


def _sigmoid(x):
    return 0.5 * (jnp.tanh(0.5 * x) + 1.0)


def _rope_lanes(x, cos, sin_signed, half, lo_mask):
    n = x.shape[-1]
    partner = jnp.where(lo_mask, pltpu.roll(x, n - half, 1), pltpu.roll(x, half, 1))
    return x * cos + partner * sin_signed


def _ffn_kernel(*refs, has_mix, has_final):
    it = iter(refs)
    x_ref = next(it)
    if has_mix:
        o1_ref, o2_ref, wo_ref = next(it), next(it), next(it)
    g_ref, wg_ref, wu_ref, wd_ref = next(it), next(it), next(it), next(it)
    gf_ref = next(it) if has_final else None
    y_ref = next(it)
    acc_ref = next(it)

    x = x_ref[...]
    if has_mix:
        half = wo_ref.shape[0] // 2
        x = x + _dot(o1_ref[...], wo_ref[:half, :]) + _dot(o2_ref[...], wo_ref[half:, :])
    n = _rms(x, g_ref[...]).astype(BF16)
    for c in range(D_FF // FF_CHUNK):
        sl = slice(c * FF_CHUNK, (c + 1) * FF_CHUNK)
        gate = _dot(n, wg_ref[:, sl])
        up = _dot(n, wu_ref[:, sl])
        h = (gate * jax.nn.sigmoid(gate) * up).astype(BF16)
        d = _dot(h, wd_ref[sl, :])
        if c == 0:
            acc_ref[...] = d
        else:
            acc_ref[...] += d
    y = x + 0.5 * acc_ref[...]
    if has_final:
        y = _rms(y, gf_ref[...])
    y_ref[...] = y


def _ffn(x, g, wg, wu, wd, layer, mix=None, final_g=None):
    tm = FFN_TILE
    tok = lambda w: pl.BlockSpec((tm, w), lambda i: (i, 0))
    slab = lambda a: pl.BlockSpec((None,) + a.shape[1:], lambda i: (layer, 0, 0),
                                  pipeline_mode=pl.Buffered(1))
    args, specs = [x], [tok(D_MODEL)]
    if mix is not None:
        o1, o2, wo = mix
        args += [o1, o2, wo]
        specs += [tok(o1.shape[1]), tok(o2.shape[1]), _resident(wo.shape)]
    args += [g, wg, wu, wd]
    specs += [_resident(g.shape), slab(wg), slab(wu), slab(wd)]
    if final_g is not None:
        args.append(final_g)
        specs.append(_resident(final_g.shape))
    return pl.pallas_call(
        functools.partial(_ffn_kernel, has_mix=mix is not None, has_final=final_g is not None),
        grid=(TOKENS // tm,),
        in_specs=specs,
        out_specs=tok(D_MODEL),
        out_shape=jax.ShapeDtypeStruct((TOKENS, D_MODEL), F32),
        scratch_shapes=[pltpu.VMEM((tm, D_MODEL), F32)],
        compiler_params=_params("parallel"),
        name="ffn_mix" if mix is not None else "ffn",
    )(*args)


def _even_in_kernel(x_ref, g_ref, w_ref, wvt_ref, cos_ref, sin_ref,
                    qa_ref, ka_ref, vt_ref, qb_ref, kb_ref):
    h = _rms(x_ref[...], g_ref[...]).astype(BF16)
    cos, sin = cos_ref[...], sin_ref[...]
    lane = lax.broadcasted_iota(jnp.int32, cos.shape, 1)
    lo = (lane % A_DH) < (A_DH // 2)
    scale = A_DH ** -0.5 * LOG2E

    def proj(i):
        return _dot(h, w_ref[:, i * A_QK:(i + 1) * A_QK])

    qa, ka = proj(0), proj(1)
    for j in range(A_QK // LANES):
        sl = slice(j * LANES, (j + 1) * LANES)
        qa_ref[:, sl] = (_rope_lanes(qa[:, sl], cos, sin, A_DH // 2, lo) * scale).astype(BF16)
        ka_ref[:, sl] = _rope_lanes(ka[:, sl], cos, sin, A_DH // 2, lo).astype(BF16)
    vt_ref[0] = lax.dot_general(wvt_ref[...], h, _NT, preferred_element_type=F32).astype(BF16)
    qb_ref[...] = (proj(3) * (B_DH ** -0.5 * LOG2E)).astype(BF16)
    kb_ref[...] = proj(4).astype(BF16)


def _even_in(x, g, w_in, wvt, cos, sin):
    tm = TOKEN_TILE
    per_seq = SEQ // tm
    tok = lambda w: pl.BlockSpec((tm, w), lambda i: (i, 0))
    pos = pl.BlockSpec((tm, LANES), lambda i: (i % per_seq, 0))
    vw = wvt.shape[0]
    tr = pl.BlockSpec((1, vw, tm), lambda i: (i // per_seq, 0, i % per_seq))
    out = jax.ShapeDtypeStruct((TOKENS, A_QK), BF16)
    out_t = jax.ShapeDtypeStruct((BATCH, vw, SEQ), BF16)
    return pl.pallas_call(
        _even_in_kernel,
        grid=(TOKENS // tm,),
        in_specs=[tok(D_MODEL), _resident(g.shape), _resident(w_in.shape), _resident(wvt.shape), pos, pos],
        out_specs=[tok(A_QK), tok(A_QK), tr, tok(B_W), tok(B_W)],
        out_shape=[out, out, out_t, out, out],
        compiler_params=_params("parallel"),
        name="even_in",
    )(x, g, w_in, wvt, cos, sin)


def _logit_chunks(q, k_ref, s_buf, m_buf):
    tq, qw = q.shape
    lane = lax.broadcasted_iota(jnp.int32, q.shape, 1)
    zero = jnp.zeros_like(q)
    q2 = jnp.concatenate([jnp.where(lane < qw // 2, q, zero), jnp.where(lane >= qw // 2, q, zero)], axis=0)
    groups8 = LOGIT_CHUNK // SUBLANES
    mrun = [None, None]
    for c in range(SEQ // LOGIT_CHUNK):
        keys = slice(c * LOGIT_CHUNK, (c + 1) * LOGIT_CHUNK)
        s2 = lax.dot_general(k_ref[keys, :], q2, _NT, preferred_element_type=F32)
        for j in range(2):
            s = s2[:, j * tq:(j + 1) * tq]
            s_buf[j, keys, :] = s
            part = jnp.max(s.reshape(groups8, SUBLANES, tq), axis=0)
            mrun[j] = part if mrun[j] is None else jnp.maximum(mrun[j], part)
        yield
    for j in range(2):
        m_buf[j] = jnp.broadcast_to(jnp.max(mrun[j], axis=0, keepdims=True), (SUBLANES, tq))


def _softmax_pv_chunks(s_buf, m_buf, vt_ref, vrows, acc_scr, l_scr, key_chunk):
    tq = s_buf.shape[-1]
    groups8 = key_chunk // SUBLANES
    m = [m_buf[j][None] for j in range(2)]
    acc = [None, None]
    lrun = [None, None]
    for c in range(SEQ // key_chunk):
        keys = slice(c * key_chunk, (c + 1) * key_chunk)
        for j in range(2):
            e = jnp.exp2(s_buf[j, keys, :].reshape(groups8, SUBLANES, tq) - m[j])
            part = jnp.sum(e, axis=0)
            lrun[j] = part if lrun[j] is None else lrun[j] + part
            pv = _dot(vt_ref[0, vrows[j], keys], e.reshape(key_chunk, tq).astype(BF16))
            acc[j] = pv if acc[j] is None else acc[j] + pv
            yield
    for j in range(2):
        acc_scr[j, :acc[j].shape[0], :] = acc[j]
        l_scr[j] = lrun[j]


def _stream_attn_kernel(*refs, n_steps, shared_values, finish, key_chunk):
    qf_ref, qn_ref, kn_ref, vt_ref = refs[:4]
    extra = refs[4:-7]
    o_ref, s_a, m_a, s_b, m_b, acc_scr, l_scr = refs[-7:]
    t = pl.program_id(0)
    dv = vt_ref.shape[1]
    vrows = [slice(None)] * 2 if shared_values else [slice(0, dv // 2), slice(dv // 2, dv)]
    rows = dv if shared_values else dv // 2
    n_sub = s_a.shape[0]

    def logit_items(q_ref, s_buf, m_buf):
        for sub in range(n_sub):
            q = q_ref[sub * PV_TILE:(sub + 1) * PV_TILE, :]
            yield from _logit_chunks(q, kn_ref, s_buf.at[sub], m_buf.at[sub])

    def softmax_items(s_buf, m_buf):
        for sub in range(n_sub):
            yield from _softmax_pv_chunks(s_buf.at[sub], m_buf.at[sub], vt_ref, vrows,
                                          acc_scr.at[sub], l_scr.at[sub], key_chunk)

    @pl.when(t == 0)
    def _():
        for _ in logit_items(qf_ref, s_a, m_a):
            pass
        acc_scr[...] = jnp.zeros_like(acc_scr)
        l_scr[...] = jnp.ones_like(l_scr)

    def finish_previous():
        for sub in range(n_sub):
            result = [(acc_scr[sub, j, :rows, :], jnp.sum(l_scr[sub, j], axis=0, keepdims=True))
                      for j in range(2)]
            o_ref[sub * PV_TILE:(sub + 1) * PV_TILE, :] = finish(result, *extra).astype(o_ref.dtype)

    def step(nxt, cur):
        finish_previous()
        logits = logit_items(qn_ref, nxt[0], nxt[1])
        softmax = softmax_items(cur[0], cur[1])
        ratio = max((SEQ // key_chunk) * 2 // (SEQ // LOGIT_CHUNK), 1)
        for i, _ in enumerate(softmax):
            if i % ratio == 0:
                next(logits, None)
        for _ in logits:
            pass

    live = t < n_steps
    pl.when(live & ((t & 1) == 0))(lambda: step((s_b, m_b), (s_a, m_a)))
    pl.when(live & ((t & 1) == 1))(lambda: step((s_a, m_a), (s_b, m_b)))
    pl.when(t == n_steps)(finish_previous)


def _stream_attn(q, k, vt, extra, finish, *, groups, shared_values, key_chunk, name):
    tq = ATTN_ITEMS_PER_STEP * PV_TILE
    nq = SEQ // tq
    n_steps = BATCH * groups * nq
    qw = q.shape[1] // groups

    def rows(t):
        return (t // (groups * nq)) * nq + t % nq

    def group(t):
        return (t // nq) % groups

    def batch(t):
        return t // (groups * nq)

    nxt = lambda t: jnp.minimum(t + 1, n_steps - 1)
    cur = lambda t: jnp.minimum(t, n_steps - 1)
    prv = lambda t: jnp.maximum(t - 1, 0)
    in_specs = [
        pl.BlockSpec((tq, qw), lambda t: (0, 0)),
        pl.BlockSpec((tq, qw), lambda t: (rows(nxt(t)), group(nxt(t)))),
        pl.BlockSpec((SEQ, qw), lambda t: (batch(nxt(t)), group(nxt(t)))),
        pl.BlockSpec((1, LANES, SEQ), lambda t: (batch(cur(t)), group(cur(t)), 0)),
    ] + [_resident(a.shape) for a in extra]
    sbuf = pltpu.VMEM((ATTN_ITEMS_PER_STEP, 2, SEQ, PV_TILE), F32)
    mbuf = pltpu.VMEM((ATTN_ITEMS_PER_STEP, 2, SUBLANES, PV_TILE), F32)
    return pl.pallas_call(
        functools.partial(_stream_attn_kernel, n_steps=n_steps, shared_values=shared_values, finish=finish,
                          key_chunk=key_chunk),
        grid=(n_steps + 1,),
        in_specs=in_specs,
        out_specs=pl.BlockSpec((tq, LANES), lambda t: (rows(prv(t)), group(prv(t)))),
        out_shape=jax.ShapeDtypeStruct((TOKENS, groups * LANES), BF16),
        scratch_shapes=[sbuf, mbuf, sbuf, mbuf,
                        pltpu.VMEM((ATTN_ITEMS_PER_STEP, 2, LANES, PV_TILE), F32),
                        pltpu.VMEM((ATTN_ITEMS_PER_STEP, 2, SUBLANES, PV_TILE), F32)],
        compiler_params=_params("arbitrary"),
        name=name,
    )(q, q, k, vt, *extra)


def _diff_finish(result, lam_ref, g_ref, *, lam_init):
    (acc1, l1), (acc2, l2) = result
    lf = lam_ref[...]
    lam = (jnp.exp(jnp.sum(lf[0:1] * lf[1:2], axis=-1, keepdims=True))
           - jnp.exp(jnp.sum(lf[2:3] * lf[3:4], axis=-1, keepdims=True)) + lam_init)
    o = (acc1 / l1 - lam * (acc2 / l2)).T
    return _rms(o, g_ref[...]) * (1.0 - lam_init)


def _diff_attn(qa, ka, vt, lam_vec, subln_g, lam_init):
    return _stream_attn(qa, ka, vt, (lam_vec, subln_g),
                        functools.partial(_diff_finish, lam_init=lam_init),
                        groups=A_HEADS, shared_values=True, key_chunk=DIFF_KEY_CHUNK, name="diff_attn")


NA_QROWS = 4
NA_KROWS = NA_QROWS + NA_ROWS
NA_BLOCKS = GRID_ROWS // NA_QROWS
NA_PLACEMENTS = 3


def _na_window_start(g):
    lo, hi = 0, GRID_ROWS - NA_KROWS
    s = g * NA_QROWS - NA_ROWS // 2
    if isinstance(g, int):
        return min(max(s, lo), hi)
    return jnp.clip(s, lo, hi)


def _na_attn_kernel(q_ref, k_ref, vt_ref, bias_ref, o_ref, s_a, m_a, s_b, m_b):
    nq = NA_QROWS * GRID_W
    nk = NA_KROWS * GRID_W
    groups8 = nk // SUBLANES
    lane = lax.broadcasted_iota(jnp.int32, (nq, 2 * B_DH), 1)

    def key_start(g):
        return pl.multiple_of(_na_window_start(g) * GRID_W, NA_QROWS * GRID_W)

    def logits(g, s_buf, m_buf):
        var = jnp.where(g == 0, 0, jnp.where(g == NA_BLOCKS - 1, 2, 1))
        q = q_ref[pl.ds(pl.multiple_of(g * nq, nq), nq), :]
        zero = jnp.zeros_like(q)
        q2 = jnp.concatenate([jnp.where(lane < B_DH, q, zero), jnp.where(lane >= B_DH, q, zero)], axis=0)
        s2 = lax.dot_general(k_ref[pl.ds(key_start(g), nk), :], q2, _NT, preferred_element_type=F32)
        for j in range(2):
            s = s2[:, j * nq:(j + 1) * nq] + bias_ref[j, var]
            s_buf[j] = s
            m = jnp.max(jnp.max(s.reshape(groups8, SUBLANES, nq), axis=0), axis=0, keepdims=True)
            m_buf[j] = jnp.broadcast_to(m, (SUBLANES, nq))

    def softmax_pv(g, s_buf, m_buf):
        outs = []
        for j in range(2):
            e = jnp.exp2(s_buf[j].reshape(groups8, SUBLANES, nq) - m_buf[j][None])
            l = jnp.sum(jnp.sum(e, axis=0), axis=0, keepdims=True)
            vt = vt_ref[0, j * B_DH:(j + 1) * B_DH, pl.ds(key_start(g), nk)]
            outs.append(_dot(vt, e.reshape(nk, nq).astype(BF16)) / l)
        o_ref[pl.ds(pl.multiple_of(g * nq, nq), nq), :] = jnp.concatenate(outs, axis=0).T.astype(BF16)

    logits(0, s_a, m_a)

    def pair(i, carry):
        g = 2 * i
        logits(g + 1, s_b, m_b)
        softmax_pv(g, s_a, m_a)
        logits(jnp.minimum(g + 2, NA_BLOCKS - 1), s_a, m_a)
        softmax_pv(g + 1, s_b, m_b)
        return carry

    lax.fori_loop(0, NA_BLOCKS // 2, pair, 0)


def _na_attn(qb, kb, vt, bias):
    hw = 2 * B_DH
    first_block = A_V // hw
    spec = pl.BlockSpec((SEQ, hw), lambda p, b: (b, p))
    vspec = pl.BlockSpec((1, hw, SEQ), lambda p, b: (b, first_block + p, 0))
    bspec = pl.BlockSpec((2,) + bias.shape[1:], lambda p, b: (p, 0, 0, 0))
    return pl.pallas_call(
        _na_attn_kernel,
        grid=(B_HEADS // 2, BATCH),
        in_specs=[spec, spec, vspec, bspec],
        out_specs=spec,
        out_shape=jax.ShapeDtypeStruct((TOKENS, B_W), BF16),
        scratch_shapes=[pltpu.VMEM((2,) + bias.shape[2:], F32), pltpu.VMEM((2, SUBLANES, bias.shape[3]), F32)] * 2,
        compiler_params=_params("parallel", "parallel"),
        name="na_attn",
    )(qb, kb, vt, bias)


def _na_bias_table(rpb):
    c = np.arange(GRID_W)
    c0 = np.clip(c - NA_COLS // 2, 0, GRID_W - NA_COLS)
    col_in = (c[None, :] >= c0[:, None]) & (c[None, :] < c0[:, None] + NA_COLS)
    dc = np.clip(c[None, :] - c[:, None], -(NA_COLS - 1), NA_COLS - 1) + NA_COLS - 1
    rpb = rpb.astype(F32) * LOG2E
    n_dr = 2 * NA_ROWS - 1
    toe = jnp.zeros((B_HEADS, GRID_W, n_dr, GRID_W), F32)
    for d in range(2 * NA_COLS - 1):
        toe = jnp.where((dc.T == d)[:, None, :], rpb[:, None, ::-1, d, None], toe)
    toe = jnp.where(col_in.T[:, None, :], toe, -jnp.inf).reshape(B_HEADS, GRID_W, n_dr * GRID_W)
    masked = lambda n: [jnp.full((B_HEADS, GRID_W, n * GRID_W), -jnp.inf, F32)] if n else []
    first_key_row = lambda qr: min(max(qr - NA_ROWS // 2, 0), GRID_ROWS - NA_ROWS)
    slabs = []
    for g in (0, 1, NA_BLOCKS - 1):
        for kl in range(NA_KROWS):
            kr = _na_window_start(g) + kl
            qrs = [g * NA_QROWS + ql for ql in range(NA_QROWS)]
            ok = [ql for ql, qr in enumerate(qrs) if first_key_row(qr) <= kr < first_key_row(qr) + NA_ROWS]
            if not ok:
                slabs.append(masked(NA_QROWS)[0])
                continue
            lo, hi = ok[0], ok[-1]
            assert ok == list(range(lo, hi + 1))
            first = n_dr - 1 - (kr - qrs[lo] + NA_ROWS - 1)
            own = toe[:, :, first * GRID_W:(first + hi - lo + 1) * GRID_W]
            slabs.append(jnp.concatenate(masked(lo) + [own] + masked(NA_QROWS - 1 - hi), axis=-1))
    return jnp.stack(slabs, axis=1).reshape(
        B_HEADS, NA_PLACEMENTS, NA_KROWS * GRID_W, NA_QROWS * GRID_W)


def _odd_in_kernel(x_ref, g_ref, w_ref, gq_ref, gkv_ref, wuq_ref, wk_ref, wvt_ref,
                   cos_ref, sin_ref, xc_ref, gc_ref, q_ref, k_ref, vt_ref):
    h = _rms(x_ref[...], g_ref[...]).astype(BF16)
    cos, sin = cos_ref[...], sin_ref[...]
    lane = lax.broadcasted_iota(jnp.int32, cos.shape, 1)
    lo = lane < D_NOPE + D_ROPE // 2
    scale = (D_NOPE + D_ROPE) ** -0.5 * LOG2E
    o = 0
    xc_ref[...] = _dot(h, w_ref[:, o:o + C_WIDTH]); o += C_WIDTH
    gc_ref[...] = _dot(h, w_ref[:, o:o + C_WIDTH]); o += C_WIDTH
    cq = _dot(h, w_ref[:, o:o + D_Q_RANK]); o += D_Q_RANK
    ckv = _dot(h, w_ref[:, o:o + D_KV_RANK]); o += D_KV_RANK
    kr = _dot(h, w_ref[:, o:o + PAD_DK])
    q = _dot(_rms(cq, gq_ref[...]).astype(BF16), wuq_ref[...])
    ckvn = _rms(ckv, gkv_ref[...]).astype(BF16)
    kn = _dot(ckvn, wk_ref[...])
    kpe = _rope_lanes(kr, cos, sin, D_ROPE // 2, lo)
    for j in range(D_HEADS):
        sl = slice(j * PAD_DK, (j + 1) * PAD_DK)
        q_ref[:, sl] = (_rope_lanes(q[:, sl], cos, sin, D_ROPE // 2, lo) * scale).astype(BF16)
        k_ref[:, sl] = (kn[:, sl] + kpe).astype(BF16)
    vt_ref[0] = lax.dot_general(wvt_ref[...], ckvn, _NT, preferred_element_type=F32).astype(BF16)


def _odd_in(x, g, w_in, gq, gkv, wuq, wk, wvt, cos, sin):
    tm = TOKEN_TILE
    per_seq = SEQ // tm
    tok = lambda w: pl.BlockSpec((tm, w), lambda i: (i, 0))
    pos = pl.BlockSpec((tm, LANES), lambda i: (i % per_seq, 0))
    sds = lambda w, dt: jax.ShapeDtypeStruct((TOKENS, w), dt)
    qw = D_HEADS * PAD_DK
    vw = D_HEADS * D_VDIM
    return pl.pallas_call(
        _odd_in_kernel,
        grid=(TOKENS // tm,),
        in_specs=[tok(D_MODEL)] + [_resident(a.shape) for a in (g, w_in, gq, gkv, wuq, wk, wvt)] + [pos, pos],
        out_specs=[tok(C_WIDTH), tok(C_WIDTH), tok(qw), tok(qw),
                   pl.BlockSpec((1, vw, tm), lambda i: (i // per_seq, 0, i % per_seq))],
        out_shape=[sds(C_WIDTH, F32), sds(C_WIDTH, F32), sds(qw, BF16), sds(qw, BF16),
                   jax.ShapeDtypeStruct((BATCH, vw, SEQ), BF16)],
        compiler_params=_params("parallel"),
        name="odd_in",
    )(x, g, w_in, gq, gkv, wuq, wk, wvt, cos, sin)


def _rglru_kernel(xc_ref, gc_ref, cw_ref, cb_ref, wgate_ref, bgate_ref, lam_ref, o_ref,
                  a_scr, b_scr, h_scr):
    cw = cw_ref[...]
    x = xc_ref[...]
    row = lax.broadcasted_iota(jnp.int32, x.shape, 0)

    def shifted(d):
        rolled = pltpu.roll(x, (-d) % SEQ, 0)
        valid = (row + d >= 0) & (row + d < SEQ)
        return jnp.where(valid, rolled, 0.0)

    lp = (CONV_W - 1) // 2
    u = cb_ref[...] + sum((x if j == lp else shifted(j - lp)) * cw[j:j + 1] for j in range(CONV_W))
    gates = _dot(u.astype(BF16), wgate_ref[0]) + bgate_ref[0]
    lam = lam_ref[0]
    for d in range(2):
        o = 2 * d * LANES
        r_t = _sigmoid(gates[:, o:o + LANES])
        i_t = _sigmoid(gates[:, o + LANES:o + 2 * LANES])
        log_a = (-RG_C * r_t) * jax.nn.softplus(-lam[:, d * LANES:(d + 1) * LANES])
        a = jnp.exp(log_a)
        y = -jnp.tanh(log_a) * (a * a + 1.0)
        mult = jnp.where(y > 0.0, y * lax.rsqrt(y), 0.0)
        mult = jnp.where(row == (SEQ - 1 if d else 0), 1.0, mult)
        a_scr[d] = a
        b_scr[d] = mult * i_t * u

    n_tiles = SEQ // SUBLANES
    srow = lax.broadcasted_iota(jnp.int32, (SUBLANES, LANES), 0)

    def tile_scan(a, b, h_prev, reverse):
        for d in (1, 2, 4):
            if reverse:
                keep = srow < SUBLANES - d
                shift = SUBLANES - d
            else:
                keep = srow >= d
                shift = d
            a_s = jnp.where(keep, pltpu.roll(a, shift, 0), 1.0)
            b_s = jnp.where(keep, pltpu.roll(b, shift, 0), 0.0)
            b = a * b_s + b
            a = a * a_s
        return a * h_prev + b

    def step(i, carry):
        hf, hr = carry
        tf = pl.multiple_of(i * SUBLANES, SUBLANES)
        h = tile_scan(a_scr[0, pl.ds(tf, SUBLANES), :], b_scr[0, pl.ds(tf, SUBLANES), :], hf, False)
        h_scr[0, pl.ds(tf, SUBLANES), :] = h
        hf = jnp.broadcast_to(h[SUBLANES - 1:SUBLANES, :], h.shape)
        tr = pl.multiple_of((n_tiles - 1 - i) * SUBLANES, SUBLANES)
        h = tile_scan(a_scr[1, pl.ds(tr, SUBLANES), :], b_scr[1, pl.ds(tr, SUBLANES), :], hr, True)
        h_scr[1, pl.ds(tr, SUBLANES), :] = h
        hr = jnp.broadcast_to(h[0:1, :], h.shape)
        return hf, hr

    z = jnp.zeros((SUBLANES, LANES), F32)
    lax.fori_loop(0, n_tiles, step, (z, z), unroll=SCAN_UNROLL)
    o_ref[...] = (jax.nn.gelu(gc_ref[...]) * (h_scr[0] + h_scr[1])).astype(BF16)


def _rglru(xc, gc, conv_w, conv_b, wgate, bgate, lam):
    ng = C_WIDTH // LANES
    seq = pl.BlockSpec((SEQ, LANES), lambda b, g: (b, g))
    grp = lambda a: pl.BlockSpec((1,) + a.shape[1:], lambda b, g: (g, 0, 0))
    return pl.pallas_call(
        _rglru_kernel,
        grid=(BATCH, ng),
        in_specs=[seq, seq,
                  pl.BlockSpec((CONV_W, LANES), lambda b, g: (0, g)),
                  pl.BlockSpec((1, LANES), lambda b, g: (0, g)),
                  grp(wgate), grp(bgate), grp(lam)],
        out_specs=seq,
        out_shape=jax.ShapeDtypeStruct((TOKENS, C_WIDTH), BF16),
        scratch_shapes=[pltpu.VMEM((2, SEQ, LANES), F32)] * 3,
        compiler_params=_params("parallel", "parallel"),
        name="rglru",
    )(xc, gc, conv_w, conv_b, wgate, bgate, lam)


def _rglru_gate_params(rg_wa, rg_ba, rg_wx, rg_bx, rg_lam):
    ng = C_WIDTH // LANES
    per = LANES // C_BW

    def dense(w):
        w = w.reshape(ng, per, C_BW, C_BW)
        eye = jnp.eye(per, dtype=w.dtype)
        return jnp.einsum('gpcd,pq->gpcqd', w, eye).reshape(ng, LANES, LANES)

    wgate = jnp.concatenate([dense(rg_wa[0]), dense(rg_wx[0]), dense(rg_wa[1]), dense(rg_wx[1])], axis=-1)
    grp = lambda v: v.reshape(ng, 1, LANES)
    bgate = jnp.concatenate([grp(rg_ba[0]), grp(rg_bx[0]), grp(rg_ba[1]), grp(rg_bx[1])], axis=-1)
    lam = jnp.concatenate([grp(rg_lam[0]), grp(rg_lam[1])], axis=-1)
    return wgate.astype(BF16), bgate.astype(F32), lam.astype(F32)


def _mla_finish(result):
    (acc1, l1), (acc2, l2) = result
    return jnp.concatenate([acc1 / l1, acc2 / l2], axis=0).T


def _mla_attn(q, k, vt):
    return _stream_attn(q, k, vt, (), _mla_finish, groups=D_HEADS // 2, shared_values=False,
                        key_chunk=MLA_KEY_CHUNK, name="mla_attn")


def _rope_angles(dim):
    inv = 1.0 / (ROPE_THETA ** (jnp.arange(0, dim, 2, dtype=F32) / dim))
    ang = jnp.arange(SEQ, dtype=F32)[:, None] * inv[None, :]
    return jnp.cos(ang), jnp.sin(ang)


def _even_rope_tables():
    cos, sin = _rope_angles(A_DH)
    reps = LANES // A_DH
    return (jnp.tile(jnp.concatenate([cos, cos], -1), (1, reps)),
            jnp.tile(jnp.concatenate([-sin, sin], -1), (1, reps)))


def _odd_rope_tables():
    cos, sin = _rope_angles(D_ROPE)
    ones = jnp.ones((SEQ, D_NOPE), F32)
    zn = jnp.zeros((SEQ, D_NOPE), F32)
    zp = jnp.zeros((SEQ, PAD_DK - D_NOPE - D_ROPE), F32)
    return (jnp.concatenate([ones, cos, cos, zp], -1),
            jnp.concatenate([zn, -sin, sin, zp], -1))


def _odd_weights(w_in, wuq, wukv):
    base = 2 * C_WIDTH + D_Q_RANK + D_KV_RANK
    zl = jnp.zeros((D_MODEL, D_NOPE), w_in.dtype)
    zr = jnp.zeros((D_MODEL, PAD_DK - D_NOPE - D_ROPE), w_in.dtype)
    w_in_p = jnp.concatenate([w_in[:, :base], zl, w_in[:, base:], zr], axis=-1)
    dqk = D_NOPE + D_ROPE
    wuq_p = jnp.pad(wuq.reshape(D_Q_RANK, D_HEADS, dqk), ((0, 0), (0, 0), (0, PAD_DK - dqk)))
    wukv_h = wukv.reshape(D_KV_RANK, D_HEADS, D_NOPE + D_VDIM)
    wk_p = jnp.pad(wukv_h[:, :, :D_NOPE], ((0, 0), (0, 0), (0, PAD_DK - D_NOPE)))
    wv = wukv_h[:, :, D_NOPE:]
    return (w_in_p.astype(BF16), wuq_p.reshape(D_Q_RANK, -1).astype(BF16),
            wk_p.reshape(D_KV_RANK, -1).astype(BF16), wv.reshape(D_KV_RANK, -1).T.astype(BF16))


def kernel(x, ffn1_norm, ffn1_wg, ffn1_wu, ffn1_wd, mix_norm, ffn2_norm, ffn2_wg, ffn2_wu, ffn2_wd, final_norm, ev_w_in, ev_w_out, diff_lam, diff_subln, na_rpb, od_w_in, od_w_out, conv_w, conv_b, rg_wa, rg_ba, rg_wx, rg_bx, rg_lam, mla_gq, mla_gkv, mla_wuq, mla_wukv):
    row = lambda v: v.reshape(1, -1).astype(F32)
    bf = lambda w: w.astype(BF16)
    xt = x.reshape(TOKENS, D_MODEL)

    ffn1 = (bf(ffn1_wg), bf(ffn1_wu), bf(ffn1_wd))
    ffn2 = (bf(ffn2_wg), bf(ffn2_wu), bf(ffn2_wd))

    xt = _ffn(xt, row(ffn1_norm[0]), *ffn1, 0)
    cos_a, sin_a = _even_rope_tables()
    wv_cols = jnp.concatenate([ev_w_in[0][:, 2 * A_QK:2 * A_QK + A_V], ev_w_in[0][:, EVEN_IN - B_W:]], axis=1)
    qa, ka, vt, qb, kb = _even_in(xt, row(mix_norm[0]), bf(ev_w_in[0]), bf(wv_cols.T), cos_a, sin_a)
    lam_init0 = 0.8 - 0.6 * math.exp(-0.3 * 0)
    oa = _diff_attn(qa, ka, vt, diff_lam[0].astype(F32), row(diff_subln[0]), lam_init0)
    ob = _na_attn(qb, kb, vt, _na_bias_table(na_rpb[0]))
    xt = _ffn(xt, row(ffn2_norm[0]), *ffn2, 0, mix=(oa, ob, bf(ev_w_out[0])))

    xt = _ffn(xt, row(ffn1_norm[1]), *ffn1, 1)
    cos_d, sin_d = _odd_rope_tables()
    w_in_p, wuq_p, wk_p, wvt_p = _odd_weights(od_w_in[0], mla_wuq[0], mla_wukv[0])
    xc, gc, q, k, vt = _odd_in(xt, row(mix_norm[1]), w_in_p, row(mla_gq[0]), row(mla_gkv[0]),
                               wuq_p, wk_p, wvt_p, cos_d, sin_d)
    wgate, bgate, lam = _rglru_gate_params(rg_wa[0], rg_ba[0], rg_wx[0], rg_bx[0], rg_lam[0])
    oc = _rglru(xc, gc, conv_w[0].astype(F32), row(conv_b[0]), wgate, bgate, lam)
    od = _mla_attn(q, k, vt)
    xt = _ffn(xt, row(ffn2_norm[1]), *ffn2, 1, mix=(oc, od, bf(od_w_out[0])), final_g=row(final_norm))
    return xt.reshape(BATCH, SEQ, D_MODEL)
```

```python
import functools
import math

import jax
import jax.numpy as jnp
import numpy as np
from jax import lax
from jax.experimental import pallas as pl
from jax.experimental.pallas import tpu as pltpu

F32 = jnp.float32
BF16 = jnp.bfloat16

D_MODEL = 1024
BATCH = 4
SEQ = 4096
DEPTH = 2
TOKENS = BATCH * SEQ
RMS_EPS = 1e-6
ROPE_THETA = 10000.0
GRID_W = 64
GRID_ROWS = SEQ // GRID_W
D_FF = 2816

A_HEADS = 4
A_DH = 64
A_QK = A_HEADS * 2 * A_DH
A_V = A_HEADS * 2 * A_DH
B_HEADS = 8
B_DH = 64
B_W = B_HEADS * B_DH
NA_ROWS = 8
NA_COLS = 16
C_WIDTH = 512
C_BLOCKS = 8
C_BW = C_WIDTH // C_BLOCKS
CONV_W = 4
RG_C = 8.0
D_HEADS = 8
D_NOPE = 64
D_ROPE = 32
D_VDIM = 64
D_Q_RANK = 256
D_KV_RANK = 128
EVEN_IN = 2 * A_QK + A_V + 3 * B_W

LANES = 128
SUBLANES = 8
VMEM_LIMIT_BYTES = 56 * 1024 * 1024

TOKEN_TILE = 1024
FFN_TILE = 1024
FF_CHUNK = 256
PV_TILE = 256
ATTN_ITEMS_PER_STEP = 2
LOGIT_CHUNK = 1024
DIFF_KEY_CHUNK = 1024
MLA_KEY_CHUNK = 2048
PAD_DK = 128
SCAN_UNROLL = 16

_NT = (((1,), (1,)), ((), ()))

LOG2E = math.log2(math.e)


def _params(*sem):
    return pltpu.CompilerParams(dimension_semantics=sem, vmem_limit_bytes=VMEM_LIMIT_BYTES)


def _resident(shape):
    nd = len(shape)
    return pl.BlockSpec(shape, lambda *_: (0,) * nd, pipeline_mode=pl.Buffered(1))


def _rms(x, g):
    return x * lax.rsqrt(jnp.mean(x * x, axis=-1, keepdims=True) + RMS_EPS) * g


def _dot(a, b):
    return jnp.dot(a, b, preferred_element_type=F32)


def _sigmoid(x):
    return 0.5 * (jnp.tanh(0.5 * x) + 1.0)


def _rope_lanes(x, cos, sin_signed, half, lo_mask):
    n = x.shape[-1]
    partner = jnp.where(lo_mask, pltpu.roll(x, n - half, 1), pltpu.roll(x, half, 1))
    return x * cos + partner * sin_signed


def _ffn_kernel(*refs, has_mix, has_final):
    it = iter(refs)
    x_ref = next(it)
    if has_mix:
        o1_ref, o2_ref, wo_ref = next(it), next(it), next(it)
    g_ref, wg_ref, wu_ref, wd_ref = next(it), next(it), next(it), next(it)
    gf_ref = next(it) if has_final else None
    y_ref = next(it)
    acc_ref = next(it)

    x = x_ref[...]
    if has_mix:
        half = wo_ref.shape[0] // 2
        x = x + _dot(o1_ref[...], wo_ref[:half, :]) + _dot(o2_ref[...], wo_ref[half:, :])
    n = _rms(x, g_ref[...]).astype(BF16)
    for c in range(D_FF // FF_CHUNK):
        sl = slice(c * FF_CHUNK, (c + 1) * FF_CHUNK)
        gate = _dot(n, wg_ref[:, sl])
        up = _dot(n, wu_ref[:, sl])
        h = (gate * jax.nn.sigmoid(gate) * up).astype(BF16)
        d = _dot(h, wd_ref[sl, :])
        if c == 0:
            acc_ref[...] = d
        else:
            acc_ref[...] += d
    y = x + 0.5 * acc_ref[...]
    if has_final:
        y = _rms(y, gf_ref[...])
    y_ref[...] = y


def _ffn(x, g, wg, wu, wd, layer, mix=None, final_g=None):
    tm = FFN_TILE
    tok = lambda w: pl.BlockSpec((tm, w), lambda i: (i, 0))
    slab = lambda a: pl.BlockSpec((None,) + a.shape[1:], lambda i: (layer, 0, 0),
                                  pipeline_mode=pl.Buffered(1))
    args, specs = [x], [tok(D_MODEL)]
    if mix is not None:
        o1, o2, wo = mix
        args += [o1, o2, wo]
        specs += [tok(o1.shape[1]), tok(o2.shape[1]), _resident(wo.shape)]
    args += [g, wg, wu, wd]
    specs += [_resident(g.shape), slab(wg), slab(wu), slab(wd)]
    if final_g is not None:
        args.append(final_g)
        specs.append(_resident(final_g.shape))
    return pl.pallas_call(
        functools.partial(_ffn_kernel, has_mix=mix is not None, has_final=final_g is not None),
        grid=(TOKENS // tm,),
        in_specs=specs,
        out_specs=tok(D_MODEL),
        out_shape=jax.ShapeDtypeStruct((TOKENS, D_MODEL), F32),
        scratch_shapes=[pltpu.VMEM((tm, D_MODEL), F32)],
        compiler_params=_params("parallel"),
        name="ffn_mix" if mix is not None else "ffn",
    )(*args)


def _even_in_kernel(x_ref, g_ref, w_ref, wvt_ref, cos_ref, sin_ref,
                    qa_ref, ka_ref, vt_ref, qb_ref, kb_ref):
    h = _rms(x_ref[...], g_ref[...]).astype(BF16)
    cos, sin = cos_ref[...], sin_ref[...]
    lane = lax.broadcasted_iota(jnp.int32, cos.shape, 1)
    lo = (lane % A_DH) < (A_DH // 2)
    scale = A_DH ** -0.5 * LOG2E

    def proj(i):
        return _dot(h, w_ref[:, i * A_QK:(i + 1) * A_QK])

    qa, ka = proj(0), proj(1)
    for j in range(A_QK // LANES):
        sl = slice(j * LANES, (j + 1) * LANES)
        qa_ref[:, sl] = (_rope_lanes(qa[:, sl], cos, sin, A_DH // 2, lo) * scale).astype(BF16)
        ka_ref[:, sl] = _rope_lanes(ka[:, sl], cos, sin, A_DH // 2, lo).astype(BF16)
    vt_ref[0] = lax.dot_general(wvt_ref[...], h, _NT, preferred_element_type=F32).astype(BF16)
    qb_ref[...] = (proj(3) * (B_DH ** -0.5 * LOG2E)).astype(BF16)
    kb_ref[...] = proj(4).astype(BF16)


def _even_in(x, g, w_in, wvt, cos, sin):
    tm = TOKEN_TILE
    per_seq = SEQ // tm
    tok = lambda w: pl.BlockSpec((tm, w), lambda i: (i, 0))
    pos = pl.BlockSpec((tm, LANES), lambda i: (i % per_seq, 0))
    vw = wvt.shape[0]
    tr = pl.BlockSpec((1, vw, tm), lambda i: (i // per_seq, 0, i % per_seq))
    out = jax.ShapeDtypeStruct((TOKENS, A_QK), BF16)
    out_t = jax.ShapeDtypeStruct((BATCH, vw, SEQ), BF16)
    return pl.pallas_call(
        _even_in_kernel,
        grid=(TOKENS // tm,),
        in_specs=[tok(D_MODEL), _resident(g.shape), _resident(w_in.shape), _resident(wvt.shape), pos, pos],
        out_specs=[tok(A_QK), tok(A_QK), tr, tok(B_W), tok(B_W)],
        out_shape=[out, out, out_t, out, out],
        compiler_params=_params("parallel"),
        name="even_in",
    )(x, g, w_in, wvt, cos, sin)


def _logit_chunks(q, k_ref, s_buf, m_buf):
    tq, qw = q.shape
    lane = lax.broadcasted_iota(jnp.int32, q.shape, 1)
    zero = jnp.zeros_like(q)
    q2 = jnp.concatenate([jnp.where(lane < qw // 2, q, zero), jnp.where(lane >= qw // 2, q, zero)], axis=0)
    groups8 = LOGIT_CHUNK // SUBLANES
    mrun = [None, None]
    for c in range(SEQ // LOGIT_CHUNK):
        keys = slice(c * LOGIT_CHUNK, (c + 1) * LOGIT_CHUNK)
        s2 = lax.dot_general(k_ref[keys, :], q2, _NT, preferred_element_type=F32)
        for j in range(2):
            s = s2[:, j * tq:(j + 1) * tq]
            s_buf[j, keys, :] = s
            part = jnp.max(s.reshape(groups8, SUBLANES, tq), axis=0)
            mrun[j] = part if mrun[j] is None else jnp.maximum(mrun[j], part)
        yield
    for j in range(2):
        m_buf[j] = jnp.broadcast_to(jnp.max(mrun[j], axis=0, keepdims=True), (SUBLANES, tq))


def _softmax_pv_chunks(s_buf, m_buf, vt_ref, vrows, acc_scr, l_scr, key_chunk):
    tq = s_buf.shape[-1]
    groups8 = key_chunk // SUBLANES
    m = [m_buf[j][None] for j in range(2)]
    acc = [None, None]
    lrun = [None, None]
    for c in range(SEQ // key_chunk):
        keys = slice(c * key_chunk, (c + 1) * key_chunk)
        for j in range(2):
            e = jnp.exp2(s_buf[j, keys, :].reshape(groups8, SUBLANES, tq) - m[j])
            part = jnp.sum(e, axis=0)
            lrun[j] = part if lrun[j] is None else lrun[j] + part
            pv = _dot(vt_ref[0, vrows[j], keys], e.reshape(key_chunk, tq).astype(BF16))
            acc[j] = pv if acc[j] is None else acc[j] + pv
            yield
    for j in range(2):
        acc_scr[j, :acc[j].shape[0], :] = acc[j]
        l_scr[j] = lrun[j]


def _stream_attn_kernel(*refs, n_steps, shared_values, finish, key_chunk):
    qf_ref, qn_ref, kn_ref, vt_ref = refs[:4]
    extra = refs[4:-7]
    o_ref, s_a, m_a, s_b, m_b, acc_scr, l_scr = refs[-7:]
    t = pl.program_id(0)
    dv = vt_ref.shape[1]
    vrows = [slice(None)] * 2 if shared_values else [slice(0, dv // 2), slice(dv // 2, dv)]
    rows = dv if shared_values else dv // 2
    n_sub = s_a.shape[0]

    def logit_items(q_ref, s_buf, m_buf):
        for sub in range(n_sub):
            q = q_ref[sub * PV_TILE:(sub + 1) * PV_TILE, :]
            yield from _logit_chunks(q, kn_ref, s_buf.at[sub], m_buf.at[sub])

    def softmax_items(s_buf, m_buf):
        for sub in range(n_sub):
            yield from _softmax_pv_chunks(s_buf.at[sub], m_buf.at[sub], vt_ref, vrows,
                                          acc_scr.at[sub], l_scr.at[sub], key_chunk)

    @pl.when(t == 0)
    def _():
        for _ in logit_items(qf_ref, s_a, m_a):
            pass
        acc_scr[...] = jnp.zeros_like(acc_scr)
        l_scr[...] = jnp.ones_like(l_scr)

    def finish_previous():
        for sub in range(n_sub):
            result = [(acc_scr[sub, j, :rows, :], jnp.sum(l_scr[sub, j], axis=0, keepdims=True))
                      for j in range(2)]
            o_ref[sub * PV_TILE:(sub + 1) * PV_TILE, :] = finish(result, *extra).astype(o_ref.dtype)

    def step(nxt, cur):
        finish_previous()
        logits = logit_items(qn_ref, nxt[0], nxt[1])
        softmax = softmax_items(cur[0], cur[1])
        ratio = max((SEQ // key_chunk) * 2 // (SEQ // LOGIT_CHUNK), 1)
        for i, _ in enumerate(softmax):
            if i % ratio == 0:
                next(logits, None)
        for _ in logits:
            pass

    live = t < n_steps
    pl.when(live & ((t & 1) == 0))(lambda: step((s_b, m_b), (s_a, m_a)))
    pl.when(live & ((t & 1) == 1))(lambda: step((s_a, m_a), (s_b, m_b)))
    pl.when(t == n_steps)(finish_previous)


def _stream_attn(q, k, vt, extra, finish, *, groups, shared_values, key_chunk, name):
    tq = ATTN_ITEMS_PER_STEP * PV_TILE
    nq = SEQ // tq
    n_steps = BATCH * groups * nq
    qw = q.shape[1] // groups

    def rows(t):
        return (t // (groups * nq)) * nq + t % nq

    def group(t):
        return (t // nq) % groups

    def batch(t):
        return t // (groups * nq)

    nxt = lambda t: jnp.minimum(t + 1, n_steps - 1)
    cur = lambda t: jnp.minimum(t, n_steps - 1)
    prv = lambda t: jnp.maximum(t - 1, 0)
    in_specs = [
        pl.BlockSpec((tq, qw), lambda t: (0, 0)),
        pl.BlockSpec((tq, qw), lambda t: (rows(nxt(t)), group(nxt(t)))),
        pl.BlockSpec((SEQ, qw), lambda t: (batch(nxt(t)), group(nxt(t)))),
        pl.BlockSpec((1, LANES, SEQ), lambda t: (batch(cur(t)), group(cur(t)), 0)),
    ] + [_resident(a.shape) for a in extra]
    sbuf = pltpu.VMEM((ATTN_ITEMS_PER_STEP, 2, SEQ, PV_TILE), F32)
    mbuf = pltpu.VMEM((ATTN_ITEMS_PER_STEP, 2, SUBLANES, PV_TILE), F32)
    return pl.pallas_call(
        functools.partial(_stream_attn_kernel, n_steps=n_steps, shared_values=shared_values, finish=finish,
                          key_chunk=key_chunk),
        grid=(n_steps + 1,),
        in_specs=in_specs,
        out_specs=pl.BlockSpec((tq, LANES), lambda t: (rows(prv(t)), group(prv(t)))),
        out_shape=jax.ShapeDtypeStruct((TOKENS, groups * LANES), BF16),
        scratch_shapes=[sbuf, mbuf, sbuf, mbuf,
                        pltpu.VMEM((ATTN_ITEMS_PER_STEP, 2, LANES, PV_TILE), F32),
                        pltpu.VMEM((ATTN_ITEMS_PER_STEP, 2, SUBLANES, PV_TILE), F32)],
        compiler_params=_params("arbitrary"),
        name=name,
    )(q, q, k, vt, *extra)


def _diff_finish(result, lam_ref, g_ref, *, lam_init):
    (acc1, l1), (acc2, l2) = result
    lf = lam_ref[...]
    lam = (jnp.exp(jnp.sum(lf[0:1] * lf[1:2], axis=-1, keepdims=True))
           - jnp.exp(jnp.sum(lf[2:3] * lf[3:4], axis=-1, keepdims=True)) + lam_init)
    o = (acc1 / l1 - lam * (acc2 / l2)).T
    return _rms(o, g_ref[...]) * (1.0 - lam_init)


def _diff_attn(qa, ka, vt, lam_vec, subln_g, lam_init):
    return _stream_attn(qa, ka, vt, (lam_vec, subln_g),
                        functools.partial(_diff_finish, lam_init=lam_init),
                        groups=A_HEADS, shared_values=True, key_chunk=DIFF_KEY_CHUNK, name="diff_attn")


NA_QROWS = 4
NA_KROWS = NA_QROWS + NA_ROWS
NA_BLOCKS = GRID_ROWS // NA_QROWS
NA_PLACEMENTS = 3
NA_PAIR_UNROLL = 4


def _na_window_start(g):
    lo, hi = 0, GRID_ROWS - NA_KROWS
    s = g * NA_QROWS - NA_ROWS // 2
    if isinstance(g, int):
        return min(max(s, lo), hi)
    return jnp.clip(s, lo, hi)


def _na_attn_kernel(q_ref, k_ref, vt_ref, bias_ref, o_ref, s_a, m_a, s_b, m_b):
    nq = NA_QROWS * GRID_W
    nk = NA_KROWS * GRID_W
    groups8 = nk // SUBLANES
    lane = lax.broadcasted_iota(jnp.int32, (nq, 2 * B_DH), 1)

    def key_start(g):
        return pl.multiple_of(_na_window_start(g) * GRID_W, NA_QROWS * GRID_W)

    def logits(g, s_buf, m_buf):
        var = jnp.where(g == 0, 0, jnp.where(g == NA_BLOCKS - 1, 2, 1))
        q = q_ref[pl.ds(pl.multiple_of(g * nq, nq), nq), :]
        zero = jnp.zeros_like(q)
        q2 = jnp.concatenate([jnp.where(lane < B_DH, q, zero), jnp.where(lane >= B_DH, q, zero)], axis=0)
        s2 = lax.dot_general(k_ref[pl.ds(key_start(g), nk), :], q2, _NT, preferred_element_type=F32)
        for j in range(2):
            s = s2[:, j * nq:(j + 1) * nq] + bias_ref[j, var]
            s_buf[j] = s
            m = jnp.max(jnp.max(s.reshape(groups8, SUBLANES, nq), axis=0), axis=0, keepdims=True)
            m_buf[j] = jnp.broadcast_to(m, (SUBLANES, nq))

    def softmax_pv(g, s_buf, m_buf):
        outs = []
        for j in range(2):
            e = jnp.exp2(s_buf[j].reshape(groups8, SUBLANES, nq) - m_buf[j][None])
            l = jnp.sum(jnp.sum(e, axis=0), axis=0, keepdims=True)
            vt = vt_ref[0, j * B_DH:(j + 1) * B_DH, pl.ds(key_start(g), nk)]
            outs.append(_dot(vt, e.reshape(nk, nq).astype(BF16)) / l)
        o_ref[pl.ds(pl.multiple_of(g * nq, nq), nq), :] = jnp.concatenate(outs, axis=0).T.astype(BF16)

    logits(0, s_a, m_a)

    def pair(i, carry):
        g = 2 * i
        logits(g + 1, s_b, m_b)
        softmax_pv(g, s_a, m_a)
        logits(jnp.minimum(g + 2, NA_BLOCKS - 1), s_a, m_a)
        softmax_pv(g + 1, s_b, m_b)
        return carry

    lax.fori_loop(0, NA_BLOCKS // 2, pair, 0, unroll=NA_PAIR_UNROLL)


def _na_attn(qb, kb, vt, bias):
    hw = 2 * B_DH
    first_block = A_V // hw
    spec = pl.BlockSpec((SEQ, hw), lambda p, b: (b, p))
    vspec = pl.BlockSpec((1, hw, SEQ), lambda p, b: (b, first_block + p, 0))
    bspec = pl.BlockSpec((2,) + bias.shape[1:], lambda p, b: (p, 0, 0, 0))
    return pl.pallas_call(
        _na_attn_kernel,
        grid=(B_HEADS // 2, BATCH),
        in_specs=[spec, spec, vspec, bspec],
        out_specs=spec,
        out_shape=jax.ShapeDtypeStruct((TOKENS, B_W), BF16),
        scratch_shapes=[pltpu.VMEM((2,) + bias.shape[2:], F32), pltpu.VMEM((2, SUBLANES, bias.shape[3]), F32)] * 2,
        compiler_params=_params("parallel", "parallel"),
        name="na_attn",
    )(qb, kb, vt, bias)


def _na_bias_table(rpb):
    c = np.arange(GRID_W)
    c0 = np.clip(c - NA_COLS // 2, 0, GRID_W - NA_COLS)
    col_in = (c[None, :] >= c0[:, None]) & (c[None, :] < c0[:, None] + NA_COLS)
    dc = np.clip(c[None, :] - c[:, None], -(NA_COLS - 1), NA_COLS - 1) + NA_COLS - 1
    rpb = rpb.astype(F32) * LOG2E
    n_dr = 2 * NA_ROWS - 1
    toe = jnp.zeros((B_HEADS, GRID_W, n_dr, GRID_W), F32)
    for d in range(2 * NA_COLS - 1):
        toe = jnp.where((dc.T == d)[:, None, :], rpb[:, None, ::-1, d, None], toe)
    toe = jnp.where(col_in.T[:, None, :], toe, -jnp.inf).reshape(B_HEADS, GRID_W, n_dr * GRID_W)
    masked = lambda n: [jnp.full((B_HEADS, GRID_W, n * GRID_W), -jnp.inf, F32)] if n else []
    first_key_row = lambda qr: min(max(qr - NA_ROWS // 2, 0), GRID_ROWS - NA_ROWS)
    slabs = []
    for g in (0, 1, NA_BLOCKS - 1):
        for kl in range(NA_KROWS):
            kr = _na_window_start(g) + kl
            qrs = [g * NA_QROWS + ql for ql in range(NA_QROWS)]
            ok = [ql for ql, qr in enumerate(qrs) if first_key_row(qr) <= kr < first_key_row(qr) + NA_ROWS]
            if not ok:
                slabs.append(masked(NA_QROWS)[0])
                continue
            lo, hi = ok[0], ok[-1]
            assert ok == list(range(lo, hi + 1))
            first = n_dr - 1 - (kr - qrs[lo] + NA_ROWS - 1)
            own = toe[:, :, first * GRID_W:(first + hi - lo + 1) * GRID_W]
            slabs.append(jnp.concatenate(masked(lo) + [own] + masked(NA_QROWS - 1 - hi), axis=-1))
    return jnp.stack(slabs, axis=1).reshape(
        B_HEADS, NA_PLACEMENTS, NA_KROWS * GRID_W, NA_QROWS * GRID_W)


def _odd_in_kernel(x_ref, g_ref, w_ref, gq_ref, gkv_ref, wuq_ref, wk_ref, wvt_ref,
                   cos_ref, sin_ref, xc_ref, gc_ref, q_ref, k_ref, vt_ref):
    h = _rms(x_ref[...], g_ref[...]).astype(BF16)
    cos, sin = cos_ref[...], sin_ref[...]
    lane = lax.broadcasted_iota(jnp.int32, cos.shape, 1)
    lo = lane < D_NOPE + D_ROPE // 2
    scale = (D_NOPE + D_ROPE) ** -0.5 * LOG2E
    o = 0
    xc_ref[...] = _dot(h, w_ref[:, o:o + C_WIDTH]); o += C_WIDTH
    gc_ref[...] = _dot(h, w_ref[:, o:o + C_WIDTH]); o += C_WIDTH
    cq = _dot(h, w_ref[:, o:o + D_Q_RANK]); o += D_Q_RANK
    ckv = _dot(h, w_ref[:, o:o + D_KV_RANK]); o += D_KV_RANK
    kr = _dot(h, w_ref[:, o:o + PAD_DK])
    q = _dot(_rms(cq, gq_ref[...]).astype(BF16), wuq_ref[...])
    ckvn = _rms(ckv, gkv_ref[...]).astype(BF16)
    kn = _dot(ckvn, wk_ref[...])
    kpe = _rope_lanes(kr, cos, sin, D_ROPE // 2, lo)
    for j in range(D_HEADS):
        sl = slice(j * PAD_DK, (j + 1) * PAD_DK)
        q_ref[:, sl] = (_rope_lanes(q[:, sl], cos, sin, D_ROPE // 2, lo) * scale).astype(BF16)
        k_ref[:, sl] = (kn[:, sl] + kpe).astype(BF16)
    vt_ref[0] = lax.dot_general(wvt_ref[...], ckvn, _NT, preferred_element_type=F32).astype(BF16)


def _odd_in(x, g, w_in, gq, gkv, wuq, wk, wvt, cos, sin):
    tm = TOKEN_TILE
    per_seq = SEQ // tm
    tok = lambda w: pl.BlockSpec((tm, w), lambda i: (i, 0))
    pos = pl.BlockSpec((tm, LANES), lambda i: (i % per_seq, 0))
    sds = lambda w, dt: jax.ShapeDtypeStruct((TOKENS, w), dt)
    qw = D_HEADS * PAD_DK
    vw = D_HEADS * D_VDIM
    return pl.pallas_call(
        _odd_in_kernel,
        grid=(TOKENS // tm,),
        in_specs=[tok(D_MODEL)] + [_resident(a.shape) for a in (g, w_in, gq, gkv, wuq, wk, wvt)] + [pos, pos],
        out_specs=[tok(C_WIDTH), tok(C_WIDTH), tok(qw), tok(qw),
                   pl.BlockSpec((1, vw, tm), lambda i: (i // per_seq, 0, i % per_seq))],
        out_shape=[sds(C_WIDTH, F32), sds(C_WIDTH, F32), sds(qw, BF16), sds(qw, BF16),
                   jax.ShapeDtypeStruct((BATCH, vw, SEQ), BF16)],
        compiler_params=_params("parallel"),
        name="odd_in",
    )(x, g, w_in, gq, gkv, wuq, wk, wvt, cos, sin)


def _rglru_kernel(xc_ref, gc_ref, cw_ref, cb_ref, wgate_ref, bgate_ref, lam_ref, o_ref,
                  a_scr, b_scr, h_scr):
    cw = cw_ref[...]
    x = xc_ref[...]
    row = lax.broadcasted_iota(jnp.int32, x.shape, 0)

    def shifted(d):
        rolled = pltpu.roll(x, (-d) % SEQ, 0)
        valid = (row + d >= 0) & (row + d < SEQ)
        return jnp.where(valid, rolled, 0.0)

    lp = (CONV_W - 1) // 2
    u = cb_ref[...] + sum((x if j == lp else shifted(j - lp)) * cw[j:j + 1] for j in range(CONV_W))
    gates = _dot(u.astype(BF16), wgate_ref[0]) + bgate_ref[0]
    lam = lam_ref[0]
    for d in range(2):
        o = 2 * d * LANES
        r_t = _sigmoid(gates[:, o:o + LANES])
        i_t = _sigmoid(gates[:, o + LANES:o + 2 * LANES])
        log_a = (-RG_C * r_t) * jax.nn.softplus(-lam[:, d * LANES:(d + 1) * LANES])
        a = jnp.exp(log_a)
        y = -jnp.tanh(log_a) * (a * a + 1.0)
        mult = jnp.where(y > 0.0, y * lax.rsqrt(y), 0.0)
        mult = jnp.where(row == (SEQ - 1 if d else 0), 1.0, mult)
        a_scr[d] = a
        b_scr[d] = mult * i_t * u

    n_tiles = SEQ // SUBLANES
    srow = lax.broadcasted_iota(jnp.int32, (SUBLANES, LANES), 0)

    def tile_scan(a, b, h_prev, reverse):
        for d in (1, 2, 4):
            if reverse:
                keep = srow < SUBLANES - d
                shift = SUBLANES - d
            else:
                keep = srow >= d
                shift = d
            a_s = jnp.where(keep, pltpu.roll(a, shift, 0), 1.0)
            b_s = jnp.where(keep, pltpu.roll(b, shift, 0), 0.0)
            b = a * b_s + b
            a = a * a_s
        return a * h_prev + b

    def step(i, carry):
        hf, hr = carry
        tf = pl.multiple_of(i * SUBLANES, SUBLANES)
        h = tile_scan(a_scr[0, pl.ds(tf, SUBLANES), :], b_scr[0, pl.ds(tf, SUBLANES), :], hf, False)
        h_scr[0, pl.ds(tf, SUBLANES), :] = h
        hf = jnp.broadcast_to(h[SUBLANES - 1:SUBLANES, :], h.shape)
        tr = pl.multiple_of((n_tiles - 1 - i) * SUBLANES, SUBLANES)
        h = tile_scan(a_scr[1, pl.ds(tr, SUBLANES), :], b_scr[1, pl.ds(tr, SUBLANES), :], hr, True)
        h_scr[1, pl.ds(tr, SUBLANES), :] = h
        hr = jnp.broadcast_to(h[0:1, :], h.shape)
        return hf, hr

    z = jnp.zeros((SUBLANES, LANES), F32)
    lax.fori_loop(0, n_tiles, step, (z, z), unroll=SCAN_UNROLL)
    o_ref[...] = (jax.nn.gelu(gc_ref[...]) * (h_scr[0] + h_scr[1])).astype(BF16)


def _rglru(xc, gc, conv_w, conv_b, wgate, bgate, lam):
    ng = C_WIDTH // LANES
    seq = pl.BlockSpec((SEQ, LANES), lambda b, g: (b, g))
    grp = lambda a: pl.BlockSpec((1,) + a.shape[1:], lambda b, g: (g, 0, 0))
    return pl.pallas_call(
        _rglru_kernel,
        grid=(BATCH, ng),
        in_specs=[seq, seq,
                  pl.BlockSpec((CONV_W, LANES), lambda b, g: (0, g)),
                  pl.BlockSpec((1, LANES), lambda b, g: (0, g)),
                  grp(wgate), grp(bgate), grp(lam)],
        out_specs=seq,
        out_shape=jax.ShapeDtypeStruct((TOKENS, C_WIDTH), BF16),
        scratch_shapes=[pltpu.VMEM((2, SEQ, LANES), F32)] * 3,
        compiler_params=_params("parallel", "parallel"),
        name="rglru",
    )(xc, gc, conv_w, conv_b, wgate, bgate, lam)


def _rglru_gate_params(rg_wa, rg_ba, rg_wx, rg_bx, rg_lam):
    ng = C_WIDTH // LANES
    per = LANES // C_BW

    def dense(w):
        w = w.reshape(ng, per, C_BW, C_BW)
        eye = jnp.eye(per, dtype=w.dtype)
        return jnp.einsum('gpcd,pq->gpcqd', w, eye).reshape(ng, LANES, LANES)

    wgate = jnp.concatenate([dense(rg_wa[0]), dense(rg_wx[0]), dense(rg_wa[1]), dense(rg_wx[1])], axis=-1)
    grp = lambda v: v.reshape(ng, 1, LANES)
    bgate = jnp.concatenate([grp(rg_ba[0]), grp(rg_bx[0]), grp(rg_ba[1]), grp(rg_bx[1])], axis=-1)
    lam = jnp.concatenate([grp(rg_lam[0]), grp(rg_lam[1])], axis=-1)
    return wgate.astype(BF16), bgate.astype(F32), lam.astype(F32)


def _mla_finish(result):
    (acc1, l1), (acc2, l2) = result
    return jnp.concatenate([acc1 / l1, acc2 / l2], axis=0).T


def _mla_attn(q, k, vt):
    return _stream_attn(q, k, vt, (), _mla_finish, groups=D_HEADS // 2, shared_values=False,
                        key_chunk=MLA_KEY_CHUNK, name="mla_attn")


def _rope_angles(dim):
    inv = 1.0 / (ROPE_THETA ** (jnp.arange(0, dim, 2, dtype=F32) / dim))
    ang = jnp.arange(SEQ, dtype=F32)[:, None] * inv[None, :]
    return jnp.cos(ang), jnp.sin(ang)


def _even_rope_tables():
    cos, sin = _rope_angles(A_DH)
    reps = LANES // A_DH
    return (jnp.tile(jnp.concatenate([cos, cos], -1), (1, reps)),
            jnp.tile(jnp.concatenate([-sin, sin], -1), (1, reps)))


def _odd_rope_tables():
    cos, sin = _rope_angles(D_ROPE)
    ones = jnp.ones((SEQ, D_NOPE), F32)
    zn = jnp.zeros((SEQ, D_NOPE), F32)
    zp = jnp.zeros((SEQ, PAD_DK - D_NOPE - D_ROPE), F32)
    return (jnp.concatenate([ones, cos, cos, zp], -1),
            jnp.concatenate([zn, -sin, sin, zp], -1))


def _odd_weights(w_in, wuq, wukv):
    base = 2 * C_WIDTH + D_Q_RANK + D_KV_RANK
    zl = jnp.zeros((D_MODEL, D_NOPE), w_in.dtype)
    zr = jnp.zeros((D_MODEL, PAD_DK - D_NOPE - D_ROPE), w_in.dtype)
    w_in_p = jnp.concatenate([w_in[:, :base], zl, w_in[:, base:], zr], axis=-1)
    dqk = D_NOPE + D_ROPE
    wuq_p = jnp.pad(wuq.reshape(D_Q_RANK, D_HEADS, dqk), ((0, 0), (0, 0), (0, PAD_DK - dqk)))
    wukv_h = wukv.reshape(D_KV_RANK, D_HEADS, D_NOPE + D_VDIM)
    wk_p = jnp.pad(wukv_h[:, :, :D_NOPE], ((0, 0), (0, 0), (0, PAD_DK - D_NOPE)))
    wv = wukv_h[:, :, D_NOPE:]
    return (w_in_p.astype(BF16), wuq_p.reshape(D_Q_RANK, -1).astype(BF16),
            wk_p.reshape(D_KV_RANK, -1).astype(BF16), wv.reshape(D_KV_RANK, -1).T.astype(BF16))


def kernel(x, ffn1_norm, ffn1_wg, ffn1_wu, ffn1_wd, mix_norm, ffn2_norm, ffn2_wg, ffn2_wu, ffn2_wd, final_norm, ev_w_in, ev_w_out, diff_lam, diff_subln, na_rpb, od_w_in, od_w_out, conv_w, conv_b, rg_wa, rg_ba, rg_wx, rg_bx, rg_lam, mla_gq, mla_gkv, mla_wuq, mla_wukv):
    row = lambda v: v.reshape(1, -1).astype(F32)
    bf = lambda w: w.astype(BF16)
    xt = x.reshape(TOKENS, D_MODEL)

    ffn1 = (bf(ffn1_wg), bf(ffn1_wu), bf(ffn1_wd))
    ffn2 = (bf(ffn2_wg), bf(ffn2_wu), bf(ffn2_wd))

    xt = _ffn(xt, row(ffn1_norm[0]), *ffn1, 0)
    cos_a, sin_a = _even_rope_tables()
    wv_cols = jnp.concatenate([ev_w_in[0][:, 2 * A_QK:2 * A_QK + A_V], ev_w_in[0][:, EVEN_IN - B_W:]], axis=1)
    qa, ka, vt, qb, kb = _even_in(xt, row(mix_norm[0]), bf(ev_w_in[0]), bf(wv_cols.T), cos_a, sin_a)
    lam_init0 = 0.8 - 0.6 * math.exp(-0.3 * 0)
    oa = _diff_attn(qa, ka, vt, diff_lam[0].astype(F32), row(diff_subln[0]), lam_init0)
    ob = _na_attn(qb, kb, vt, _na_bias_table(na_rpb[0]))
    xt = _ffn(xt, row(ffn2_norm[0]), *ffn2, 0, mix=(oa, ob, bf(ev_w_out[0])))

    xt = _ffn(xt, row(ffn1_norm[1]), *ffn1, 1)
    cos_d, sin_d = _odd_rope_tables()
    w_in_p, wuq_p, wk_p, wvt_p = _odd_weights(od_w_in[0], mla_wuq[0], mla_wukv[0])
    xc, gc, q, k, vt = _odd_in(xt, row(mix_norm[1]), w_in_p, row(mla_gq[0]), row(mla_gkv[0]),
                               wuq_p, wk_p, wvt_p, cos_d, sin_d)
    wgate, bgate, lam = _rglru_gate_params(rg_wa[0], rg_ba[0], rg_wx[0], rg_bx[0], rg_lam[0])
    oc = _rglru(xc, gc, conv_w[0].astype(F32), row(conv_b[0]), wgate, bgate, lam)
    od = _mla_attn(q, k, vt)
    xt = _ffn(xt, row(ffn2_norm[1]), *ffn2, 1, mix=(oc, od, bf(od_w_out[0])), final_g=row(final_norm))
    return xt.reshape(BATCH, SEQ, D_MODEL)
```

```python
import functools
import math

import jax
import jax.numpy as jnp
import numpy as np
from jax import lax
from jax.experimental import pallas as pl
from jax.experimental.pallas import tpu as pltpu

F32 = jnp.float32
BF16 = jnp.bfloat16

D_MODEL = 1024
BATCH = 4
SEQ = 4096
DEPTH = 2
TOKENS = BATCH * SEQ
RMS_EPS = 1e-6
ROPE_THETA = 10000.0
GRID_W = 64
GRID_ROWS = SEQ // GRID_W
D_FF = 2816

A_HEADS = 4
A_DH = 64
A_QK = A_HEADS * 2 * A_DH
A_V = A_HEADS * 2 * A_DH
B_HEADS = 8
B_DH = 64
B_W = B_HEADS * B_DH
NA_ROWS = 8
NA_COLS = 16
C_WIDTH = 512
C_BLOCKS = 8
C_BW = C_WIDTH // C_BLOCKS
CONV_W = 4
RG_C = 8.0
D_HEADS = 8
D_NOPE = 64
D_ROPE = 32
D_VDIM = 64
D_Q_RANK = 256
D_KV_RANK = 128
EVEN_IN = 2 * A_QK + A_V + 3 * B_W

LANES = 128
SUBLANES = 8
VMEM_LIMIT_BYTES = 56 * 1024 * 1024

TOKEN_TILE = 1024
FFN_TILE = 1024
FF_CHUNK = 256
PV_TILE = 256
ATTN_ITEMS_PER_STEP = 2
LOGIT_CHUNK = 1024
DIFF_KEY_CHUNK = 1024
MLA_KEY_CHUNK = 2048
PAD_DK = 128
SCAN_UNROLL = 16

_NT = (((1,), (1,)), ((), ()))

LOG2E = math.log2(math.e)


def _params(*sem):
    return pltpu.CompilerParams(dimension_semantics=sem, vmem_limit_bytes=VMEM_LIMIT_BYTES)


def _resident(shape):
    nd = len(shape)
    return pl.BlockSpec(shape, lambda *_: (0,) * nd, pipeline_mode=pl.Buffered(1))


def _rms(x, g):
    return x * lax.rsqrt(jnp.mean(x * x, axis=-1, keepdims=True) + RMS_EPS) * g


def _dot(a, b):
    return jnp.dot(a, b, preferred_element_type=F32)


def _sigmoid(x):
    return 0.5 * (jnp.tanh(0.5 * x) + 1.0)


def _rope_lanes(x, cos, sin_signed, half, lo_mask):
    n = x.shape[-1]
    partner = jnp.where(lo_mask, pltpu.roll(x, n - half, 1), pltpu.roll(x, half, 1))
    return x * cos + partner * sin_signed


def _ffn_kernel(*refs, has_mix, has_final):
    it = iter(refs)
    x_ref = next(it)
    if has_mix:
        o1_ref, o2_ref, wo_ref = next(it), next(it), next(it)
    g_ref, wg_ref, wu_ref, wd_ref = next(it), next(it), next(it), next(it)
    gf_ref = next(it) if has_final else None
    y_ref = next(it)
    acc_ref = next(it)

    x = x_ref[...]
    if has_mix:
        half = wo_ref.shape[0] // 2
        x = x + _dot(o1_ref[...], wo_ref[:half, :]) + _dot(o2_ref[...], wo_ref[half:, :])
    n = _rms(x, g_ref[...]).astype(BF16)
    for c in range(D_FF // FF_CHUNK):
        sl = slice(c * FF_CHUNK, (c + 1) * FF_CHUNK)
        gate = _dot(n, wg_ref[:, sl])
        up = _dot(n, wu_ref[:, sl])
        h = (gate * jax.nn.sigmoid(gate) * up).astype(BF16)
        d = _dot(h, wd_ref[sl, :])
        if c == 0:
            acc_ref[...] = d
        else:
            acc_ref[...] += d
    y = x + 0.5 * acc_ref[...]
    if has_final:
        y = _rms(y, gf_ref[...])
    y_ref[...] = y


def _ffn(x, g, wg, wu, wd, layer, mix=None, final_g=None):
    tm = FFN_TILE
    tok = lambda w: pl.BlockSpec((tm, w), lambda i: (i, 0))
    slab = lambda a: pl.BlockSpec((None,) + a.shape[1:], lambda i: (layer, 0, 0),
                                  pipeline_mode=pl.Buffered(1))
    args, specs = [x], [tok(D_MODEL)]
    if mix is not None:
        o1, o2, wo = mix
        args += [o1, o2, wo]
        specs += [tok(o1.shape[1]), tok(o2.shape[1]), _resident(wo.shape)]
    args += [g, wg, wu, wd]
    specs += [_resident(g.shape), slab(wg), slab(wu), slab(wd)]
    if final_g is not None:
        args.append(final_g)
        specs.append(_resident(final_g.shape))
    return pl.pallas_call(
        functools.partial(_ffn_kernel, has_mix=mix is not None, has_final=final_g is not None),
        grid=(TOKENS // tm,),
        in_specs=specs,
        out_specs=tok(D_MODEL),
        out_shape=jax.ShapeDtypeStruct((TOKENS, D_MODEL), F32),
        scratch_shapes=[pltpu.VMEM((tm, D_MODEL), F32)],
        compiler_params=_params("parallel"),
        name="ffn_mix" if mix is not None else "ffn",
    )(*args)


def _even_in_kernel(x_ref, g_ref, w_ref, wvt_ref, cos_ref, sin_ref,
                    qa_ref, ka_ref, vt_ref, qb_ref, kb_ref):
    h = _rms(x_ref[...], g_ref[...]).astype(BF16)
    cos, sin = cos_ref[...], sin_ref[...]
    lane = lax.broadcasted_iota(jnp.int32, cos.shape, 1)
    lo = (lane % A_DH) < (A_DH // 2)
    scale = A_DH ** -0.5 * LOG2E

    def proj(i):
        return _dot(h, w_ref[:, i * A_QK:(i + 1) * A_QK])

    qa, ka = proj(0), proj(1)
    for j in range(A_QK // LANES):
        sl = slice(j * LANES, (j + 1) * LANES)
        qa_ref[:, sl] = (_rope_lanes(qa[:, sl], cos, sin, A_DH // 2, lo) * scale).astype(BF16)
        ka_ref[:, sl] = _rope_lanes(ka[:, sl], cos, sin, A_DH // 2, lo).astype(BF16)
    vt_ref[0] = lax.dot_general(wvt_ref[...], h, _NT, preferred_element_type=F32).astype(BF16)
    qb_ref[...] = (proj(3) * (B_DH ** -0.5 * LOG2E)).astype(BF16)
    kb_ref[...] = proj(4).astype(BF16)


def _even_in(x, g, w_in, wvt, cos, sin):
    tm = TOKEN_TILE
    per_seq = SEQ // tm
    tok = lambda w: pl.BlockSpec((tm, w), lambda i: (i, 0))
    pos = pl.BlockSpec((tm, LANES), lambda i: (i % per_seq, 0))
    vw = wvt.shape[0]
    tr = pl.BlockSpec((1, vw, tm), lambda i: (i // per_seq, 0, i % per_seq))
    out = jax.ShapeDtypeStruct((TOKENS, A_QK), BF16)
    out_t = jax.ShapeDtypeStruct((BATCH, vw, SEQ), BF16)
    return pl.pallas_call(
        _even_in_kernel,
        grid=(TOKENS // tm,),
        in_specs=[tok(D_MODEL), _resident(g.shape), _resident(w_in.shape), _resident(wvt.shape), pos, pos],
        out_specs=[tok(A_QK), tok(A_QK), tr, tok(B_W), tok(B_W)],
        out_shape=[out, out, out_t, out, out],
        compiler_params=_params("parallel"),
        name="even_in",
    )(x, g, w_in, wvt, cos, sin)


def _logit_chunks(q, k_ref, s_buf, m_buf):
    tq, qw = q.shape
    lane = lax.broadcasted_iota(jnp.int32, q.shape, 1)
    zero = jnp.zeros_like(q)
    q2 = jnp.concatenate([jnp.where(lane < qw // 2, q, zero), jnp.where(lane >= qw // 2, q, zero)], axis=0)
    groups8 = LOGIT_CHUNK // SUBLANES
    mrun = [None, None]
    for c in range(SEQ // LOGIT_CHUNK):
        keys = slice(c * LOGIT_CHUNK, (c + 1) * LOGIT_CHUNK)
        s2 = lax.dot_general(k_ref[keys, :], q2, _NT, preferred_element_type=F32)
        for j in range(2):
            s = s2[:, j * tq:(j + 1) * tq]
            s_buf[j, keys, :] = s
            part = jnp.max(s.reshape(groups8, SUBLANES, tq), axis=0)
            mrun[j] = part if mrun[j] is None else jnp.maximum(mrun[j], part)
        yield
    for j in range(2):
        m_buf[j] = jnp.broadcast_to(jnp.max(mrun[j], axis=0, keepdims=True), (SUBLANES, tq))


def _softmax_pv_chunks(s_buf, m_buf, vt_ref, vrows, acc_scr, l_scr, key_chunk):
    tq = s_buf.shape[-1]
    groups8 = key_chunk // SUBLANES
    m = [m_buf[j][None] for j in range(2)]
    acc = [None, None]
    lrun = [None, None]
    for c in range(SEQ // key_chunk):
        keys = slice(c * key_chunk, (c + 1) * key_chunk)
        for j in range(2):
            e = jnp.exp2(s_buf[j, keys, :].reshape(groups8, SUBLANES, tq) - m[j])
            part = jnp.sum(e, axis=0)
            lrun[j] = part if lrun[j] is None else lrun[j] + part
            pv = _dot(vt_ref[0, vrows[j], keys], e.reshape(key_chunk, tq).astype(BF16))
            acc[j] = pv if acc[j] is None else acc[j] + pv
            yield
    for j in range(2):
        acc_scr[j, :acc[j].shape[0], :] = acc[j]
        l_scr[j] = lrun[j]


def _stream_attn_kernel(*refs, n_steps, shared_values, finish, key_chunk):
    qf_ref, qn_ref, kn_ref, vt_ref = refs[:4]
    extra = refs[4:-7]
    o_ref, s_a, m_a, s_b, m_b, acc_scr, l_scr = refs[-7:]
    t = pl.program_id(0)
    dv = vt_ref.shape[1]
    vrows = [slice(None)] * 2 if shared_values else [slice(0, dv // 2), slice(dv // 2, dv)]
    rows = dv if shared_values else dv // 2
    n_sub = s_a.shape[0]

    def logit_items(q_ref, s_buf, m_buf):
        for sub in range(n_sub):
            q = q_ref[sub * PV_TILE:(sub + 1) * PV_TILE, :]
            yield from _logit_chunks(q, kn_ref, s_buf.at[sub], m_buf.at[sub])

    def softmax_items(s_buf, m_buf):
        for sub in range(n_sub):
            yield from _softmax_pv_chunks(s_buf.at[sub], m_buf.at[sub], vt_ref, vrows,
                                          acc_scr.at[sub], l_scr.at[sub], key_chunk)

    @pl.when(t == 0)
    def _():
        for _ in logit_items(qf_ref, s_a, m_a):
            pass
        acc_scr[...] = jnp.zeros_like(acc_scr)
        l_scr[...] = jnp.ones_like(l_scr)

    def finish_previous():
        for sub in range(n_sub):
            result = [(acc_scr[sub, j, :rows, :], jnp.sum(l_scr[sub, j], axis=0, keepdims=True))
                      for j in range(2)]
            o_ref[sub * PV_TILE:(sub + 1) * PV_TILE, :] = finish(result, *extra).astype(o_ref.dtype)

    def step(nxt, cur):
        finish_previous()
        logits = logit_items(qn_ref, nxt[0], nxt[1])
        softmax = softmax_items(cur[0], cur[1])
        ratio = max((SEQ // key_chunk) * 2 // (SEQ // LOGIT_CHUNK), 1)
        for i, _ in enumerate(softmax):
            if i % ratio == 0:
                next(logits, None)
        for _ in logits:
            pass

    live = t < n_steps
    pl.when(live & ((t & 1) == 0))(lambda: step((s_b, m_b), (s_a, m_a)))
    pl.when(live & ((t & 1) == 1))(lambda: step((s_a, m_a), (s_b, m_b)))
    pl.when(t == n_steps)(finish_previous)


def _stream_attn(q, k, vt, extra, finish, *, groups, shared_values, key_chunk, name):
    tq = ATTN_ITEMS_PER_STEP * PV_TILE
    nq = SEQ // tq
    n_steps = BATCH * groups * nq
    qw = q.shape[1] // groups

    def rows(t):
        return (t // (groups * nq)) * nq + t % nq

    def group(t):
        return (t // nq) % groups

    def batch(t):
        return t // (groups * nq)

    nxt = lambda t: jnp.minimum(t + 1, n_steps - 1)
    cur = lambda t: jnp.minimum(t, n_steps - 1)
    prv = lambda t: jnp.maximum(t - 1, 0)
    in_specs = [
        pl.BlockSpec((tq, qw), lambda t: (0, 0)),
        pl.BlockSpec((tq, qw), lambda t: (rows(nxt(t)), group(nxt(t)))),
        pl.BlockSpec((SEQ, qw), lambda t: (batch(nxt(t)), group(nxt(t)))),
        pl.BlockSpec((1, LANES, SEQ), lambda t: (batch(cur(t)), group(cur(t)), 0)),
    ] + [_resident(a.shape) for a in extra]
    sbuf = pltpu.VMEM((ATTN_ITEMS_PER_STEP, 2, SEQ, PV_TILE), F32)
    mbuf = pltpu.VMEM((ATTN_ITEMS_PER_STEP, 2, SUBLANES, PV_TILE), F32)
    return pl.pallas_call(
        functools.partial(_stream_attn_kernel, n_steps=n_steps, shared_values=shared_values, finish=finish,
                          key_chunk=key_chunk),
        grid=(n_steps + 1,),
        in_specs=in_specs,
        out_specs=pl.BlockSpec((tq, LANES), lambda t: (rows(prv(t)), group(prv(t)))),
        out_shape=jax.ShapeDtypeStruct((TOKENS, groups * LANES), BF16),
        scratch_shapes=[sbuf, mbuf, sbuf, mbuf,
                        pltpu.VMEM((ATTN_ITEMS_PER_STEP, 2, LANES, PV_TILE), F32),
                        pltpu.VMEM((ATTN_ITEMS_PER_STEP, 2, SUBLANES, PV_TILE), F32)],
        compiler_params=_params("arbitrary"),
        name=name,
    )(q, q, k, vt, *extra)


def _diff_finish(result, lam_ref, g_ref, *, lam_init):
    (acc1, l1), (acc2, l2) = result
    lf = lam_ref[...]
    lam = (jnp.exp(jnp.sum(lf[0:1] * lf[1:2], axis=-1, keepdims=True))
           - jnp.exp(jnp.sum(lf[2:3] * lf[3:4], axis=-1, keepdims=True)) + lam_init)
    o = (acc1 / l1 - lam * (acc2 / l2)).T
    return _rms(o, g_ref[...]) * (1.0 - lam_init)


def _diff_attn(qa, ka, vt, lam_vec, subln_g, lam_init):
    return _stream_attn(qa, ka, vt, (lam_vec, subln_g),
                        functools.partial(_diff_finish, lam_init=lam_init),
                        groups=A_HEADS, shared_values=True, key_chunk=DIFF_KEY_CHUNK, name="diff_attn")


NA_QROWS = 2
NA_KROWS = NA_QROWS + NA_ROWS
NA_BLOCKS = GRID_ROWS // NA_QROWS
NA_EDGE = -(-(NA_ROWS // 2) // NA_QROWS)
NA_PLACEMENTS = 2 * NA_EDGE + 1
NA_PAIR_UNROLL = 8


def _na_window_start(g):
    lo, hi = 0, GRID_ROWS - NA_KROWS
    s = g * NA_QROWS - NA_ROWS // 2
    if isinstance(g, int):
        return min(max(s, lo), hi)
    return jnp.clip(s, lo, hi)


def _na_attn_kernel(q_ref, k_ref, vt_ref, bias_ref, o_ref, s_a, m_a, s_b, m_b):
    nq = NA_QROWS * GRID_W
    nk = NA_KROWS * GRID_W
    groups8 = nk // SUBLANES
    lane = lax.broadcasted_iota(jnp.int32, (nq, 2 * B_DH), 1)

    def key_start(g):
        return pl.multiple_of(_na_window_start(g) * GRID_W, NA_QROWS * GRID_W)

    def logits(g, s_buf, m_buf):
        var = jnp.where(g < NA_EDGE, g,
                        jnp.where(g >= NA_BLOCKS - NA_EDGE, g - (NA_BLOCKS - NA_PLACEMENTS), NA_EDGE))
        q = q_ref[pl.ds(pl.multiple_of(g * nq, nq), nq), :]
        zero = jnp.zeros_like(q)
        q2 = jnp.concatenate([jnp.where(lane < B_DH, q, zero), jnp.where(lane >= B_DH, q, zero)], axis=0)
        s2 = lax.dot_general(k_ref[pl.ds(key_start(g), nk), :], q2, _NT, preferred_element_type=F32)
        for j in range(2):
            s = s2[:, j * nq:(j + 1) * nq] + bias_ref[j, var]
            s_buf[j] = s
            m = jnp.max(jnp.max(s.reshape(groups8, SUBLANES, nq), axis=0), axis=0, keepdims=True)
            m_buf[j] = jnp.broadcast_to(m, (SUBLANES, nq))

    def softmax_pv(g, s_buf, m_buf):
        outs = []
        for j in range(2):
            e = jnp.exp2(s_buf[j].reshape(groups8, SUBLANES, nq) - m_buf[j][None])
            l = jnp.sum(jnp.sum(e, axis=0), axis=0, keepdims=True)
            vt = vt_ref[0, j * B_DH:(j + 1) * B_DH, pl.ds(key_start(g), nk)]
            outs.append(_dot(vt, e.reshape(nk, nq).astype(BF16)) / l)
        o_ref[pl.ds(pl.multiple_of(g * nq, nq), nq), :] = jnp.concatenate(outs, axis=0).T.astype(BF16)

    logits(0, s_a, m_a)

    def pair(i, carry):
        g = 2 * i
        logits(g + 1, s_b, m_b)
        softmax_pv(g, s_a, m_a)
        logits(jnp.minimum(g + 2, NA_BLOCKS - 1), s_a, m_a)
        softmax_pv(g + 1, s_b, m_b)
        return carry

    lax.fori_loop(0, NA_BLOCKS // 2, pair, 0, unroll=NA_PAIR_UNROLL)


def _na_attn(qb, kb, vt, bias):
    hw = 2 * B_DH
    first_block = A_V // hw
    spec = pl.BlockSpec((SEQ, hw), lambda p, b: (b, p))
    vspec = pl.BlockSpec((1, hw, SEQ), lambda p, b: (b, first_block + p, 0))
    bspec = pl.BlockSpec((2,) + bias.shape[1:], lambda p, b: (p, 0, 0, 0))
    return pl.pallas_call(
        _na_attn_kernel,
        grid=(B_HEADS // 2, BATCH),
        in_specs=[spec, spec, vspec, bspec],
        out_specs=spec,
        out_shape=jax.ShapeDtypeStruct((TOKENS, B_W), BF16),
        scratch_shapes=[pltpu.VMEM((2,) + bias.shape[2:], F32), pltpu.VMEM((2, SUBLANES, bias.shape[3]), F32)] * 2,
        compiler_params=_params("parallel", "parallel"),
        name="na_attn",
    )(qb, kb, vt, bias)


def _na_bias_table(rpb):
    c = np.arange(GRID_W)
    c0 = np.clip(c - NA_COLS // 2, 0, GRID_W - NA_COLS)
    col_in = (c[None, :] >= c0[:, None]) & (c[None, :] < c0[:, None] + NA_COLS)
    dc = np.clip(c[None, :] - c[:, None], -(NA_COLS - 1), NA_COLS - 1) + NA_COLS - 1
    rpb = rpb.astype(F32) * LOG2E
    n_dr = 2 * NA_ROWS - 1
    toe = jnp.zeros((B_HEADS, GRID_W, n_dr, GRID_W), F32)
    for d in range(2 * NA_COLS - 1):
        toe = jnp.where((dc.T == d)[:, None, :], rpb[:, None, ::-1, d, None], toe)
    toe = jnp.where(col_in.T[:, None, :], toe, -jnp.inf).reshape(B_HEADS, GRID_W, n_dr * GRID_W)
    masked = lambda n: [jnp.full((B_HEADS, GRID_W, n * GRID_W), -jnp.inf, F32)] if n else []
    first_key_row = lambda qr: min(max(qr - NA_ROWS // 2, 0), GRID_ROWS - NA_ROWS)
    slabs = []
    for g in [*range(NA_EDGE + 1), *range(NA_BLOCKS - NA_EDGE, NA_BLOCKS)]:
        for kl in range(NA_KROWS):
            kr = _na_window_start(g) + kl
            qrs = [g * NA_QROWS + ql for ql in range(NA_QROWS)]
            ok = [ql for ql, qr in enumerate(qrs) if first_key_row(qr) <= kr < first_key_row(qr) + NA_ROWS]
            if not ok:
                slabs.append(masked(NA_QROWS)[0])
                continue
            lo, hi = ok[0], ok[-1]
            assert ok == list(range(lo, hi + 1))
            first = n_dr - 1 - (kr - qrs[lo] + NA_ROWS - 1)
            own = toe[:, :, first * GRID_W:(first + hi - lo + 1) * GRID_W]
            slabs.append(jnp.concatenate(masked(lo) + [own] + masked(NA_QROWS - 1 - hi), axis=-1))
    return jnp.stack(slabs, axis=1).reshape(
        B_HEADS, NA_PLACEMENTS, NA_KROWS * GRID_W, NA_QROWS * GRID_W)


def _odd_in_kernel(x_ref, g_ref, w_ref, gq_ref, gkv_ref, wuq_ref, wk_ref, wvt_ref,
                   cos_ref, sin_ref, xc_ref, gc_ref, q_ref, k_ref, vt_ref):
    h = _rms(x_ref[...], g_ref[...]).astype(BF16)
    cos, sin = cos_ref[...], sin_ref[...]
    lane = lax.broadcasted_iota(jnp.int32, cos.shape, 1)
    lo = lane < D_NOPE + D_ROPE // 2
    scale = (D_NOPE + D_ROPE) ** -0.5 * LOG2E
    o = 0
    xc_ref[...] = _dot(h, w_ref[:, o:o + C_WIDTH]); o += C_WIDTH
    gc_ref[...] = _dot(h, w_ref[:, o:o + C_WIDTH]); o += C_WIDTH
    cq = _dot(h, w_ref[:, o:o + D_Q_RANK]); o += D_Q_RANK
    ckv = _dot(h, w_ref[:, o:o + D_KV_RANK]); o += D_KV_RANK
    kr = _dot(h, w_ref[:, o:o + PAD_DK])
    q = _dot(_rms(cq, gq_ref[...]).astype(BF16), wuq_ref[...])
    ckvn = _rms(ckv, gkv_ref[...]).astype(BF16)
    kn = _dot(ckvn, wk_ref[...])
    kpe = _rope_lanes(kr, cos, sin, D_ROPE // 2, lo)
    for j in range(D_HEADS):
        sl = slice(j * PAD_DK, (j + 1) * PAD_DK)
        q_ref[:, sl] = (_rope_lanes(q[:, sl], cos, sin, D_ROPE // 2, lo) * scale).astype(BF16)
        k_ref[:, sl] = (kn[:, sl] + kpe).astype(BF16)
    vt_ref[0] = lax.dot_general(wvt_ref[...], ckvn, _NT, preferred_element_type=F32).astype(BF16)


def _odd_in(x, g, w_in, gq, gkv, wuq, wk, wvt, cos, sin):
    tm = TOKEN_TILE
    per_seq = SEQ // tm
    tok = lambda w: pl.BlockSpec((tm, w), lambda i: (i, 0))
    pos = pl.BlockSpec((tm, LANES), lambda i: (i % per_seq, 0))
    sds = lambda w, dt: jax.ShapeDtypeStruct((TOKENS, w), dt)
    qw = D_HEADS * PAD_DK
    vw = D_HEADS * D_VDIM
    return pl.pallas_call(
        _odd_in_kernel,
        grid=(TOKENS // tm,),
        in_specs=[tok(D_MODEL)] + [_resident(a.shape) for a in (g, w_in, gq, gkv, wuq, wk, wvt)] + [pos, pos],
        out_specs=[tok(C_WIDTH), tok(C_WIDTH), tok(qw), tok(qw),
                   pl.BlockSpec((1, vw, tm), lambda i: (i // per_seq, 0, i % per_seq))],
        out_shape=[sds(C_WIDTH, F32), sds(C_WIDTH, F32), sds(qw, BF16), sds(qw, BF16),
                   jax.ShapeDtypeStruct((BATCH, vw, SEQ), BF16)],
        compiler_params=_params("parallel"),
        name="odd_in",
    )(x, g, w_in, gq, gkv, wuq, wk, wvt, cos, sin)


def _rglru_kernel(xc_ref, gc_ref, cw_ref, cb_ref, wgate_ref, bgate_ref, lam_ref, o_ref,
                  a_scr, b_scr, h_scr):
    cw = cw_ref[...]
    x = xc_ref[...]
    row = lax.broadcasted_iota(jnp.int32, x.shape, 0)

    def shifted(d):
        rolled = pltpu.roll(x, (-d) % SEQ, 0)
        valid = (row + d >= 0) & (row + d < SEQ)
        return jnp.where(valid, rolled, 0.0)

    lp = (CONV_W - 1) // 2
    u = cb_ref[...] + sum((x if j == lp else shifted(j - lp)) * cw[j:j + 1] for j in range(CONV_W))
    gates = _dot(u.astype(BF16), wgate_ref[0]) + bgate_ref[0]
    lam = lam_ref[0]
    for d in range(2):
        o = 2 * d * LANES
        r_t = _sigmoid(gates[:, o:o + LANES])
        i_t = _sigmoid(gates[:, o + LANES:o + 2 * LANES])
        log_a = (-RG_C * r_t) * jax.nn.softplus(-lam[:, d * LANES:(d + 1) * LANES])
        a = jnp.exp(log_a)
        y = -jnp.tanh(log_a) * (a * a + 1.0)
        mult = jnp.where(y > 0.0, y * lax.rsqrt(y), 0.0)
        mult = jnp.where(row == (SEQ - 1 if d else 0), 1.0, mult)
        a_scr[d] = a
        b_scr[d] = mult * i_t * u

    n_tiles = SEQ // SUBLANES
    srow = lax.broadcasted_iota(jnp.int32, (SUBLANES, LANES), 0)

    def tile_scan(a, b, h_prev, reverse):
        for d in (1, 2, 4):
            if reverse:
                keep = srow < SUBLANES - d
                shift = SUBLANES - d
            else:
                keep = srow >= d
                shift = d
            a_s = jnp.where(keep, pltpu.roll(a, shift, 0), 1.0)
            b_s = jnp.where(keep, pltpu.roll(b, shift, 0), 0.0)
            b = a * b_s + b
            a = a * a_s
        return a * h_prev + b

    def step(i, carry):
        hf, hr = carry
        tf = pl.multiple_of(i * SUBLANES, SUBLANES)
        h = tile_scan(a_scr[0, pl.ds(tf, SUBLANES), :], b_scr[0, pl.ds(tf, SUBLANES), :], hf, False)
        h_scr[0, pl.ds(tf, SUBLANES), :] = h
        hf = jnp.broadcast_to(h[SUBLANES - 1:SUBLANES, :], h.shape)
        tr = pl.multiple_of((n_tiles - 1 - i) * SUBLANES, SUBLANES)
        h = tile_scan(a_scr[1, pl.ds(tr, SUBLANES), :], b_scr[1, pl.ds(tr, SUBLANES), :], hr, True)
        h_scr[1, pl.ds(tr, SUBLANES), :] = h
        hr = jnp.broadcast_to(h[0:1, :], h.shape)
        return hf, hr

    z = jnp.zeros((SUBLANES, LANES), F32)
    lax.fori_loop(0, n_tiles, step, (z, z), unroll=SCAN_UNROLL)
    o_ref[...] = (jax.nn.gelu(gc_ref[...]) * (h_scr[0] + h_scr[1])).astype(BF16)


def _rglru(xc, gc, conv_w, conv_b, wgate, bgate, lam):
    ng = C_WIDTH // LANES
    seq = pl.BlockSpec((SEQ, LANES), lambda b, g: (b, g))
    grp = lambda a: pl.BlockSpec((1,) + a.shape[1:], lambda b, g: (g, 0, 0))
    return pl.pallas_call(
        _rglru_kernel,
        grid=(BATCH, ng),
        in_specs=[seq, seq,
                  pl.BlockSpec((CONV_W, LANES), lambda b, g: (0, g)),
                  pl.BlockSpec((1, LANES), lambda b, g: (0, g)),
                  grp(wgate), grp(bgate), grp(lam)],
        out_specs=seq,
        out_shape=jax.ShapeDtypeStruct((TOKENS, C_WIDTH), BF16),
        scratch_shapes=[pltpu.VMEM((2, SEQ, LANES), F32)] * 3,
        compiler_params=_params("parallel", "parallel"),
        name="rglru",
    )(xc, gc, conv_w, conv_b, wgate, bgate, lam)


def _rglru_gate_params(rg_wa, rg_ba, rg_wx, rg_bx, rg_lam):
    ng = C_WIDTH // LANES
    per = LANES // C_BW

    def dense(w):
        w = w.reshape(ng, per, C_BW, C_BW)
        eye = jnp.eye(per, dtype=w.dtype)
        return jnp.einsum('gpcd,pq->gpcqd', w, eye).reshape(ng, LANES, LANES)

    wgate = jnp.concatenate([dense(rg_wa[0]), dense(rg_wx[0]), dense(rg_wa[1]), dense(rg_wx[1])], axis=-1)
    grp = lambda v: v.reshape(ng, 1, LANES)
    bgate = jnp.concatenate([grp(rg_ba[0]), grp(rg_bx[0]), grp(rg_ba[1]), grp(rg_bx[1])], axis=-1)
    lam = jnp.concatenate([grp(rg_lam[0]), grp(rg_lam[1])], axis=-1)
    return wgate.astype(BF16), bgate.astype(F32), lam.astype(F32)


def _mla_finish(result):
    (acc1, l1), (acc2, l2) = result
    return jnp.concatenate([acc1 / l1, acc2 / l2], axis=0).T


def _mla_attn(q, k, vt):
    return _stream_attn(q, k, vt, (), _mla_finish, groups=D_HEADS // 2, shared_values=False,
                        key_chunk=MLA_KEY_CHUNK, name="mla_attn")


def _rope_angles(dim):
    inv = 1.0 / (ROPE_THETA ** (jnp.arange(0, dim, 2, dtype=F32) / dim))
    ang = jnp.arange(SEQ, dtype=F32)[:, None] * inv[None, :]
    return jnp.cos(ang), jnp.sin(ang)


def _even_rope_tables():
    cos, sin = _rope_angles(A_DH)
    reps = LANES // A_DH
    return (jnp.tile(jnp.concatenate([cos, cos], -1), (1, reps)),
            jnp.tile(jnp.concatenate([-sin, sin], -1), (1, reps)))


def _odd_rope_tables():
    cos, sin = _rope_angles(D_ROPE)
    ones = jnp.ones((SEQ, D_NOPE), F32)
    zn = jnp.zeros((SEQ, D_NOPE), F32)
    zp = jnp.zeros((SEQ, PAD_DK - D_NOPE - D_ROPE), F32)
    return (jnp.concatenate([ones, cos, cos, zp], -1),
            jnp.concatenate([zn, -sin, sin, zp], -1))


def _odd_weights(w_in, wuq, wukv):
    base = 2 * C_WIDTH + D_Q_RANK + D_KV_RANK
    zl = jnp.zeros((D_MODEL, D_NOPE), w_in.dtype)
    zr = jnp.zeros((D_MODEL, PAD_DK - D_NOPE - D_ROPE), w_in.dtype)
    w_in_p = jnp.concatenate([w_in[:, :base], zl, w_in[:, base:], zr], axis=-1)
    dqk = D_NOPE + D_ROPE
    wuq_p = jnp.pad(wuq.reshape(D_Q_RANK, D_HEADS, dqk), ((0, 0), (0, 0), (0, PAD_DK - dqk)))
    wukv_h = wukv.reshape(D_KV_RANK, D_HEADS, D_NOPE + D_VDIM)
    wk_p = jnp.pad(wukv_h[:, :, :D_NOPE], ((0, 0), (0, 0), (0, PAD_DK - D_NOPE)))
    wv = wukv_h[:, :, D_NOPE:]
    return (w_in_p.astype(BF16), wuq_p.reshape(D_Q_RANK, -1).astype(BF16),
            wk_p.reshape(D_KV_RANK, -1).astype(BF16), wv.reshape(D_KV_RANK, -1).T.astype(BF16))


def kernel(x, ffn1_norm, ffn1_wg, ffn1_wu, ffn1_wd, mix_norm, ffn2_norm, ffn2_wg, ffn2_wu, ffn2_wd, final_norm, ev_w_in, ev_w_out, diff_lam, diff_subln, na_rpb, od_w_in, od_w_out, conv_w, conv_b, rg_wa, rg_ba, rg_wx, rg_bx, rg_lam, mla_gq, mla_gkv, mla_wuq, mla_wukv):
    row = lambda v: v.reshape(1, -1).astype(F32)
    bf = lambda w: w.astype(BF16)
    xt = x.reshape(TOKENS, D_MODEL)

    ffn1 = (bf(ffn1_wg), bf(ffn1_wu), bf(ffn1_wd))
    ffn2 = (bf(ffn2_wg), bf(ffn2_wu), bf(ffn2_wd))

    xt = _ffn(xt, row(ffn1_norm[0]), *ffn1, 0)
    cos_a, sin_a = _even_rope_tables()
    wv_cols = jnp.concatenate([ev_w_in[0][:, 2 * A_QK:2 * A_QK + A_V], ev_w_in[0][:, EVEN_IN - B_W:]], axis=1)
    qa, ka, vt, qb, kb = _even_in(xt, row(mix_norm[0]), bf(ev_w_in[0]), bf(wv_cols.T), cos_a, sin_a)
    lam_init0 = 0.8 - 0.6 * math.exp(-0.3 * 0)
    oa = _diff_attn(qa, ka, vt, diff_lam[0].astype(F32), row(diff_subln[0]), lam_init0)
    ob = _na_attn(qb, kb, vt, _na_bias_table(na_rpb[0]))
    xt = _ffn(xt, row(ffn2_norm[0]), *ffn2, 0, mix=(oa, ob, bf(ev_w_out[0])))

    xt = _ffn(xt, row(ffn1_norm[1]), *ffn1, 1)
    cos_d, sin_d = _odd_rope_tables()
    w_in_p, wuq_p, wk_p, wvt_p = _odd_weights(od_w_in[0], mla_wuq[0], mla_wukv[0])
    xc, gc, q, k, vt = _odd_in(xt, row(mix_norm[1]), w_in_p, row(mla_gq[0]), row(mla_gkv[0]),
                               wuq_p, wk_p, wvt_p, cos_d, sin_d)
    wgate, bgate, lam = _rglru_gate_params(rg_wa[0], rg_ba[0], rg_wx[0], rg_bx[0], rg_lam[0])
    oc = _rglru(xc, gc, conv_w[0].astype(F32), row(conv_b[0]), wgate, bgate, lam)
    od = _mla_attn(q, k, vt)
    xt = _ffn(xt, row(ffn2_norm[1]), *ffn2, 1, mix=(oc, od, bf(od_w_out[0])), final_g=row(final_norm))
    return xt.reshape(BATCH, SEQ, D_MODEL)
```
